```python
import math
import jax, jax.numpy as jnp
from jax import lax
import numpy as np

D_MODEL = 1024
BATCH = 2
SEQ = 8192
DEPTH = 1
DEC_BATCH = 8
DEC_SEQ = 2048
PAST_LEN = 128

N_MEM = 256
HEAD_DIM = 64
ATT_Q_HEADS = 8
ATT_KV_HEADS = 2
ATT_GROUP = ATT_Q_HEADS // ATT_KV_HEADS
WINDOW = 128
BLOCK = 128
RWKV_HEADS = 8
D_ATT = ATT_Q_HEADS * HEAD_DIM
D_KV = ATT_KV_HEADS * HEAD_DIM
D_RWKV = RWKV_HEADS * HEAD_DIM
D_MIX = D_ATT + D_RWKV
W_LORA = 64
A_LORA = 64
G_LORA = 128
D_RWKV_IN = 3 * D_RWKV + W_LORA + A_LORA + G_LORA
D_IN = D_ATT + 2 * D_KV + D_RWKV_IN
CROSS_HEADS = 4
CROSS_HEAD_DIM = 128
D_CROSS = CROSS_HEADS * CROSS_HEAD_DIM
N_EXPERT_GROUPS = 4
EXPERTS_PER_GROUP = 4
N_EXPERTS = N_EXPERT_GROUPS * EXPERTS_PER_GROUP
TOP_K = 2
D_EXPERT = 512
LN_EPS = 1e-5
GN_EPS = 64e-5
DEEPNORM_ALPHA = (2 * DEPTH) ** 0.25
DEEPNORM_BETA = (8 * DEPTH) ** -0.25
NEG_INF = -1e30

kernel_name = 'hymba_swa_rwkv7_hmoe_encoder'


def layer_norm(x, g, b, eps=LN_EPS):
    xf = x.astype(jnp.float32)
    mu = jnp.mean(xf, -1, keepdims=True)
    var = jnp.mean(jnp.square(xf - mu), -1, keepdims=True)
    return ((xf - mu) * lax.rsqrt(var + eps) * g + b).astype(x.dtype)


def alibi_slopes(n):
    return jnp.exp2(-8.0 / n * jnp.arange(1, n + 1, dtype=jnp.float32))


def windowed_gqa(q, k, v, sink):
    B, T = q.shape[:2]
    nb = T // BLOCK
    qb = q.reshape(B, nb, BLOCK, ATT_KV_HEADS, ATT_GROUP, HEAD_DIM)

    def band(z):
        zp = jnp.pad(z, ((0, 0), (BLOCK, BLOCK), (0, 0), (0, 0)))
        zp = zp.reshape(B, nb + 2, BLOCK, ATT_KV_HEADS, HEAD_DIM)
        return jnp.concatenate([zp[:, :-2], zp[:, 1:-1], zp[:, 2:]], axis=2)

    kb, vb = band(k), band(v)
    s = jnp.einsum('bnqhgd,bnkhd->bnhgqk', qb, kb, preferred_element_type=jnp.float32) * (HEAD_DIM ** -0.5)
    qi = jnp.arange(BLOCK)[:, None]
    ki = jnp.arange(3 * BLOCK)[None, :]
    dist = qi - ki + BLOCK
    spos = jnp.arange(nb)[:, None, None] * BLOCK + ki[None] - BLOCK
    valid = (jnp.abs(dist)[None] <= WINDOW) & (spos >= 0) & (spos < T)
    slopes = alibi_slopes(ATT_Q_HEADS).reshape(ATT_KV_HEADS, ATT_GROUP)
    s = s - slopes[:, :, None, None] * jnp.abs(dist).astype(jnp.float32)
    s = jnp.where(valid[None, :, None, None], s, NEG_INF)
    sink_l = jnp.broadcast_to(sink.astype(jnp.float32).reshape(1, 1, ATT_KV_HEADS, ATT_GROUP, 1, 1), s.shape[:-1] + (1,))
    p = jax.nn.softmax(jnp.concatenate([s, sink_l], -1), axis=-1)[..., :-1]
    o = jnp.einsum('bnhgqk,bnkhd->bnqhgd', p.astype(v.dtype), vb)
    return o.reshape(B, T, D_ATT)


def centred_shift(p, mu):
    prev = jnp.pad(p, ((0, 0), (1, 0), (0, 0)))[:, :-1]
    nxt = jnp.pad(p, ((0, 0), (0, 1), (0, 0)))[:, 1:]
    return p + mu[0] * (prev - p) + mu[1] * (nxt - p)


def rwkv7_scan(r, w, k, v, a, b, reverse):
    def step(S, inp):
        r_t, w_t, k_t, v_t, a_t, b_t = inp
        sa = jnp.einsum('bhij,bhj->bhi', S, a_t)
        S = S * w_t[:, :, None, :] + sa[..., None] * b_t[:, :, None, :] + v_t[..., None] * k_t[:, :, None, :]
        return S, jnp.einsum('bhij,bhj->bhi', S, r_t)

    B, T, H, N = r.shape
    S0 = jnp.zeros((B, H, N, N), jnp.float32)
    xs = tuple(jnp.swapaxes(z.astype(jnp.float32), 0, 1) for z in (r, w, k, v, a, b))
    _, y = lax.scan(step, S0, xs, reverse=reverse)
    return jnp.swapaxes(y, 0, 1)


def rwkv7_bidir(z, mu, w0, w_up, a0, a_up, g_up, k_k, k_a, r_k, gn_g, gn_b):
    B, T = z.shape[:2]
    z = centred_shift(z, mu)
    o1, o2, o3 = D_RWKV, 2 * D_RWKV, 3 * D_RWKV
    r, k, v = z[..., :o1], z[..., o1:o2], z[..., o2:o3]
    wd = z[..., o3:o3 + W_LORA]
    ad = z[..., o3 + W_LORA:o3 + W_LORA + A_LORA]
    gd = z[..., o3 + W_LORA + A_LORA:]
    hs = lambda t: t.reshape(B, T, RWKV_HEADS, HEAD_DIM)
    kk = hs(k * k_k).astype(jnp.float32)
    kk = kk * lax.rsqrt(jnp.sum(kk * kk, -1, keepdims=True) + 1e-12)
    g = jax.nn.sigmoid(gd) @ g_up
    wt = jnp.tanh(wd)
    ys = []
    for d in range(2):
        w_log = -jax.nn.softplus(-(w0[d] + wt @ w_up[d])) - 0.5
        decay = jnp.exp(-jnp.exp(w_log.astype(jnp.float32)))
        a = jax.nn.sigmoid(a0[d] + ad @ a_up[d])
        kd = k * (1 + (a - 1) * k_a)
        ys.append(rwkv7_scan(hs(r), hs(decay), hs(kd), hs(v), -kk, kk * hs(a).astype(jnp.float32), reverse=(d == 1)))
    y = ys[0] + ys[1]
    mu_y = jnp.mean(y, -1, keepdims=True)
    var_y = jnp.mean(jnp.square(y - mu_y), -1, keepdims=True)
    yn = ((y - mu_y) * lax.rsqrt(var_y + GN_EPS)).reshape(B, T, D_RWKV) * gn_g + gn_b
    bonus = (jnp.sum(hs(r * k * r_k), -1, keepdims=True) * hs(v)).reshape(B, T, D_RWKV)
    return ((yn + bonus) * g).astype(z.dtype)


def memory_cross_attn(x, mem, mem_ln_g, mem_ln_b, w_cq, w_ckv, w_co):
    B, T, _ = x.shape
    M = mem.shape[1]
    m = layer_norm(mem, mem_ln_g, mem_ln_b)
    q = (x @ w_cq).reshape(B, T, CROSS_HEADS, CROSS_HEAD_DIM)
    kv = m @ w_ckv
    k = kv[..., :D_CROSS].reshape(B, M, CROSS_HEADS, CROSS_HEAD_DIM)
    v = kv[..., D_CROSS:].reshape(B, M, CROSS_HEADS, CROSS_HEAD_DIM)
    s = jnp.einsum('bthd,bmhd->bhtm', q, k, preferred_element_type=jnp.float32) * (CROSS_HEAD_DIM ** -0.5)
    p = jax.nn.softmax(s, axis=-1)
    o = jnp.einsum('bhtm,bmhd->bthd', p.astype(v.dtype), v).reshape(B, T, D_CROSS)
    return o @ w_co


def hierarchical_moe(x, w_rg, b_rg, w_re, b_re, w_gate, w_up, w_down):
    B, T, D = x.shape
    xf = x.reshape(-1, D)
    pg = jax.nn.softmax((xf @ w_rg + b_rg).astype(jnp.float32), axis=-1)
    gi = jnp.argmax(pg, axis=-1)
    pg_top = jnp.max(pg, axis=-1)
    le = (xf @ w_re + b_re).astype(jnp.float32).reshape(-1, N_EXPERT_GROUPS, EXPERTS_PER_GROUP)
    le = jnp.take_along_axis(le, gi[:, None, None], axis=1)[:, 0]
    pe = jax.nn.softmax(le, axis=-1)
    top_p, top_i = lax.top_k(pe, TOP_K)
    top_w = top_p / jnp.sum(top_p, -1, keepdims=True) * pg_top[:, None]
    eidx = gi[:, None] * EXPERTS_PER_GROUP + top_i
    combine = jnp.sum(jax.nn.one_hot(eidx, N_EXPERTS, dtype=jnp.float32) * top_w[..., None], axis=1)
    combine = combine.astype(x.dtype)
    y = jnp.zeros_like(xf)
    for e in range(N_EXPERTS):
        h = jax.nn.silu(xf @ w_gate[e]) * (xf @ w_up[e])
        y = y + combine[:, e:e + 1] * (h @ w_down[e])
    return y.reshape(B, T, D)


def encoder_layer(x, mem, w_in, tshift_mu, attn_sink, rwkv_w0, rwkv_w_up, rwkv_a0, rwkv_a_up, rwkv_g_up,
                  rwkv_k_k, rwkv_k_a, rwkv_r_k, rwkv_gn_g, rwkv_gn_b, w_out, ln1_g, ln1_b,
                  mem_ln_g, mem_ln_b, w_cq, w_ckv, w_co, ln2_g, ln2_b,
                  w_route_group, b_route_group, w_route_expert, b_route_expert,
                  w_exp_gate, w_exp_up, w_exp_down, ln3_g, ln3_b):
    B, T, _ = x.shape
    z = x @ w_in
    q = z[..., :D_ATT].reshape(B, T, ATT_Q_HEADS, HEAD_DIM)
    k = z[..., D_ATT:D_ATT + D_KV].reshape(B, T, ATT_KV_HEADS, HEAD_DIM)
    v = z[..., D_ATT + D_KV:D_ATT + 2 * D_KV].reshape(B, T, ATT_KV_HEADS, HEAD_DIM)
    zr = z[..., D_ATT + 2 * D_KV:]
    att = windowed_gqa(q, k, v, attn_sink)
    rw = rwkv7_bidir(zr, tshift_mu, rwkv_w0, rwkv_w_up, rwkv_a0, rwkv_a_up, rwkv_g_up,
                     rwkv_k_k, rwkv_k_a, rwkv_r_k, rwkv_gn_g, rwkv_gn_b)
    mix = jnp.concatenate([att, rw.astype(att.dtype)], axis=-1) @ w_out
    x = layer_norm(DEEPNORM_ALPHA * x + mix, ln1_g, ln1_b)
    cr = memory_cross_attn(x, mem, mem_ln_g, mem_ln_b, w_cq, w_ckv, w_co)
    x = layer_norm(DEEPNORM_ALPHA * x + cr, ln2_g, ln2_b)
    ff = hierarchical_moe(x, w_route_group, b_route_group, w_route_expert, b_route_expert,
                          w_exp_gate, w_exp_up, w_exp_down)
    return layer_norm(DEEPNORM_ALPHA * x + ff, ln3_g, ln3_b)


def setup_inputs(seed: int = 0) -> dict:
    key = jax.random.key(seed)
    ks = iter(jax.random.split(key, 48))
    f32 = jnp.float32
    nrm = lambda shape, scale: jax.random.normal(next(ks), shape, f32) * scale
    uni = lambda shape, lo, hi: jax.random.uniform(next(ks), shape, f32, lo, hi)
    L = DEPTH
    beta = DEEPNORM_BETA
    gain = lambda n: 1.0 + nrm((L, n), 0.02)
    bias = lambda n: nrm((L, n), 0.02)
    return {
        'x_prompt': nrm((BATCH, SEQ, D_MODEL), 1.0),
        'x_sample': nrm((DEC_BATCH, DEC_SEQ, D_MODEL), 1.0),
        'mem_prompt': nrm((BATCH, N_MEM, D_MODEL), 1.0),
        'mem_sample': nrm((DEC_BATCH, N_MEM, D_MODEL), 1.0),
        'w_in': nrm((L, D_MODEL, D_IN), D_MODEL ** -0.5),
        'tshift_mu': uni((L, 2, D_RWKV_IN), 0.0, 0.5),
        'attn_sink': nrm((L, ATT_Q_HEADS), 0.5),
        'rwkv_w0': uni((L, 2, D_RWKV), -6.0, 2.0),
        'rwkv_w_up': nrm((L, 2, W_LORA, D_RWKV), 0.1 * W_LORA ** -0.5),
        'rwkv_a0': nrm((L, 2, D_RWKV), 0.5),
        'rwkv_a_up': nrm((L, 2, A_LORA, D_RWKV), A_LORA ** -0.5),
        'rwkv_g_up': nrm((L, G_LORA, D_RWKV), G_LORA ** -0.5),
        'rwkv_k_k': 0.85 + nrm((L, D_RWKV), 0.02),
        'rwkv_k_a': 1.0 + nrm((L, D_RWKV), 0.02),
        'rwkv_r_k': nrm((L, D_RWKV), 0.1),
        'rwkv_gn_g': gain(D_RWKV),
        'rwkv_gn_b': bias(D_RWKV),
        'w_out': nrm((L, D_MIX, D_MODEL), beta * D_MIX ** -0.5),
        'ln1_g': gain(D_MODEL),
        'ln1_b': bias(D_MODEL),
        'mem_ln_g': gain(D_MODEL),
        'mem_ln_b': bias(D_MODEL),
        'w_cq': nrm((L, D_MODEL, D_CROSS), D_MODEL ** -0.5),
        'w_ckv': nrm((L, D_MODEL, 2 * D_CROSS), D_MODEL ** -0.5),
        'w_co': nrm((L, D_CROSS, D_MODEL), beta * D_CROSS ** -0.5),
        'ln2_g': gain(D_MODEL),
        'ln2_b': bias(D_MODEL),
        'w_route_group': nrm((L, D_MODEL, N_EXPERT_GROUPS), D_MODEL ** -0.5),
        'b_route_group': nrm((L, N_EXPERT_GROUPS), 0.01),
        'w_route_expert': nrm((L, D_MODEL, N_EXPERTS), D_MODEL ** -0.5),
        'b_route_expert': nrm((L, N_EXPERTS), 0.01),
        'w_exp_gate': nrm((L, N_EXPERTS, D_MODEL, D_EXPERT), D_MODEL ** -0.5),
        'w_exp_up': nrm((L, N_EXPERTS, D_MODEL, D_EXPERT), D_MODEL ** -0.5),
        'w_exp_down': nrm((L, N_EXPERTS, D_EXPERT, D_MODEL), beta * D_EXPERT ** -0.5),
        'ln3_g': gain(D_MODEL),
        'ln3_b': bias(D_MODEL),
    }


def reference(x_prompt, x_sample, mem_prompt, mem_sample, w_in, tshift_mu, attn_sink, rwkv_w0, rwkv_w_up,
              rwkv_a0, rwkv_a_up, rwkv_g_up, rwkv_k_k, rwkv_k_a, rwkv_r_k, rwkv_gn_g, rwkv_gn_b, w_out,
              ln1_g, ln1_b, mem_ln_g, mem_ln_b, w_cq, w_ckv, w_co, ln2_g, ln2_b,
              w_route_group, b_route_group, w_route_expert, b_route_expert,
              w_exp_gate, w_exp_up, w_exp_down, ln3_g, ln3_b):
    weights = (w_in, tshift_mu, attn_sink, rwkv_w0, rwkv_w_up, rwkv_a0, rwkv_a_up, rwkv_g_up,
               rwkv_k_k, rwkv_k_a, rwkv_r_k, rwkv_gn_g, rwkv_gn_b, w_out, ln1_g, ln1_b,
               mem_ln_g, mem_ln_b, w_cq, w_ckv, w_co, ln2_g, ln2_b,
               w_route_group, b_route_group, w_route_expert, b_route_expert,
               w_exp_gate, w_exp_up, w_exp_down, ln3_g, ln3_b)

    def trunk(x, mem):
        for l in range(DEPTH):
            x = encoder_layer(x, mem, *[w[l] for w in weights])
        return x

    y_prompt = trunk(x_prompt, mem_prompt)
    y_sample = trunk(x_sample, mem_sample)
    return (y_prompt, y_sample)
```

```python
import functools

import jax
import jax.numpy as jnp
from jax import lax
from jax.experimental import pallas as pl
from jax.experimental.pallas import tpu as pltpu

F32 = jnp.float32
BF16 = jnp.bfloat16
HI = lax.Precision.HIGHEST

D_MODEL = 1024
HEAD_DIM = 64
ATT_Q_HEADS = 8
ATT_KV_HEADS = 2
ATT_GROUP = ATT_Q_HEADS // ATT_KV_HEADS
WINDOW = 128
BLOCK = 128
RWKV_HEADS = 8
D_ATT = ATT_Q_HEADS * HEAD_DIM
D_KV = ATT_KV_HEADS * HEAD_DIM
D_QKV = D_ATT + 2 * D_KV
D_RWKV = RWKV_HEADS * HEAD_DIM
W_LORA = 64
A_LORA = 64
G_LORA = 128
D_RWKV_IN = 3 * D_RWKV + W_LORA + A_LORA + G_LORA
N_MEM = 256
CROSS_HEADS = 4
CROSS_HEAD_DIM = 128
D_CROSS = CROSS_HEADS * CROSS_HEAD_DIM
N_EXPERT_GROUPS = 4
EXPERTS_PER_GROUP = 4
N_EXPERTS = N_EXPERT_GROUPS * EXPERTS_PER_GROUP
D_EXPERT = 512
LN_EPS = 1e-5
GN_EPS = 64e-5
DEEPNORM_ALPHA = 2.0 ** 0.25
NEG_INF = -1e30
ROUTE_LANES = 128
CHUNK = 64
VMEM_LIMIT = 56 * 1024 * 1024


def _cparams(*sem):
    return pltpu.CompilerParams(dimension_semantics=sem, vmem_limit_bytes=VMEM_LIMIT)


def _ln(x, g, b):
    mu = jnp.mean(x, -1, keepdims=True)
    xc = x - mu
    var = jnp.mean(xc * xc, -1, keepdims=True)
    return xc * lax.rsqrt(var + LN_EPS) * g + b


def _const_spec(shape):
    nd = len(shape)
    return pl.BlockSpec(shape, lambda *_: (0,) * nd)


def _inproj_kernel(x_ref, wa_ref, wr_ref, qkv_ref, zr_ref):
    xb = x_ref[...].astype(BF16)
    qkv_ref[...] = jnp.dot(xb, wa_ref[...], preferred_element_type=F32).astype(BF16)
    zr_ref[...] = jnp.dot(xb, wr_ref[...], preferred_element_type=F32)


def _inproj(x2d, w_att, w_rw, tm):
    n = x2d.shape[0]
    return pl.pallas_call(
        _inproj_kernel,
        grid=(n // tm,),
        in_specs=[pl.BlockSpec((tm, D_MODEL), lambda i: (i, 0)),
                  _const_spec((D_MODEL, D_QKV)),
                  _const_spec((D_MODEL, D_RWKV_IN))],
        out_specs=[pl.BlockSpec((tm, D_QKV), lambda i: (i, 0)),
                   pl.BlockSpec((tm, D_RWKV_IN), lambda i: (i, 0))],
        out_shape=[jax.ShapeDtypeStruct((n, D_QKV), BF16),
                   jax.ShapeDtypeStruct((n, D_RWKV_IN), F32)],
        compiler_params=_cparams("arbitrary"),
        name="inproj",
    )(x2d, w_att, w_rw)


def _attn_kernel(sink_ref, cur_ref, prv_ref, nxt_ref, o_ref, *, nb):
    n = pl.program_id(1)
    cur = cur_ref[0]
    prv = prv_ref[0]
    nxt = nxt_ref[0]
    qi = lax.broadcasted_iota(jnp.int32, (BLOCK, 3 * BLOCK), 0)
    ki = lax.broadcasted_iota(jnp.int32, (BLOCK, 3 * BLOCK), 1)
    adist = jnp.abs(qi - ki + BLOCK)
    valid = (adist <= WINDOW) & ((ki >= BLOCK) | (n > 0)) & ((ki < 2 * BLOCK) | (n < nb - 1))
    adist_f = adist.astype(F32)
    for h in range(ATT_KV_HEADS):
        k0 = D_ATT + h * HEAD_DIM
        v0 = D_ATT + D_KV + h * HEAD_DIM
        kcat = jnp.concatenate([prv[:, k0:k0 + HEAD_DIM], cur[:, k0:k0 + HEAD_DIM], nxt[:, k0:k0 + HEAD_DIM]], axis=0)
        vcat = jnp.concatenate([prv[:, v0:v0 + HEAD_DIM], cur[:, v0:v0 + HEAD_DIM], nxt[:, v0:v0 + HEAD_DIM]], axis=0)
        for g in range(ATT_GROUP):
            hq = h * ATT_GROUP + g
            slope = 2.0 ** (-8.0 / ATT_Q_HEADS * (hq + 1))
            q = cur[:, hq * HEAD_DIM:(hq + 1) * HEAD_DIM] * (HEAD_DIM ** -0.5)
            s = lax.dot_general(q, kcat, (((1,), (1,)), ((), ())), preferred_element_type=F32)
            s = jnp.where(valid, s - slope * adist_f, NEG_INF)
            sk = sink_ref[hq]
            m = jnp.maximum(jnp.max(s, -1, keepdims=True), sk)
            p = jnp.exp(s - m)
            den = jnp.sum(p, -1, keepdims=True) + jnp.exp(sk - m)
            o = jnp.dot(p.astype(BF16), vcat, preferred_element_type=F32) / den
            o_ref[0, :, hq * HEAD_DIM:(hq + 1) * HEAD_DIM] = o.astype(BF16)


def _attention(qkv, sink):
    b, t, _ = qkv.shape
    nb = t // BLOCK
    return pl.pallas_call(
        functools.partial(_attn_kernel, nb=nb),
        grid=(b, nb),
        in_specs=[pl.BlockSpec(memory_space=pltpu.SMEM),
                  pl.BlockSpec((1, BLOCK, D_QKV), lambda i, j: (i, j, 0)),
                  pl.BlockSpec((1, BLOCK, D_QKV), lambda i, j: (i, jnp.maximum(j - 1, 0), 0)),
                  pl.BlockSpec((1, BLOCK, D_QKV), lambda i, j: (i, jnp.minimum(j + 1, nb - 1), 0))],
        out_specs=pl.BlockSpec((1, BLOCK, D_ATT), lambda i, j: (i, j, 0)),
        out_shape=jax.ShapeDtypeStruct((b, t, D_ATT), BF16),
        compiler_params=_cparams("arbitrary", "arbitrary"),
        name="win_attn",
    )(sink, qkv, qkv, qkv)


def _heads(x):
    return jnp.stack([x[:, h * HEAD_DIM:(h + 1) * HEAD_DIM] for h in range(RWKV_HEADS)], axis=0)


def _bdot(a, b):
    return jnp.einsum('hlj,hjm->hlm', a.astype(BF16), b.astype(BF16), preferred_element_type=F32)


def _bdot_nt(a, b):
    return jnp.einsum('hlc,hjc->hlj', a.astype(BF16), b.astype(BF16), preferred_element_type=F32)


def _bdot_tn(a, b):
    return jnp.einsum('hlc,hlv->hcv', a.astype(BF16), b.astype(BF16), preferred_element_type=F32)


def _rwkv_kernel(*refs, reverse, lt, nt, final):
    if final:
        (zc_ref, zp_ref, zn_ref, mu_ref, w0_ref, wup_ref, a0_ref, aup_ref, gup_ref, kk_ref, ka_ref, rk_ref,
         gng_ref, gnb_ref, bd_ref, yf_ref, out_ref, st_ref, y_scr) = refs
    else:
        (zc_ref, zp_ref, zn_ref, mu_ref, w0_ref, wup_ref, a0_ref, aup_ref, kk_ref, ka_ref, bd_ref,
         out_ref, st_ref, y_scr) = refs
    i = pl.program_id(1)
    tt = (nt - 1 - i) if reverse else i

    @pl.when(i == 0)
    def _():
        st_ref[...] = jnp.zeros_like(st_ref)

    z = zc_ref[0]
    row = lax.broadcasted_iota(jnp.int32, (lt, 1), 0)
    zprev_edge = jnp.where(tt > 0, zp_ref[0, 7:8, :], 0.0)
    znext_edge = jnp.where(tt < nt - 1, zn_ref[0, 0:1, :], 0.0)
    prev = jnp.where(row == 0, zprev_edge, pltpu.roll(z, 1, 0))
    nxt = jnp.where(row == lt - 1, znext_edge, pltpu.roll(z, lt - 1, 0))
    zs = z + mu_ref[0:1, :] * (prev - z) + mu_ref[1:2, :] * (nxt - z)

    o1, o2, o3 = D_RWKV, 2 * D_RWKV, 3 * D_RWKV
    r = zs[:, :o1]
    k = zs[:, o1:o2]
    v = zs[:, o2:o3]
    wad = zs[:, o3:o3 + W_LORA + A_LORA]
    bd = bd_ref[...]

    def head_sum(x):
        return jnp.dot(x, bd, precision=HI, preferred_element_type=F32)

    kk = k * kk_ref[...]
    kk = kk * lax.rsqrt(head_sum(kk * kk) + 1e-12)
    w_arg = w0_ref[...] + jnp.dot(jnp.tanh(wad).astype(BF16), wup_ref[...], preferred_element_type=F32)
    lw = -jnp.exp(-jax.nn.softplus(-w_arg) - 0.5)
    a_sig = jax.nn.sigmoid(a0_ref[...] + jnp.dot(wad.astype(BF16), aup_ref[...], preferred_element_type=F32))
    kd = k * (1.0 + (a_sig - 1.0) * ka_ref[...])
    a_vec = -kk
    b_vec = kk * a_sig

    ti = lax.broadcasted_iota(jnp.int32, (CHUNK, CHUNK), 0)
    tj = lax.broadcasted_iota(jnp.int32, (CHUNK, CHUNK), 1)
    if reverse:
        m_strict, m_incl = ti < tj, ti <= tj
    else:
        m_strict, m_incl = ti > tj, ti >= tj
    tri = m_incl.astype(F32)
    eye = (ti == tj).astype(F32)
    last = 0 if reverse else CHUNK - 1

    nc = lt // CHUNK
    order = range(nc - 1, -1, -1) if reverse else range(nc)
    for c in order:
        sl = slice(c * CHUNK, (c + 1) * CHUNK)
        lw_c = lw[sl]
        cum = jnp.dot(tri, lw_c, precision=HI, preferred_element_type=F32)
        e_pos = jnp.exp(cum)
        e_neg = jnp.exp(-cum)
        e_prev = jnp.exp(cum - lw_c)
        g_l = e_pos[last:last + 1, :]
        at = _heads(a_vec[sl] * e_prev)
        rt = _heads(r[sl] * e_pos)
        bt_full = b_vec[sl] * e_neg
        kt_full = kd[sl] * e_neg
        bt = _heads(bt_full)
        kt = _heads(kt_full)
        bh = _heads(bt_full * g_l)
        kh = _heads(kt_full * g_l)
        vh = _heads(v[sl])
        gl_h = _heads(g_l)

        m_ab = jnp.where(m_strict, _bdot_nt(at, bt), 0.0)
        m_ak = jnp.where(m_strict, _bdot_nt(at, kt), 0.0)
        q_b = jnp.where(m_incl, _bdot_nt(rt, bt), 0.0)
        q_k = jnp.where(m_incl, _bdot_nt(rt, kt), 0.0)
        tinv = eye + m_ab
        mp = m_ab
        for _ in range(5):
            mp = _bdot(mp, mp)
            tinv = tinv + _bdot(tinv, mp)
        a_hat = _bdot(tinv, at)
        u_v = _bdot(tinv, _bdot(m_ak, vh))

        st = st_ref[...]
        u = _bdot(a_hat, st) + u_v
        y = _bdot(rt, st) + _bdot(q_b, u) + _bdot(q_k, vh)
        decay = jnp.einsum('hck,hkv->hcv', eye * gl_h, st, precision=HI, preferred_element_type=F32)
        st_ref[...] = decay + _bdot_tn(bh, u) + _bdot_tn(kh, vh)
        for h in range(RWKV_HEADS):
            y_scr[sl, h * HEAD_DIM:(h + 1) * HEAD_DIM] = y[h]

    if not final:
        out_ref[0] = y_scr[...]
    else:
        yy = yf_ref[0] + y_scr[...]
        mu_y = head_sum(yy) * (1.0 / HEAD_DIM)
        yc = yy - mu_y
        var_y = head_sum(yc * yc) * (1.0 / HEAD_DIM)
        yn = yc * lax.rsqrt(var_y + GN_EPS) * gng_ref[...] + gnb_ref[...]
        bonus = head_sum(r * k * rk_ref[...]) * v
        gd = zs[:, o3 + W_LORA + A_LORA:]
        gate = jnp.dot(jax.nn.sigmoid(gd).astype(BF16), gup_ref[...], preferred_element_type=F32)
        out_ref[0] = ((yn + bonus) * gate).astype(BF16)


def _rwkv_dir(zr, y_fwd, p, d, lt):
    b, t, _ = zr.shape
    nt = t // lt
    reverse = d == 1
    final = y_fwd is not None
    tmap = (lambda j: nt - 1 - j) if reverse else (lambda j: j)
    r8 = lt // 8
    vec = _const_spec((1, D_RWKV))
    in_specs = [pl.BlockSpec((1, lt, D_RWKV_IN), lambda i, j: (i, tmap(j), 0)),
                pl.BlockSpec((1, 8, D_RWKV_IN), lambda i, j: (i, jnp.maximum(tmap(j) * r8 - 1, 0), 0)),
                pl.BlockSpec((1, 8, D_RWKV_IN), lambda i, j: (i, jnp.minimum((tmap(j) + 1) * r8, t // 8 - 1), 0)),
                _const_spec((2, D_RWKV_IN)), vec, _const_spec((W_LORA + A_LORA, D_RWKV)), vec,
                _const_spec((W_LORA + A_LORA, D_RWKV))]
    args = [zr, zr, zr, p['mu'], p['w0'][d], p['rw_w_up'][d], p['a0'][d], p['rw_a_up'][d]]
    if final:
        in_specs += [_const_spec((G_LORA, D_RWKV)), vec, vec, vec, vec, vec, _const_spec((D_RWKV, D_RWKV)),
                     pl.BlockSpec((1, lt, D_RWKV), lambda i, j: (i, tmap(j), 0))]
        args += [p['g_up'], p['k_k'], p['k_a'], p['r_k'], p['gn_g'], p['gn_b'], p['bd'], y_fwd]
    else:
        in_specs += [vec, vec, _const_spec((D_RWKV, D_RWKV))]
        args += [p['k_k'], p['k_a'], p['bd']]
    return pl.pallas_call(
        functools.partial(_rwkv_kernel, reverse=reverse, lt=lt, nt=nt, final=final),
        grid=(b, nt),
        in_specs=in_specs,
        out_specs=pl.BlockSpec((1, lt, D_RWKV), lambda i, j: (i, tmap(j), 0)),
        out_shape=jax.ShapeDtypeStruct((b, t, D_RWKV), BF16 if final else F32),
        scratch_shapes=[pltpu.VMEM((RWKV_HEADS, HEAD_DIM, HEAD_DIM), F32),
                        pltpu.VMEM((lt, D_RWKV), F32)],
        compiler_params=_cparams("arbitrary", "arbitrary"),
        name="rwkv_bwd_final" if final else "rwkv_fwd",
    )(*args)


def _memkv_kernel(m_ref, g_ref, b_ref, w_ref, kv_ref):
    m = _ln(m_ref[...], g_ref[...], b_ref[...])
    kv_ref[...] = jnp.dot(m.astype(BF16), w_ref[...], preferred_element_type=F32).astype(BF16)


def _memkv(mem2d, g, b, w_ckv):
    n = mem2d.shape[0]
    return pl.pallas_call(
        _memkv_kernel,
        grid=(n // N_MEM,),
        in_specs=[pl.BlockSpec((N_MEM, D_MODEL), lambda i: (i, 0)),
                  _const_spec((1, D_MODEL)), _const_spec((1, D_MODEL)),
                  _const_spec((D_MODEL, 2 * D_CROSS))],
        out_specs=pl.BlockSpec((N_MEM, 2 * D_CROSS), lambda i: (i, 0)),
        out_shape=jax.ShapeDtypeStruct((n, 2 * D_CROSS), BF16),
        compiler_params=_cparams("arbitrary"),
        name="mem_kv",
    )(mem2d, g, b, w_ckv)


def _mid_kernel(x_ref, att_ref, rw_ref, kv_ref, wo_ref, l1g_ref, l1b_ref, wq_ref, wco_ref, l2g_ref, l2b_ref,
                wr_ref, br_ref, x2_ref, comb_ref):
    mix = (jnp.dot(att_ref[...], wo_ref[:D_ATT, :], preferred_element_type=F32)
           + jnp.dot(rw_ref[...], wo_ref[D_ATT:, :], preferred_element_type=F32))
    x1 = _ln(DEEPNORM_ALPHA * x_ref[...] + mix, l1g_ref[...], l1b_ref[...])

    q = jnp.dot(x1.astype(BF16), wq_ref[...], preferred_element_type=F32)
    q = (q * (CROSS_HEAD_DIM ** -0.5)).astype(BF16)
    kv = kv_ref[...]
    cr = None
    for h in range(CROSS_HEADS):
        hs = slice(h * CROSS_HEAD_DIM, (h + 1) * CROSS_HEAD_DIM)
        kh = kv[:, h * CROSS_HEAD_DIM:(h + 1) * CROSS_HEAD_DIM]
        vh = kv[:, D_CROSS + h * CROSS_HEAD_DIM:D_CROSS + (h + 1) * CROSS_HEAD_DIM]
        s = lax.dot_general(q[:, hs], kh, (((1,), (1,)), ((), ())), preferred_element_type=F32)
        m = jnp.max(s, -1, keepdims=True)
        p = jnp.exp(s - m)
        den = jnp.sum(p, -1, keepdims=True)
        o = jnp.dot(p.astype(BF16), vh, preferred_element_type=F32) / den
        part = jnp.dot(o.astype(BF16), wco_ref[hs, :], preferred_element_type=F32)
        cr = part if cr is None else cr + part
    x2 = _ln(DEEPNORM_ALPHA * x1 + cr, l2g_ref[...], l2b_ref[...])
    x2_ref[...] = x2

    logits = jnp.dot(x2, wr_ref[...], precision=HI, preferred_element_type=F32) + br_ref[...]
    lane = lax.broadcasted_iota(jnp.int32, logits.shape, 1)
    is_g = (lane >= N_EXPERTS) & (lane < N_EXPERTS + N_EXPERT_GROUPS)
    lg = jnp.where(is_g, logits, NEG_INF)
    eg = jnp.where(is_g, jnp.exp(lg - jnp.max(lg, -1, keepdims=True)), 0.0)
    pg = eg / jnp.sum(eg, -1, keepdims=True)
    pg_top = jnp.max(pg, -1, keepdims=True)
    gi = jnp.min(jnp.where(is_g & (pg == pg_top), lane, 4 * ROUTE_LANES), -1, keepdims=True) - N_EXPERTS
    in_grp = (lane >= gi * EXPERTS_PER_GROUP) & (lane < (gi + 1) * EXPERTS_PER_GROUP)
    le = jnp.where(in_grp, logits, NEG_INF)
    ee = jnp.where(in_grp, jnp.exp(le - jnp.max(le, -1, keepdims=True)), 0.0)
    pe = ee / jnp.sum(ee, -1, keepdims=True)
    p1 = jnp.max(jnp.where(in_grp, pe, -1.0), -1, keepdims=True)
    i1 = jnp.min(jnp.where(in_grp & (pe == p1), lane, 4 * ROUTE_LANES), -1, keepdims=True)
    rest = in_grp & (lane != i1)
    p2 = jnp.max(jnp.where(rest, pe, -1.0), -1, keepdims=True)
    i2 = jnp.min(jnp.where(rest & (pe == p2), lane, 4 * ROUTE_LANES), -1, keepdims=True)
    tot = p1 + p2
    comb_ref[...] = (jnp.where(lane == i1, p1 / tot * pg_top, 0.0)
                     + jnp.where(lane == i2, p2 / tot * pg_top, 0.0))


def _mid(x2d, att2d, rw2d, kv, p, tm, t):
    n = x2d.shape[0]
    per_b = t // tm
    vec = _const_spec((1, D_MODEL))
    return pl.pallas_call(
        _mid_kernel,
        grid=(n // tm,),
        in_specs=[pl.BlockSpec((tm, D_MODEL), lambda i: (i, 0)),
                  pl.BlockSpec((tm, D_ATT), lambda i: (i, 0)),
                  pl.BlockSpec((tm, D_RWKV), lambda i: (i, 0)),
                  pl.BlockSpec((N_MEM, 2 * D_CROSS), lambda i: (i // per_b, 0)),
                  _const_spec((D_MODEL, D_MODEL)), vec, vec,
                  _const_spec((D_MODEL, D_CROSS)), _const_spec((D_CROSS, D_MODEL)), vec, vec,
                  _const_spec((D_MODEL, ROUTE_LANES)), _const_spec((1, ROUTE_LANES))],
        out_specs=[pl.BlockSpec((tm, D_MODEL), lambda i: (i, 0)),
                   pl.BlockSpec((tm, ROUTE_LANES), lambda i: (i, 0))],
        out_shape=[jax.ShapeDtypeStruct((n, D_MODEL), F32),
                   jax.ShapeDtypeStruct((n, ROUTE_LANES), F32)],
        compiler_params=_cparams("arbitrary"),
        name="mid",
    )(x2d, att2d, rw2d, kv, p['w_out'], p['ln1_g'], p['ln1_b'], p['w_cq'], p['w_co'], p['ln2_g'], p['ln2_b'],
      p['w_route'], p['b_route'])


def _moe_kernel(x_ref, comb_ref, wg_ref, wu_ref, wd_ref, l3g_ref, l3b_ref, o_ref, acc_ref):
    e = pl.program_id(1)

    @pl.when(e == 0)
    def _():
        acc_ref[...] = jnp.zeros_like(acc_ref)

    xb = x_ref[...].astype(BF16)
    gate = jnp.dot(xb, wg_ref[0], preferred_element_type=F32)
    up = jnp.dot(xb, wu_ref[0], preferred_element_type=F32)
    hdn = (gate * jax.nn.sigmoid(gate)) * up
    lane = lax.broadcasted_iota(jnp.int32, comb_ref.shape, 1)
    cw = jnp.sum(jnp.where(lane == e, comb_ref[...], 0.0), -1, keepdims=True)
    acc_ref[...] += cw * jnp.dot(hdn.astype(BF16), wd_ref[0], preferred_element_type=F32)

    @pl.when(e == N_EXPERTS - 1)
    def _():
        o_ref[...] = _ln(DEEPNORM_ALPHA * x_ref[...] + acc_ref[...], l3g_ref[...], l3b_ref[...])


def _moe(x2, comb, p, tm):
    n = x2.shape[0]
    vec = _const_spec((1, D_MODEL))
    return pl.pallas_call(
        _moe_kernel,
        grid=(n // tm, N_EXPERTS),
        in_specs=[pl.BlockSpec((tm, D_MODEL), lambda i, e: (i, 0)),
                  pl.BlockSpec((tm, ROUTE_LANES), lambda i, e: (i, 0)),
                  pl.BlockSpec((1, D_MODEL, D_EXPERT), lambda i, e: (e, 0, 0)),
                  pl.BlockSpec((1, D_MODEL, D_EXPERT), lambda i, e: (e, 0, 0)),
                  pl.BlockSpec((1, D_EXPERT, D_MODEL), lambda i, e: (e, 0, 0)),
                  vec, vec],
        out_specs=pl.BlockSpec((tm, D_MODEL), lambda i, e: (i, 0)),
        out_shape=jax.ShapeDtypeStruct((n, D_MODEL), F32),
        scratch_shapes=[pltpu.VMEM((tm, D_MODEL), F32)],
        compiler_params=_cparams("arbitrary", "arbitrary"),
        name="moe",
    )(x2, comb, p['w_gate'], p['w_up'], p['w_down'], p['ln3_g'], p['ln3_b'])


def _prep_params(w_in, tshift_mu, attn_sink, rwkv_w0, rwkv_w_up, rwkv_a0, rwkv_a_up, rwkv_g_up, rwkv_k_k,
                 rwkv_k_a, rwkv_r_k, rwkv_gn_g, rwkv_gn_b, w_out, ln1_g, ln1_b, mem_ln_g, mem_ln_b, w_cq, w_ckv,
                 w_co, ln2_g, ln2_b, w_route_group, b_route_group, w_route_expert, b_route_expert,
                 w_exp_gate, w_exp_up, w_exp_down, ln3_g, ln3_b):
    row = lambda a: a.reshape(1, -1).astype(F32)
    zeros_lora = jnp.zeros((2, W_LORA, D_RWKV), F32)
    hid = jnp.arange(D_RWKV) // HEAD_DIM
    pad = ROUTE_LANES - N_EXPERTS - N_EXPERT_GROUPS
    return {
        'w_in_att': w_in[:, :D_QKV].astype(BF16),
        'w_in_rw': w_in[:, D_QKV:].astype(BF16),
        'sink': attn_sink.astype(F32),
        'mu': tshift_mu.astype(F32),
        'w0': rwkv_w0.reshape(2, 1, D_RWKV).astype(F32),
        'rw_w_up': jnp.concatenate([rwkv_w_up, zeros_lora], axis=1).astype(BF16),
        'a0': rwkv_a0.reshape(2, 1, D_RWKV).astype(F32),
        'rw_a_up': jnp.concatenate([zeros_lora, rwkv_a_up], axis=1).astype(BF16),
        'g_up': rwkv_g_up.astype(BF16),
        'k_k': row(rwkv_k_k), 'k_a': row(rwkv_k_a), 'r_k': row(rwkv_r_k),
        'gn_g': row(rwkv_gn_g), 'gn_b': row(rwkv_gn_b),
        'bd': (hid[:, None] == hid[None, :]).astype(F32),
        'w_out': w_out.astype(BF16),
        'ln1_g': row(ln1_g), 'ln1_b': row(ln1_b),
        'mem_ln_g': row(mem_ln_g), 'mem_ln_b': row(mem_ln_b),
        'w_cq': w_cq.astype(BF16), 'w_ckv': w_ckv.astype(BF16), 'w_co': w_co.astype(BF16),
        'ln2_g': row(ln2_g), 'ln2_b': row(ln2_b),
        'w_route': jnp.pad(jnp.concatenate([w_route_expert, w_route_group], axis=1), ((0, 0), (0, pad))).astype(F32),
        'b_route': jnp.pad(jnp.concatenate([b_route_expert, b_route_group]), (0, pad)).reshape(1, -1).astype(F32),
        'w_gate': w_exp_gate.astype(BF16), 'w_up': w_exp_up.astype(BF16), 'w_down': w_exp_down.astype(BF16),
        'ln3_g': row(ln3_g), 'ln3_b': row(ln3_b),
    }


def _tile(n, pref):
    t = pref
    while n % t:
        t //= 2
    return t


def _layer(x, mem, p):
    b, t, _ = x.shape
    n = b * t
    x2d = x.reshape(n, D_MODEL)
    tm = _tile(t, 512)
    qkv, zr = _inproj(x2d, p['w_in_att'], p['w_in_rw'], tm)
    att = _attention(qkv.reshape(b, t, D_QKV), p['sink'])
    zr3 = zr.reshape(b, t, D_RWKV_IN)
    lt = _tile(t, 128)
    y_fwd = _rwkv_dir(zr3, None, p, 0, lt)
    rw = _rwkv_dir(zr3, y_fwd, p, 1, lt)
    kv = _memkv(mem.reshape(b * N_MEM, D_MODEL), p['mem_ln_g'], p['mem_ln_b'], p['w_ckv'])
    x2, comb = _mid(x2d, att.reshape(n, D_ATT), rw.reshape(n, D_RWKV), kv, p, tm, t)
    y = _moe(x2, comb, p, tm)
    return y.reshape(b, t, D_MODEL)


def kernel(x_prompt, x_sample, mem_prompt, mem_sample, w_in, tshift_mu, attn_sink, rwkv_w0, rwkv_w_up, rwkv_a0, rwkv_a_up, rwkv_g_up, rwkv_k_k, rwkv_k_a, rwkv_r_k, rwkv_gn_g, rwkv_gn_b, w_out, ln1_g, ln1_b, mem_ln_g, mem_ln_b, w_cq, w_ckv, w_co, ln2_g, ln2_b, w_route_group, b_route_group, w_route_expert, b_route_expert, w_exp_gate, w_exp_up, w_exp_down, ln3_g, ln3_b):
    weights = (w_in, tshift_mu, attn_sink, rwkv_w0, rwkv_w_up, rwkv_a0, rwkv_a_up, rwkv_g_up, rwkv_k_k, rwkv_k_a,
               rwkv_r_k, rwkv_gn_g, rwkv_gn_b, w_out, ln1_g, ln1_b, mem_ln_g, mem_ln_b, w_cq, w_ckv, w_co,
               ln2_g, ln2_b, w_route_group, b_route_group, w_route_expert, b_route_expert,
               w_exp_gate, w_exp_up, w_exp_down, ln3_g, ln3_b)
    p = _prep_params(*[w[0] for w in weights])
    return (_layer(x_prompt, mem_prompt, p), _layer(x_sample, mem_sample, p))
```

```python
import functools

import jax
import jax.numpy as jnp
from jax import lax
from jax.experimental import pallas as pl
from jax.experimental.pallas import tpu as pltpu

F32 = jnp.float32
BF16 = jnp.bfloat16
HI = lax.Precision.HIGHEST

D_MODEL = 1024
HEAD_DIM = 64
ATT_Q_HEADS = 8
ATT_KV_HEADS = 2
ATT_GROUP = ATT_Q_HEADS // ATT_KV_HEADS
WINDOW = 128
BLOCK = 128
RWKV_HEADS = 8
D_ATT = ATT_Q_HEADS * HEAD_DIM
D_KV = ATT_KV_HEADS * HEAD_DIM
D_QKV = D_ATT + 2 * D_KV
D_RWKV = RWKV_HEADS * HEAD_DIM
W_LORA = 64
A_LORA = 64
G_LORA = 128
D_RWKV_IN = 3 * D_RWKV + W_LORA + A_LORA + G_LORA
N_MEM = 256
CROSS_HEADS = 4
CROSS_HEAD_DIM = 128
D_CROSS = CROSS_HEADS * CROSS_HEAD_DIM
N_EXPERT_GROUPS = 4
EXPERTS_PER_GROUP = 4
N_EXPERTS = N_EXPERT_GROUPS * EXPERTS_PER_GROUP
D_EXPERT = 512
LN_EPS = 1e-5
GN_EPS = 64e-5
DEEPNORM_ALPHA = 2.0 ** 0.25
NEG_INF = -1e30
ROUTE_LANES = 128
CHUNK = 64
SEG_ALIGN = 16
VMEM_LIMIT = 56 * 1024 * 1024


def _cparams(*sem):
    return pltpu.CompilerParams(dimension_semantics=sem, vmem_limit_bytes=VMEM_LIMIT)


def _ln(x, g, b):
    mu = jnp.mean(x, -1, keepdims=True)
    xc = x - mu
    var = jnp.mean(xc * xc, -1, keepdims=True)
    return xc * lax.rsqrt(var + LN_EPS) * g + b


def _const_spec(shape):
    nd = len(shape)
    return pl.BlockSpec(shape, lambda *_: (0,) * nd)


def _inproj_kernel(x_ref, wa_ref, wr_ref, qkv_ref, zr_ref):
    xb = x_ref[...].astype(BF16)
    qkv_ref[...] = jnp.dot(xb, wa_ref[...], preferred_element_type=F32).astype(BF16)
    zr_ref[...] = jnp.dot(xb, wr_ref[...], preferred_element_type=F32)


def _inproj(x2d, w_att, w_rw, tm):
    n = x2d.shape[0]
    return pl.pallas_call(
        _inproj_kernel,
        grid=(n // tm,),
        in_specs=[pl.BlockSpec((tm, D_MODEL), lambda i: (i, 0)),
                  _const_spec((D_MODEL, D_QKV)),
                  _const_spec((D_MODEL, D_RWKV_IN))],
        out_specs=[pl.BlockSpec((tm, D_QKV), lambda i: (i, 0)),
                   pl.BlockSpec((tm, D_RWKV_IN), lambda i: (i, 0))],
        out_shape=[jax.ShapeDtypeStruct((n, D_QKV), BF16),
                   jax.ShapeDtypeStruct((n, D_RWKV_IN), F32)],
        compiler_params=_cparams("arbitrary"),
        name="inproj",
    )(x2d, w_att, w_rw)


def _attn_kernel(sink_ref, cur_ref, prv_ref, nxt_ref, o_ref, *, nb):
    n = pl.program_id(1)
    cur = cur_ref[0]
    prv = prv_ref[0]
    nxt = nxt_ref[0]
    qi = lax.broadcasted_iota(jnp.int32, (BLOCK, 3 * BLOCK), 0)
    ki = lax.broadcasted_iota(jnp.int32, (BLOCK, 3 * BLOCK), 1)
    adist = jnp.abs(qi - ki + BLOCK)
    valid = (adist <= WINDOW) & ((ki >= BLOCK) | (n > 0)) & ((ki < 2 * BLOCK) | (n < nb - 1))
    adist_f = adist.astype(F32)
    for h in range(ATT_KV_HEADS):
        k0 = D_ATT + h * HEAD_DIM
        v0 = D_ATT + D_KV + h * HEAD_DIM
        kcat = jnp.concatenate([prv[:, k0:k0 + HEAD_DIM], cur[:, k0:k0 + HEAD_DIM], nxt[:, k0:k0 + HEAD_DIM]], axis=0)
        vcat = jnp.concatenate([prv[:, v0:v0 + HEAD_DIM], cur[:, v0:v0 + HEAD_DIM], nxt[:, v0:v0 + HEAD_DIM]], axis=0)
        for g in range(ATT_GROUP):
            hq = h * ATT_GROUP + g
            slope = 2.0 ** (-8.0 / ATT_Q_HEADS * (hq + 1))
            q = cur[:, hq * HEAD_DIM:(hq + 1) * HEAD_DIM] * (HEAD_DIM ** -0.5)
            s = lax.dot_general(q, kcat, (((1,), (1,)), ((), ())), preferred_element_type=F32)
            s = jnp.where(valid, s - slope * adist_f, NEG_INF)
            sk = sink_ref[hq]
            m = jnp.maximum(jnp.max(s, -1, keepdims=True), sk)
            p = jnp.exp(s - m)
            den = jnp.sum(p, -1, keepdims=True) + jnp.exp(sk - m)
            o = jnp.dot(p.astype(BF16), vcat, preferred_element_type=F32) / den
            o_ref[0, :, hq * HEAD_DIM:(hq + 1) * HEAD_DIM] = o.astype(BF16)


def _attention(qkv, sink):
    b, t, _ = qkv.shape
    nb = t // BLOCK
    return pl.pallas_call(
        functools.partial(_attn_kernel, nb=nb),
        grid=(b, nb),
        in_specs=[pl.BlockSpec(memory_space=pltpu.SMEM),
                  pl.BlockSpec((1, BLOCK, D_QKV), lambda i, j: (i, j, 0)),
                  pl.BlockSpec((1, BLOCK, D_QKV), lambda i, j: (i, jnp.maximum(j - 1, 0), 0)),
                  pl.BlockSpec((1, BLOCK, D_QKV), lambda i, j: (i, jnp.minimum(j + 1, nb - 1), 0))],
        out_specs=pl.BlockSpec((1, BLOCK, D_ATT), lambda i, j: (i, j, 0)),
        out_shape=jax.ShapeDtypeStruct((b, t, D_ATT), BF16),
        compiler_params=_cparams("arbitrary", "arbitrary"),
        name="win_attn",
    )(sink, qkv, qkv, qkv)


def _heads(x):
    return jnp.stack([x[:, h * HEAD_DIM:(h + 1) * HEAD_DIM] for h in range(RWKV_HEADS)], axis=0)


def _bdot(a, b):
    return jnp.einsum('hlj,hjm->hlm', a.astype(BF16), b.astype(BF16), preferred_element_type=F32)


def _bdot_nt(a, b):
    return jnp.einsum('hlc,hjc->hlj', a.astype(BF16), b.astype(BF16), preferred_element_type=F32)


def _bdot_tn(a, b):
    return jnp.einsum('hlc,hlv->hcv', a.astype(BF16), b.astype(BF16), preferred_element_type=F32)


def _rwkv_kernel(*refs, reverse, lt, nt, final):
    if final:
        (zc_ref, zp_ref, zn_ref, mu_ref, w0_ref, wup_ref, a0_ref, aup_ref, gup_ref, kk_ref, ka_ref, rk_ref,
         gng_ref, gnb_ref, bd_ref, yf_ref, out_ref, st_ref, y_scr) = refs
    else:
        (zc_ref, zp_ref, zn_ref, mu_ref, w0_ref, wup_ref, a0_ref, aup_ref, kk_ref, ka_ref, bd_ref,
         out_ref, st_ref, y_scr) = refs
    i = pl.program_id(1)
    tt = (nt - 1 - i) if reverse else i

    @pl.when(i == 0)
    def _():
        st_ref[...] = jnp.zeros_like(st_ref)

    z = zc_ref[0]
    row = lax.broadcasted_iota(jnp.int32, (lt, 1), 0)
    zprev_edge = jnp.where(tt > 0, zp_ref[0, 7:8, :], 0.0)
    znext_edge = jnp.where(tt < nt - 1, zn_ref[0, 0:1, :], 0.0)
    prev = jnp.where(row == 0, zprev_edge, pltpu.roll(z, 1, 0))
    nxt = jnp.where(row == lt - 1, znext_edge, pltpu.roll(z, lt - 1, 0))
    zs = z + mu_ref[0:1, :] * (prev - z) + mu_ref[1:2, :] * (nxt - z)

    o1, o2, o3 = D_RWKV, 2 * D_RWKV, 3 * D_RWKV
    r = zs[:, :o1]
    k = zs[:, o1:o2]
    v = zs[:, o2:o3]
    wad = zs[:, o3:o3 + W_LORA + A_LORA]
    bd = bd_ref[...]

    def head_sum(x):
        return jnp.dot(x, bd, precision=HI, preferred_element_type=F32)

    kk = k * kk_ref[...]
    kk = kk * lax.rsqrt(head_sum(kk * kk) + 1e-12)
    w_arg = w0_ref[...] + jnp.dot(jnp.tanh(wad).astype(BF16), wup_ref[...], preferred_element_type=F32)
    lw = -jnp.exp(-jax.nn.softplus(-w_arg) - 0.5)
    a_sig = jax.nn.sigmoid(a0_ref[...] + jnp.dot(wad.astype(BF16), aup_ref[...], preferred_element_type=F32))
    kd = k * (1.0 + (a_sig - 1.0) * ka_ref[...])
    a_vec = -kk
    b_vec = kk * a_sig

    ti = lax.broadcasted_iota(jnp.int32, (CHUNK, CHUNK), 0)
    tj = lax.broadcasted_iota(jnp.int32, (CHUNK, CHUNK), 1)
    if reverse:
        m_strict, m_incl = ti < tj, ti <= tj
    else:
        m_strict, m_incl = ti > tj, ti >= tj
    tri = m_incl.astype(F32)
    eye = (ti == tj).astype(F32)
    last = 0 if reverse else CHUNK - 1

    nc = lt // CHUNK
    order = range(nc - 1, -1, -1) if reverse else range(nc)
    for c in order:
        sl = slice(c * CHUNK, (c + 1) * CHUNK)
        lw_c = lw[sl]
        cum = jnp.dot(tri, lw_c, precision=HI, preferred_element_type=F32)
        e_pos = jnp.exp(cum)
        e_neg = jnp.exp(-cum)
        e_prev = jnp.exp(cum - lw_c)
        g_l = e_pos[last:last + 1, :]
        at = _heads(a_vec[sl] * e_prev)
        rt = _heads(r[sl] * e_pos)
        bt_full = b_vec[sl] * e_neg
        kt_full = kd[sl] * e_neg
        bt = _heads(bt_full)
        kt = _heads(kt_full)
        bh = _heads(bt_full * g_l)
        kh = _heads(kt_full * g_l)
        vh = _heads(v[sl])
        gl_h = _heads(g_l)

        m_ab = jnp.where(m_strict, _bdot_nt(at, bt), 0.0)
        m_ak = jnp.where(m_strict, _bdot_nt(at, kt), 0.0)
        q_b = jnp.where(m_incl, _bdot_nt(rt, bt), 0.0)
        q_k = jnp.where(m_incl, _bdot_nt(rt, kt), 0.0)
        tinv = eye + m_ab
        mp = m_ab
        for _ in range(5):
            mp = _bdot(mp, mp)
            tinv = tinv + _bdot(tinv, mp)
        a_hat = _bdot(tinv, at)
        u_v = _bdot(tinv, _bdot(m_ak, vh))

        st = st_ref[...]
        u = _bdot(a_hat, st) + u_v
        y = _bdot(rt, st) + _bdot(q_b, u) + _bdot(q_k, vh)
        decay = jnp.einsum('hck,hkv->hcv', eye * gl_h, st, precision=HI, preferred_element_type=F32)
        st_ref[...] = decay + _bdot_tn(bh, u) + _bdot_tn(kh, vh)
        for h in range(RWKV_HEADS):
            y_scr[sl, h * HEAD_DIM:(h + 1) * HEAD_DIM] = y[h]

    if not final:
        out_ref[0] = y_scr[...]
    else:
        yy = yf_ref[0] + y_scr[...]
        mu_y = head_sum(yy) * (1.0 / HEAD_DIM)
        yc = yy - mu_y
        var_y = head_sum(yc * yc) * (1.0 / HEAD_DIM)
        yn = yc * lax.rsqrt(var_y + GN_EPS) * gng_ref[...] + gnb_ref[...]
        bonus = head_sum(r * k * rk_ref[...]) * v
        gd = zs[:, o3 + W_LORA + A_LORA:]
        gate = jnp.dot(jax.nn.sigmoid(gd).astype(BF16), gup_ref[...], preferred_element_type=F32)
        out_ref[0] = ((yn + bonus) * gate).astype(BF16)


def _rwkv_dir(zr, y_fwd, p, d, lt):
    b, t, _ = zr.shape
    nt = t // lt
    reverse = d == 1
    final = y_fwd is not None
    tmap = (lambda j: nt - 1 - j) if reverse else (lambda j: j)
    r8 = lt // 8
    vec = _const_spec((1, D_RWKV))
    in_specs = [pl.BlockSpec((1, lt, D_RWKV_IN), lambda i, j: (i, tmap(j), 0)),
                pl.BlockSpec((1, 8, D_RWKV_IN), lambda i, j: (i, jnp.maximum(tmap(j) * r8 - 1, 0), 0)),
                pl.BlockSpec((1, 8, D_RWKV_IN), lambda i, j: (i, jnp.minimum((tmap(j) + 1) * r8, t // 8 - 1), 0)),
                _const_spec((2, D_RWKV_IN)), vec, _const_spec((W_LORA + A_LORA, D_RWKV)), vec,
                _const_spec((W_LORA + A_LORA, D_RWKV))]
    args = [zr, zr, zr, p['mu'], p['w0'][d], p['rw_w_up'][d], p['a0'][d], p['rw_a_up'][d]]
    if final:
        in_specs += [_const_spec((G_LORA, D_RWKV)), vec, vec, vec, vec, vec, _const_spec((D_RWKV, D_RWKV)),
                     pl.BlockSpec((1, lt, D_RWKV), lambda i, j: (i, tmap(j), 0))]
        args += [p['g_up'], p['k_k'], p['k_a'], p['r_k'], p['gn_g'], p['gn_b'], p['bd'], y_fwd]
    else:
        in_specs += [vec, vec, _const_spec((D_RWKV, D_RWKV))]
        args += [p['k_k'], p['k_a'], p['bd']]
    return pl.pallas_call(
        functools.partial(_rwkv_kernel, reverse=reverse, lt=lt, nt=nt, final=final),
        grid=(b, nt),
        in_specs=in_specs,
        out_specs=pl.BlockSpec((1, lt, D_RWKV), lambda i, j: (i, tmap(j), 0)),
        out_shape=jax.ShapeDtypeStruct((b, t, D_RWKV), BF16 if final else F32),
        scratch_shapes=[pltpu.VMEM((RWKV_HEADS, HEAD_DIM, HEAD_DIM), F32),
                        pltpu.VMEM((lt, D_RWKV), F32)],
        compiler_params=_cparams("arbitrary", "arbitrary"),
        name="rwkv_bwd_final" if final else "rwkv_fwd",
    )(*args)


def _memkv_kernel(m_ref, g_ref, b_ref, w_ref, kv_ref):
    m = _ln(m_ref[...], g_ref[...], b_ref[...])
    kv_ref[...] = jnp.dot(m.astype(BF16), w_ref[...], preferred_element_type=F32).astype(BF16)


def _memkv(mem2d, g, b, w_ckv):
    n = mem2d.shape[0]
    return pl.pallas_call(
        _memkv_kernel,
        grid=(n // N_MEM,),
        in_specs=[pl.BlockSpec((N_MEM, D_MODEL), lambda i: (i, 0)),
                  _const_spec((1, D_MODEL)), _const_spec((1, D_MODEL)),
                  _const_spec((D_MODEL, 2 * D_CROSS))],
        out_specs=pl.BlockSpec((N_MEM, 2 * D_CROSS), lambda i: (i, 0)),
        out_shape=jax.ShapeDtypeStruct((n, 2 * D_CROSS), BF16),
        compiler_params=_cparams("arbitrary"),
        name="mem_kv",
    )(mem2d, g, b, w_ckv)


def _mid_kernel(x_ref, att_ref, rw_ref, kv_ref, wo_ref, l1g_ref, l1b_ref, wq_ref, wco_ref, l2g_ref, l2b_ref,
                wr_ref, br_ref, x2_ref, route_ref, routet_ref, meta_ref, *, sub, win):
    mix = (jnp.dot(att_ref[...], wo_ref[:D_ATT, :], preferred_element_type=F32)
           + jnp.dot(rw_ref[...], wo_ref[D_ATT:, :], preferred_element_type=F32))
    x1 = _ln(DEEPNORM_ALPHA * x_ref[...] + mix, l1g_ref[...], l1b_ref[...])

    q = jnp.dot(x1.astype(BF16), wq_ref[...], preferred_element_type=F32)
    q = (q * (CROSS_HEAD_DIM ** -0.5)).astype(BF16)
    kv = kv_ref[...]
    cr = None
    for h in range(CROSS_HEADS):
        hs = slice(h * CROSS_HEAD_DIM, (h + 1) * CROSS_HEAD_DIM)
        kh = kv[:, h * CROSS_HEAD_DIM:(h + 1) * CROSS_HEAD_DIM]
        vh = kv[:, D_CROSS + h * CROSS_HEAD_DIM:D_CROSS + (h + 1) * CROSS_HEAD_DIM]
        s = lax.dot_general(q[:, hs], kh, (((1,), (1,)), ((), ())), preferred_element_type=F32)
        m = jnp.max(s, -1, keepdims=True)
        p = jnp.exp(s - m)
        den = jnp.sum(p, -1, keepdims=True)
        o = jnp.dot(p.astype(BF16), vh, preferred_element_type=F32) / den
        part = jnp.dot(o.astype(BF16), wco_ref[hs, :], preferred_element_type=F32)
        cr = part if cr is None else cr + part
    x2 = _ln(DEEPNORM_ALPHA * x1 + cr, l2g_ref[...], l2b_ref[...])
    x2_ref[...] = x2

    logits = jnp.dot(x2, wr_ref[...], precision=HI, preferred_element_type=F32) + br_ref[...]
    lane = lax.broadcasted_iota(jnp.int32, logits.shape, 1)
    is_g = (lane >= N_EXPERTS) & (lane < N_EXPERTS + N_EXPERT_GROUPS)
    lg = jnp.where(is_g, logits, NEG_INF)
    eg = jnp.where(is_g, jnp.exp(lg - jnp.max(lg, -1, keepdims=True)), 0.0)
    pg = eg / jnp.sum(eg, -1, keepdims=True)
    pg_top = jnp.max(pg, -1, keepdims=True)
    gi = jnp.min(jnp.where(is_g & (pg == pg_top), lane, 4 * ROUTE_LANES), -1, keepdims=True) - N_EXPERTS
    in_grp = (lane >= gi * EXPERTS_PER_GROUP) & (lane < (gi + 1) * EXPERTS_PER_GROUP)
    le = jnp.where(in_grp, logits, NEG_INF)
    ee = jnp.where(in_grp, jnp.exp(le - jnp.max(le, -1, keepdims=True)), 0.0)
    pe = ee / jnp.sum(ee, -1, keepdims=True)
    p1 = jnp.max(jnp.where(in_grp, pe, -1.0), -1, keepdims=True)
    i1 = jnp.min(jnp.where(in_grp & (pe == p1), lane, 4 * ROUTE_LANES), -1, keepdims=True)
    rest = in_grp & (lane != i1)
    p2 = jnp.max(jnp.where(rest, pe, -1.0), -1, keepdims=True)
    i2 = jnp.min(jnp.where(rest & (pe == p2), lane, 4 * ROUTE_LANES), -1, keepdims=True)
    tot = p1 + p2
    w1 = p1 / tot * pg_top
    w2 = p2 / tot * pg_top

    tm = logits.shape[0]
    n_sub = tm // sub
    e_lane = lane & (N_EXPERTS - 1)
    sel = ((e_lane == i1) | (e_lane == i2)) & (lane < 2 * N_EXPERTS)
    sel_b = sel.astype(BF16)
    ti = lax.broadcasted_iota(jnp.int32, (tm, tm), 0)
    tj = lax.broadcasted_iota(jnp.int32, (tm, tm), 1)
    before = ((tj < ti) & ((tj // sub) == (ti // sub))).astype(BF16)
    rank = jnp.dot(before, sel_b, preferred_element_type=F32)
    si = lax.broadcasted_iota(jnp.int32, (8, tm), 0)
    sj = lax.broadcasted_iota(jnp.int32, (8, tm), 1)
    cnt = jnp.dot(((sj // sub) == si).astype(BF16), sel_b, preferred_element_type=F32)
    padded = jnp.floor((cnt + (SEG_ALIGN - 1)) * (1.0 / SEG_ALIGN)) * SEG_ALIGN
    ui = lax.broadcasted_iota(jnp.int32, (ROUTE_LANES, ROUTE_LANES), 0)
    uj = lax.broadcasted_iota(jnp.int32, (ROUTE_LANES, ROUTE_LANES), 1)
    excl = ((ui < uj) & (uj < N_EXPERTS)).astype(F32)
    lane8 = lax.broadcasted_iota(jnp.int32, (8, ROUTE_LANES), 1)
    off = jnp.dot(jnp.where(lane8 < N_EXPERTS, padded, 0.0), excl, precision=HI, preferred_element_type=F32)
    nblk = jnp.floor((cnt + (win - 1)) * (1.0 / win))
    meta_ref[...] = jnp.where(lane8 < N_EXPERTS, off, nblk).astype(jnp.int32)
    sub_id = lax.broadcasted_iota(jnp.int32, (tm, 1), 0) // sub
    off_tok = jnp.zeros_like(rank)
    for s in range(n_sub):
        off_tok = jnp.where(sub_id == s, off[s:s + 1, :], off_tok)
    dest = off_tok + rank
    d1 = jnp.sum(jnp.where(lane == i1, dest, 0.0), -1, keepdims=True)
    d2 = jnp.sum(jnp.where(lane == i2, dest, 0.0), -1, keepdims=True)
    route = (jnp.where(lane == 0, w1, 0.0) + jnp.where(lane == 1, w2, 0.0)
             + jnp.where(lane == 2, d1, 0.0) + jnp.where(lane == 3, d2, 0.0))
    route_ref[...] = route
    pick = (lane8 == lax.broadcasted_iota(jnp.int32, (8, ROUTE_LANES), 0)).astype(F32)
    routet_ref[...] = lax.dot_general(pick, route, (((1,), (1,)), ((), ())), precision=HI,
                                      preferred_element_type=F32)


def _mid(x2d, att2d, rw2d, kv, p, tm, t, sub, win):
    n = x2d.shape[0]
    per_b = t // tm
    nt = n // tm
    vec = _const_spec((1, D_MODEL))
    return pl.pallas_call(
        functools.partial(_mid_kernel, sub=sub, win=win),
        grid=(nt,),
        in_specs=[pl.BlockSpec((tm, D_MODEL), lambda i: (i, 0)),
                  pl.BlockSpec((tm, D_ATT), lambda i: (i, 0)),
                  pl.BlockSpec((tm, D_RWKV), lambda i: (i, 0)),
                  pl.BlockSpec((N_MEM, 2 * D_CROSS), lambda i: (i // per_b, 0)),
                  _const_spec((D_MODEL, D_MODEL)), vec, vec,
                  _const_spec((D_MODEL, D_CROSS)), _const_spec((D_CROSS, D_MODEL)), vec, vec,
                  _const_spec((D_MODEL, ROUTE_LANES)), _const_spec((1, ROUTE_LANES))],
        out_specs=[pl.BlockSpec((tm, D_MODEL), lambda i: (i, 0)),
                   pl.BlockSpec((tm, ROUTE_LANES), lambda i: (i, 0)),
                   pl.BlockSpec((8, tm), lambda i: (i, 0)),
                   pl.BlockSpec((8, ROUTE_LANES), lambda i: (i, 0))],
        out_shape=[jax.ShapeDtypeStruct((n, D_MODEL), F32),
                   jax.ShapeDtypeStruct((n, ROUTE_LANES), F32),
                   jax.ShapeDtypeStruct((nt * 8, tm), F32),
                   jax.ShapeDtypeStruct((nt * 8, ROUTE_LANES), jnp.int32)],
        compiler_params=_cparams("arbitrary"),
        name="mid",
    )(x2d, att2d, rw2d, kv, p['w_out'], p['ln1_g'], p['ln1_b'], p['w_cq'], p['w_co'], p['ln2_g'], p['ln2_b'],
      p['w_route'], p['b_route'])


def _moe_kernel(meta_ref, x_ref, route_ref, routet_ref, wg_ref, wu_ref, wd_ref, l3g_ref, l3b_ref, o_ref,
                xs_ref, ys_ref, *, sub, win, rows):
    i = pl.program_id(0)
    e = pl.program_id(1)
    n_sub = xs_ref.shape[0]
    dummy = rows - win

    @pl.when(e == 0)
    def _():
        ys_ref[...] = jnp.zeros_like(ys_ref)
        r_id = lax.broadcasted_iota(jnp.int32, (rows, sub), 0).astype(F32)
        for s in range(n_sub):
            d1 = routet_ref[2:3, s * sub:(s + 1) * sub]
            d2 = routet_ref[3:4, s * sub:(s + 1) * sub]
            perm = ((r_id == d1) | (r_id == d2)).astype(BF16)
            xb = x_ref[s * sub:(s + 1) * sub, :].astype(BF16)
            xs_ref[s] = jnp.dot(perm, xb, preferred_element_type=F32).astype(BF16)

    base = i * (n_sub * 32)
    offs = [meta_ref[base + s * 32 + e] for s in range(n_sub)]
    nbs = [meta_ref[base + s * 32 + N_EXPERTS + e] for s in range(n_sub)]
    nb_max = functools.reduce(jnp.maximum, nbs)

    def block(j, carry):
        starts = [pl.multiple_of(jnp.where(j < nbs[s], offs[s] + j * win, dummy), SEG_ALIGN) for s in range(n_sub)]
        xw = jnp.concatenate([xs_ref[s, pl.ds(starts[s], win), :] for s in range(n_sub)], axis=0)
        gate = jnp.dot(xw, wg_ref[0], preferred_element_type=F32)
        up = jnp.dot(xw, wu_ref[0], preferred_element_type=F32)
        hdn = (gate * jax.nn.sigmoid(gate)) * up
        ye = jnp.dot(hdn.astype(BF16), wd_ref[0], preferred_element_type=F32).astype(BF16)
        for s in range(n_sub):
            ys_ref[s, pl.ds(starts[s], win), :] = ye[s * win:(s + 1) * win]
        return carry

    lax.fori_loop(0, nb_max, block, 0)

    @pl.when(e == N_EXPERTS - 1)
    def _():
        c_id = lax.broadcasted_iota(jnp.int32, (sub, rows), 1).astype(F32)
        for s in range(n_sub):
            rt = route_ref[s * sub:(s + 1) * sub, :]
            w1, w2, d1, d2 = rt[:, 0:1], rt[:, 1:2], rt[:, 2:3], rt[:, 3:4]
            comb = jnp.where(c_id == d1, w1, 0.0) + jnp.where(c_id == d2, w2, 0.0)
            hi = comb.astype(BF16)
            lo = (comb - hi.astype(F32)).astype(BF16)
            ysv = ys_ref[s]
            ff = (jnp.dot(hi, ysv, preferred_element_type=F32) + jnp.dot(lo, ysv, preferred_element_type=F32))
            xr = x_ref[s * sub:(s + 1) * sub, :]
            o_ref[s * sub:(s + 1) * sub, :] = _ln(DEEPNORM_ALPHA * xr + ff, l3g_ref[...], l3b_ref[...])


def _moe(x2, route, routet, meta, p, tm, sub, win):
    n = x2.shape[0]
    nt = n // tm
    n_sub = tm // sub
    rows = 2 * sub + N_EXPERTS * SEG_ALIGN + 2 * win
    meta_flat = meta.reshape(nt, 8, ROUTE_LANES)[:, :n_sub, :32].reshape(-1)
    vec = pl.BlockSpec((1, D_MODEL), lambda i, e, m: (0, 0))
    grid_spec = pltpu.PrefetchScalarGridSpec(
        num_scalar_prefetch=1,
        grid=(nt, N_EXPERTS),
        in_specs=[pl.BlockSpec((tm, D_MODEL), lambda i, e, m: (i, 0)),
                  pl.BlockSpec((tm, ROUTE_LANES), lambda i, e, m: (i, 0)),
                  pl.BlockSpec((8, tm), lambda i, e, m: (i, 0)),
                  pl.BlockSpec((1, D_MODEL, D_EXPERT), lambda i, e, m: (e, 0, 0)),
                  pl.BlockSpec((1, D_MODEL, D_EXPERT), lambda i, e, m: (e, 0, 0)),
                  pl.BlockSpec((1, D_EXPERT, D_MODEL), lambda i, e, m: (e, 0, 0)),
                  vec, vec],
        out_specs=pl.BlockSpec((tm, D_MODEL), lambda i, e, m: (i, 0)),
        scratch_shapes=[pltpu.VMEM((n_sub, rows, D_MODEL), BF16),
                        pltpu.VMEM((n_sub, rows, D_MODEL), BF16)])
    return pl.pallas_call(
        functools.partial(_moe_kernel, sub=sub, win=win, rows=rows),
        grid_spec=grid_spec,
        out_shape=jax.ShapeDtypeStruct((n, D_MODEL), F32),
        compiler_params=_cparams("arbitrary", "arbitrary"),
        name="moe",
    )(meta_flat, x2, route, routet, p['w_gate'], p['w_up'], p['w_down'], p['ln3_g'], p['ln3_b'])


def _prep_params(w_in, tshift_mu, attn_sink, rwkv_w0, rwkv_w_up, rwkv_a0, rwkv_a_up, rwkv_g_up, rwkv_k_k,
                 rwkv_k_a, rwkv_r_k, rwkv_gn_g, rwkv_gn_b, w_out, ln1_g, ln1_b, mem_ln_g, mem_ln_b, w_cq, w_ckv,
                 w_co, ln2_g, ln2_b, w_route_group, b_route_group, w_route_expert, b_route_expert,
                 w_exp_gate, w_exp_up, w_exp_down, ln3_g, ln3_b):
    row = lambda a: a.reshape(1, -1).astype(F32)
    zeros_lora = jnp.zeros((2, W_LORA, D_RWKV), F32)
    hid = jnp.arange(D_RWKV) // HEAD_DIM
    pad = ROUTE_LANES - N_EXPERTS - N_EXPERT_GROUPS
    return {
        'w_in_att': w_in[:, :D_QKV].astype(BF16),
        'w_in_rw': w_in[:, D_QKV:].astype(BF16),
        'sink': attn_sink.astype(F32),
        'mu': tshift_mu.astype(F32),
        'w0': rwkv_w0.reshape(2, 1, D_RWKV).astype(F32),
        'rw_w_up': jnp.concatenate([rwkv_w_up, zeros_lora], axis=1).astype(BF16),
        'a0': rwkv_a0.reshape(2, 1, D_RWKV).astype(F32),
        'rw_a_up': jnp.concatenate([zeros_lora, rwkv_a_up], axis=1).astype(BF16),
        'g_up': rwkv_g_up.astype(BF16),
        'k_k': row(rwkv_k_k), 'k_a': row(rwkv_k_a), 'r_k': row(rwkv_r_k),
        'gn_g': row(rwkv_gn_g), 'gn_b': row(rwkv_gn_b),
        'bd': (hid[:, None] == hid[None, :]).astype(F32),
        'w_out': w_out.astype(BF16),
        'ln1_g': row(ln1_g), 'ln1_b': row(ln1_b),
        'mem_ln_g': row(mem_ln_g), 'mem_ln_b': row(mem_ln_b),
        'w_cq': w_cq.astype(BF16), 'w_ckv': w_ckv.astype(BF16), 'w_co': w_co.astype(BF16),
        'ln2_g': row(ln2_g), 'ln2_b': row(ln2_b),
        'w_route': jnp.pad(jnp.concatenate([w_route_expert, w_route_group], axis=1), ((0, 0), (0, pad))).astype(F32),
        'b_route': jnp.pad(jnp.concatenate([b_route_expert, b_route_group]), (0, pad)).reshape(1, -1).astype(F32),
        'w_gate': w_exp_gate.astype(BF16), 'w_up': w_exp_up.astype(BF16), 'w_down': w_exp_down.astype(BF16),
        'ln3_g': row(ln3_g), 'ln3_b': row(ln3_b),
    }


def _tile(n, pref):
    t = pref
    while n % t:
        t //= 2
    return t


def _layer(x, mem, p):
    b, t, _ = x.shape
    n = b * t
    x2d = x.reshape(n, D_MODEL)
    tm = _tile(t, 512)
    qkv, zr = _inproj(x2d, p['w_in_att'], p['w_in_rw'], tm)
    att = _attention(qkv.reshape(b, t, D_QKV), p['sink'])
    zr3 = zr.reshape(b, t, D_RWKV_IN)
    lt = _tile(t, 128)
    y_fwd = _rwkv_dir(zr3, None, p, 0, lt)
    rw = _rwkv_dir(zr3, y_fwd, p, 1, lt)
    kv = _memkv(mem.reshape(b * N_MEM, D_MODEL), p['mem_ln_g'], p['mem_ln_b'], p['w_ckv'])
    tme = _tile(t, 1024)
    sub = _tile(tme, 256)
    win = sub // 4
    x2, route, routet, meta = _mid(x2d, att.reshape(n, D_ATT), rw.reshape(n, D_RWKV), kv, p, tme, t, sub, win)
    y = _moe(x2, route, routet, meta, p, tme, sub, win)
    return y.reshape(b, t, D_MODEL)


def kernel(x_prompt, x_sample, mem_prompt, mem_sample, w_in, tshift_mu, attn_sink, rwkv_w0, rwkv_w_up, rwkv_a0, rwkv_a_up, rwkv_g_up, rwkv_k_k, rwkv_k_a, rwkv_r_k, rwkv_gn_g, rwkv_gn_b, w_out, ln1_g, ln1_b, mem_ln_g, mem_ln_b, w_cq, w_ckv, w_co, ln2_g, ln2_b, w_route_group, b_route_group, w_route_expert, b_route_expert, w_exp_gate, w_exp_up, w_exp_down, ln3_g, ln3_b):
    weights = (w_in, tshift_mu, attn_sink, rwkv_w0, rwkv_w_up, rwkv_a0, rwkv_a_up, rwkv_g_up, rwkv_k_k, rwkv_k_a,
               rwkv_r_k, rwkv_gn_g, rwkv_gn_b, w_out, ln1_g, ln1_b, mem_ln_g, mem_ln_b, w_cq, w_ckv, w_co,
               ln2_g, ln2_b, w_route_group, b_route_group, w_route_expert, b_route_expert,
               w_exp_gate, w_exp_up, w_exp_down, ln3_g, ln3_b)
    p = _prep_params(*[w[0] for w in weights])
    return (_layer(x_prompt, mem_prompt, p), _layer(x_sample, mem_sample, p))
```

```python
import functools

import jax
import jax.numpy as jnp
from jax import lax
from jax.experimental import pallas as pl
from jax.experimental.pallas import tpu as pltpu

F32 = jnp.float32
BF16 = jnp.bfloat16
HI = lax.Precision.HIGHEST

D_MODEL = 1024
HEAD_DIM = 64
ATT_Q_HEADS = 8
ATT_KV_HEADS = 2
ATT_GROUP = ATT_Q_HEADS // ATT_KV_HEADS
WINDOW = 128
BLOCK = 128
RWKV_HEADS = 8
D_ATT = ATT_Q_HEADS * HEAD_DIM
D_KV = ATT_KV_HEADS * HEAD_DIM
D_QKV = D_ATT + 2 * D_KV
D_RWKV = RWKV_HEADS * HEAD_DIM
W_LORA = 64
A_LORA = 64
G_LORA = 128
D_RWKV_IN = 3 * D_RWKV + W_LORA + A_LORA + G_LORA
N_MEM = 256
CROSS_HEADS = 4
CROSS_HEAD_DIM = 128
D_CROSS = CROSS_HEADS * CROSS_HEAD_DIM
N_EXPERT_GROUPS = 4
EXPERTS_PER_GROUP = 4
N_EXPERTS = N_EXPERT_GROUPS * EXPERTS_PER_GROUP
D_EXPERT = 512
LN_EPS = 1e-5
GN_EPS = 64e-5
DEEPNORM_ALPHA = 2.0 ** 0.25
NEG_INF = -1e30
ROUTE_LANES = 128
CHUNK = 64
SEG_ALIGN = 16
VMEM_LIMIT = 56 * 1024 * 1024


def _cparams(*sem):
    return pltpu.CompilerParams(dimension_semantics=sem, vmem_limit_bytes=VMEM_LIMIT)


def _ln(x, g, b):
    mu = jnp.mean(x, -1, keepdims=True)
    xc = x - mu
    var = jnp.mean(xc * xc, -1, keepdims=True)
    return xc * lax.rsqrt(var + LN_EPS) * g + b


def _const_spec(shape):
    nd = len(shape)
    return pl.BlockSpec(shape, lambda *_: (0,) * nd)


def _inproj_kernel(x_ref, wa_ref, wr_ref, qkv_ref, zr_ref):
    xb = x_ref[...].astype(BF16)
    qkv_ref[...] = jnp.dot(xb, wa_ref[...], preferred_element_type=F32).astype(BF16)
    zr_ref[...] = jnp.dot(xb, wr_ref[...], preferred_element_type=F32)


def _inproj(x2d, w_att, w_rw, tm):
    n = x2d.shape[0]
    return pl.pallas_call(
        _inproj_kernel,
        grid=(n // tm,),
        in_specs=[pl.BlockSpec((tm, D_MODEL), lambda i: (i, 0)),
                  _const_spec((D_MODEL, D_QKV)),
                  _const_spec((D_MODEL, D_RWKV_IN))],
        out_specs=[pl.BlockSpec((tm, D_QKV), lambda i: (i, 0)),
                   pl.BlockSpec((tm, D_RWKV_IN), lambda i: (i, 0))],
        out_shape=[jax.ShapeDtypeStruct((n, D_QKV), BF16),
                   jax.ShapeDtypeStruct((n, D_RWKV_IN), F32)],
        compiler_params=_cparams("arbitrary"),
        name="inproj",
    )(x2d, w_att, w_rw)


def _attn_kernel(sink_ref, cur_ref, prv_ref, nxt_ref, o_ref, *, nb):
    n = pl.program_id(1)
    cur = cur_ref[0]
    prv = prv_ref[0]
    nxt = nxt_ref[0]
    qi = lax.broadcasted_iota(jnp.int32, (BLOCK, 3 * BLOCK), 0)
    ki = lax.broadcasted_iota(jnp.int32, (BLOCK, 3 * BLOCK), 1)
    adist = jnp.abs(qi - ki + BLOCK)
    valid = (adist <= WINDOW) & ((ki >= BLOCK) | (n > 0)) & ((ki < 2 * BLOCK) | (n < nb - 1))
    adist_f = adist.astype(F32)
    for h in range(ATT_KV_HEADS):
        k0 = D_ATT + h * HEAD_DIM
        v0 = D_ATT + D_KV + h * HEAD_DIM
        kcat = jnp.concatenate([prv[:, k0:k0 + HEAD_DIM], cur[:, k0:k0 + HEAD_DIM], nxt[:, k0:k0 + HEAD_DIM]], axis=0)
        vcat = jnp.concatenate([prv[:, v0:v0 + HEAD_DIM], cur[:, v0:v0 + HEAD_DIM], nxt[:, v0:v0 + HEAD_DIM]], axis=0)
        for g in range(ATT_GROUP):
            hq = h * ATT_GROUP + g
            slope = 2.0 ** (-8.0 / ATT_Q_HEADS * (hq + 1))
            q = cur[:, hq * HEAD_DIM:(hq + 1) * HEAD_DIM] * (HEAD_DIM ** -0.5)
            s = lax.dot_general(q, kcat, (((1,), (1,)), ((), ())), preferred_element_type=F32)
            s = jnp.where(valid, s - slope * adist_f, NEG_INF)
            sk = sink_ref[hq]
            m = jnp.maximum(jnp.max(s, -1, keepdims=True), sk)
            p = jnp.exp(s - m)
            den = jnp.sum(p, -1, keepdims=True) + jnp.exp(sk - m)
            o = jnp.dot(p.astype(BF16), vcat, preferred_element_type=F32) / den
            o_ref[0, :, hq * HEAD_DIM:(hq + 1) * HEAD_DIM] = o.astype(BF16)


def _attention(qkv, sink):
    b, t, _ = qkv.shape
    nb = t // BLOCK
    return pl.pallas_call(
        functools.partial(_attn_kernel, nb=nb),
        grid=(b, nb),
        in_specs=[pl.BlockSpec(memory_space=pltpu.SMEM),
                  pl.BlockSpec((1, BLOCK, D_QKV), lambda i, j: (i, j, 0)),
                  pl.BlockSpec((1, BLOCK, D_QKV), lambda i, j: (i, jnp.maximum(j - 1, 0), 0)),
                  pl.BlockSpec((1, BLOCK, D_QKV), lambda i, j: (i, jnp.minimum(j + 1, nb - 1), 0))],
        out_specs=pl.BlockSpec((1, BLOCK, D_ATT), lambda i, j: (i, j, 0)),
        out_shape=jax.ShapeDtypeStruct((b, t, D_ATT), BF16),
        compiler_params=_cparams("arbitrary", "arbitrary"),
        name="win_attn",
    )(sink, qkv, qkv, qkv)


def _heads(x):
    return jnp.stack([x[:, h * HEAD_DIM:(h + 1) * HEAD_DIM] for h in range(RWKV_HEADS)], axis=0)


def _bmm(a, b):
    return jnp.einsum('hlj,hjm->hlm', a.astype(BF16), b.astype(BF16), preferred_element_type=F32)


def _split3(x):
    hi = x.astype(BF16)
    r1 = x - hi.astype(F32)
    mid = r1.astype(BF16)
    lo = (r1 - mid.astype(F32)).astype(BF16)
    return hi, mid, lo


def _rwkv_kernel(*refs, reverse, lt, nt, final):
    if final:
        (zc_ref, zp_ref, zn_ref, mu_ref, w0_ref, wup_ref, a0_ref, aup_ref, gup_ref, kk_ref, ka_ref, rk_ref,
         gng_ref, gnb_ref, bd_ref, yf_ref, out_ref,
         st_ref, y_scr, r_s, kd_s, v_s, a_s, b_s, lw_s, bonus_s, gate_s) = refs
    else:
        (zc_ref, zp_ref, zn_ref, mu_ref, w0_ref, wup_ref, a0_ref, aup_ref, kk_ref, ka_ref, bd_ref,
         out_ref, st_ref, y_scr, r_s, kd_s, v_s, a_s, b_s, lw_s) = refs
    i = pl.program_id(1)
    tt = (nt - 1 - i) if reverse else i

    @pl.when(i == 0)
    def _():
        st_ref[...] = jnp.zeros_like(st_ref)

    bd = bd_ref[...]

    def head_sum(x):
        hi = x.astype(BF16)
        lo = (x - hi.astype(F32)).astype(BF16)
        return jnp.dot(hi, bd, preferred_element_type=F32) + jnp.dot(lo, bd, preferred_element_type=F32)

    z = zc_ref[0]
    row = lax.broadcasted_iota(jnp.int32, (lt, 1), 0)
    zprev_edge = jnp.where(tt > 0, zp_ref[0, 7:8, :], 0.0)
    znext_edge = jnp.where(tt < nt - 1, zn_ref[0, 0:1, :], 0.0)
    prev = jnp.where(row == 0, zprev_edge, pltpu.roll(z, 1, 0))
    nxt = jnp.where(row == lt - 1, znext_edge, pltpu.roll(z, lt - 1, 0))
    zs = z + mu_ref[0:1, :] * (prev - z) + mu_ref[1:2, :] * (nxt - z)

    o1, o2, o3 = D_RWKV, 2 * D_RWKV, 3 * D_RWKV
    r = zs[:, :o1]
    k = zs[:, o1:o2]
    v = zs[:, o2:o3]
    wad = zs[:, o3:o3 + W_LORA + A_LORA]
    kk = k * kk_ref[...]
    kk = kk * lax.rsqrt(head_sum(kk * kk) + 1e-12)
    w_arg = w0_ref[...] + jnp.dot(jnp.tanh(wad).astype(BF16), wup_ref[...], preferred_element_type=F32)
    a_sig = jax.nn.sigmoid(a0_ref[...] + jnp.dot(wad.astype(BF16), aup_ref[...], preferred_element_type=F32))
    r_s[...] = r
    v_s[...] = v
    kd_s[...] = k * (1.0 + (a_sig - 1.0) * ka_ref[...])
    a_s[...] = -kk
    b_s[...] = kk * a_sig
    lw_s[...] = -jnp.exp(-jax.nn.softplus(-w_arg) - 0.5)
    if final:
        bonus_s[...] = head_sum(r * k * rk_ref[...]) * v
        gd = zs[:, o3 + W_LORA + A_LORA:]
        gate_s[...] = jnp.dot(jax.nn.sigmoid(gd).astype(BF16), gup_ref[...], preferred_element_type=F32)

    li = lax.broadcasted_iota(jnp.int32, (2 * CHUNK, 2 * CHUNK), 0)
    lj = lax.broadcasted_iota(jnp.int32, (2 * CHUNK, 2 * CHUNK), 1)
    lt_, jt_ = li & (CHUNK - 1), lj & (CHUNK - 1)
    if reverse:
        strict, incl = lt_ < jt_, lt_ <= jt_
    else:
        strict, incl = lt_ > jt_, lt_ >= jt_
    pmask = strict | (incl & (li >= CHUNK))
    ci = lax.broadcasted_iota(jnp.int32, (CHUNK, CHUNK), 0)
    cj = lax.broadcasted_iota(jnp.int32, (CHUNK, CHUNK), 1)
    tri = ((ci <= cj) if reverse else (ci >= cj)).astype(BF16)
    eye = (ci == cj).astype(F32)
    ones_b = jnp.ones((CHUNK, 2 * CHUNK), BF16)
    zeros_h = jnp.zeros((RWKV_HEADS, CHUNK, HEAD_DIM), BF16)
    nc = lt // CHUNK

    def chunk(j, carry):
        c = (nc - 1 - j) if reverse else j
        rows = pl.ds(pl.multiple_of(c * CHUNK, CHUNK), CHUNK)
        lw_c = lw_s[rows, :]
        parts = _split3(lw_c)
        cum = sum(jnp.dot(tri, q, preferred_element_type=F32) for q in parts)
        tot = sum(lax.dot_general(q, ones_b, (((0,), (0,)), ((), ())), preferred_element_type=F32) for q in parts)
        g_col = jnp.exp(tot).reshape(RWKV_HEADS, HEAD_DIM, 2 * CHUNK)[:, :, :HEAD_DIM]
        e_pos = jnp.exp(cum)
        e_neg = jnp.exp(-cum)
        e_prev = jnp.exp(cum - lw_c)
        ar = jnp.concatenate([_heads((a_s[rows, :] * e_prev).astype(BF16)),
                              _heads((r_s[rows, :] * e_pos).astype(BF16))], axis=1)
        bk = jnp.concatenate([_heads((b_s[rows, :] * e_neg).astype(BF16)),
                              _heads((kd_s[rows, :] * e_neg).astype(BF16))], axis=1)
        vh = _heads(v_s[rows, :].astype(BF16))
        pw = jnp.einsum('hlc,hjc->hlj', ar, bk, preferred_element_type=F32)
        pw = jnp.where(pmask, pw, 0.0)
        pw_b = pw.astype(BF16)
        m_ab = pw[:, :CHUNK, :CHUNK]
        tinv = eye + m_ab
        mp = m_ab
        for _ in range(5):
            mp = _bmm(mp, mp)
            tinv = tinv + _bmm(tinv, mp)

        st = st_ref[...]
        ar_st = _bmm(ar, st)
        w = ar_st[:, :CHUNK] + _bmm(pw_b[:, :CHUNK], jnp.concatenate([zeros_h, vh], axis=1))
        u = _bmm(tinv, w)
        uv = jnp.concatenate([u.astype(BF16), vh], axis=1)
        y = ar_st[:, CHUNK:] + _bmm(pw_b[:, CHUNK:], uv)
        st_ref[...] = g_col * (st + jnp.einsum('hjc,hjv->hcv', bk, uv, preferred_element_type=F32))
        for h in range(RWKV_HEADS):
            y_scr[rows, h * HEAD_DIM:(h + 1) * HEAD_DIM] = y[h]
        return carry

    lax.fori_loop(0, nc, chunk, 0)

    if not final:
        out_ref[0] = y_scr[...]
    else:
        yy = yf_ref[0] + y_scr[...]
        mu_y = head_sum(yy) * (1.0 / HEAD_DIM)
        yc = yy - mu_y
        var_y = head_sum(yc * yc) * (1.0 / HEAD_DIM)
        yn = yc * lax.rsqrt(var_y + GN_EPS) * gng_ref[...] + gnb_ref[...]
        out_ref[0] = ((yn + bonus_s[...]) * gate_s[...]).astype(BF16)


def _rwkv_dir(zr, y_fwd, p, d, lt):
    b, t, _ = zr.shape
    nt = t // lt
    reverse = d == 1
    final = y_fwd is not None
    tmap = (lambda j: nt - 1 - j) if reverse else (lambda j: j)
    r8 = lt // 8
    vec = _const_spec((1, D_RWKV))
    in_specs = [pl.BlockSpec((1, lt, D_RWKV_IN), lambda i, j: (i, tmap(j), 0)),
                pl.BlockSpec((1, 8, D_RWKV_IN), lambda i, j: (i, jnp.maximum(tmap(j) * r8 - 1, 0), 0)),
                pl.BlockSpec((1, 8, D_RWKV_IN), lambda i, j: (i, jnp.minimum((tmap(j) + 1) * r8, t // 8 - 1), 0)),
                _const_spec((2, D_RWKV_IN)), vec, _const_spec((W_LORA + A_LORA, D_RWKV)), vec,
                _const_spec((W_LORA + A_LORA, D_RWKV))]
    args = [zr, zr, zr, p['mu'], p['w0'][d], p['rw_w_up'][d], p['a0'][d], p['rw_a_up'][d]]
    n_tok_scratch = 6
    if final:
        in_specs += [_const_spec((G_LORA, D_RWKV)), vec, vec, vec, vec, vec, _const_spec((D_RWKV, D_RWKV)),
                     pl.BlockSpec((1, lt, D_RWKV), lambda i, j: (i, tmap(j), 0))]
        args += [p['g_up'], p['k_k'], p['k_a'], p['r_k'], p['gn_g'], p['gn_b'], p['bd'], y_fwd]
        n_tok_scratch = 8
    else:
        in_specs += [vec, vec, _const_spec((D_RWKV, D_RWKV))]
        args += [p['k_k'], p['k_a'], p['bd']]
    return pl.pallas_call(
        functools.partial(_rwkv_kernel, reverse=reverse, lt=lt, nt=nt, final=final),
        grid=(b, nt),
        in_specs=in_specs,
        out_specs=pl.BlockSpec((1, lt, D_RWKV), lambda i, j: (i, tmap(j), 0)),
        out_shape=jax.ShapeDtypeStruct((b, t, D_RWKV), BF16 if final else F32),
        scratch_shapes=[pltpu.VMEM((RWKV_HEADS, HEAD_DIM, HEAD_DIM), F32)]
                       + [pltpu.VMEM((lt, D_RWKV), F32)] * (1 + n_tok_scratch),
        compiler_params=_cparams("arbitrary", "arbitrary"),
        name="rwkv_bwd_final" if final else "rwkv_fwd",
    )(*args)


def _memkv_kernel(m_ref, g_ref, b_ref, w_ref, kv_ref):
    m = _ln(m_ref[...], g_ref[...], b_ref[...])
    kv_ref[...] = jnp.dot(m.astype(BF16), w_ref[...], preferred_element_type=F32).astype(BF16)


def _memkv(mem2d, g, b, w_ckv):
    n = mem2d.shape[0]
    return pl.pallas_call(
        _memkv_kernel,
        grid=(n // N_MEM,),
        in_specs=[pl.BlockSpec((N_MEM, D_MODEL), lambda i: (i, 0)),
                  _const_spec((1, D_MODEL)), _const_spec((1, D_MODEL)),
                  _const_spec((D_MODEL, 2 * D_CROSS))],
        out_specs=pl.BlockSpec((N_MEM, 2 * D_CROSS), lambda i: (i, 0)),
        out_shape=jax.ShapeDtypeStruct((n, 2 * D_CROSS), BF16),
        compiler_params=_cparams("arbitrary"),
        name="mem_kv",
    )(mem2d, g, b, w_ckv)


def _mid_kernel(x_ref, att_ref, rw_ref, kv_ref, wo_ref, l1g_ref, l1b_ref, wq_ref, wco_ref, l2g_ref, l2b_ref,
                wr_ref, br_ref, x2_ref, route_ref, routet_ref, meta_ref, *, sub, win):
    mix = (jnp.dot(att_ref[...], wo_ref[:D_ATT, :], preferred_element_type=F32)
           + jnp.dot(rw_ref[...], wo_ref[D_ATT:, :], preferred_element_type=F32))
    x1 = _ln(DEEPNORM_ALPHA * x_ref[...] + mix, l1g_ref[...], l1b_ref[...])

    q = jnp.dot(x1.astype(BF16), wq_ref[...], preferred_element_type=F32)
    q = (q * (CROSS_HEAD_DIM ** -0.5)).astype(BF16)
    kv = kv_ref[...]
    cr = None
    for h in range(CROSS_HEADS):
        hs = slice(h * CROSS_HEAD_DIM, (h + 1) * CROSS_HEAD_DIM)
        kh = kv[:, h * CROSS_HEAD_DIM:(h + 1) * CROSS_HEAD_DIM]
        vh = kv[:, D_CROSS + h * CROSS_HEAD_DIM:D_CROSS + (h + 1) * CROSS_HEAD_DIM]
        s = lax.dot_general(q[:, hs], kh, (((1,), (1,)), ((), ())), preferred_element_type=F32)
        m = jnp.max(s, -1, keepdims=True)
        p = jnp.exp(s - m)
        den = jnp.sum(p, -1, keepdims=True)
        o = jnp.dot(p.astype(BF16), vh, preferred_element_type=F32) / den
        part = jnp.dot(o.astype(BF16), wco_ref[hs, :], preferred_element_type=F32)
        cr = part if cr is None else cr + part
    x2 = _ln(DEEPNORM_ALPHA * x1 + cr, l2g_ref[...], l2b_ref[...])
    x2_ref[...] = x2

    logits = jnp.dot(x2, wr_ref[...], precision=HI, preferred_element_type=F32) + br_ref[...]
    lane = lax.broadcasted_iota(jnp.int32, logits.shape, 1)
    is_g = (lane >= N_EXPERTS) & (lane < N_EXPERTS + N_EXPERT_GROUPS)
    lg = jnp.where(is_g, logits, NEG_INF)
    eg = jnp.where(is_g, jnp.exp(lg - jnp.max(lg, -1, keepdims=True)), 0.0)
    pg = eg / jnp.sum(eg, -1, keepdims=True)
    pg_top = jnp.max(pg, -1, keepdims=True)
    gi = jnp.min(jnp.where(is_g & (pg == pg_top), lane, 4 * ROUTE_LANES), -1, keepdims=True) - N_EXPERTS
    in_grp = (lane >= gi * EXPERTS_PER_GROUP) & (lane < (gi + 1) * EXPERTS_PER_GROUP)
    le = jnp.where(in_grp, logits, NEG_INF)
    ee = jnp.where(in_grp, jnp.exp(le - jnp.max(le, -1, keepdims=True)), 0.0)
    pe = ee / jnp.sum(ee, -1, keepdims=True)
    p1 = jnp.max(jnp.where(in_grp, pe, -1.0), -1, keepdims=True)
    i1 = jnp.min(jnp.where(in_grp & (pe == p1), lane, 4 * ROUTE_LANES), -1, keepdims=True)
    rest = in_grp & (lane != i1)
    p2 = jnp.max(jnp.where(rest, pe, -1.0), -1, keepdims=True)
    i2 = jnp.min(jnp.where(rest & (pe == p2), lane, 4 * ROUTE_LANES), -1, keepdims=True)
    tot = p1 + p2
    w1 = p1 / tot * pg_top
    w2 = p2 / tot * pg_top

    tm = logits.shape[0]
    n_sub = tm // sub
    e_lane = lane & (N_EXPERTS - 1)
    sel = ((e_lane == i1) | (e_lane == i2)) & (lane < 2 * N_EXPERTS)
    sel_b = sel.astype(BF16)
    ti = lax.broadcasted_iota(jnp.int32, (tm, tm), 0)
    tj = lax.broadcasted_iota(jnp.int32, (tm, tm), 1)
    before = ((tj < ti) & ((tj // sub) == (ti // sub))).astype(BF16)
    rank = jnp.dot(before, sel_b, preferred_element_type=F32)
    si = lax.broadcasted_iota(jnp.int32, (8, tm), 0)
    sj = lax.broadcasted_iota(jnp.int32, (8, tm), 1)
    cnt = jnp.dot(((sj // sub) == si).astype(BF16), sel_b, preferred_element_type=F32)
    padded = jnp.floor((cnt + (SEG_ALIGN - 1)) * (1.0 / SEG_ALIGN)) * SEG_ALIGN
    ui = lax.broadcasted_iota(jnp.int32, (ROUTE_LANES, ROUTE_LANES), 0)
    uj = lax.broadcasted_iota(jnp.int32, (ROUTE_LANES, ROUTE_LANES), 1)
    excl = ((ui < uj) & (uj < N_EXPERTS)).astype(F32)
    lane8 = lax.broadcasted_iota(jnp.int32, (8, ROUTE_LANES), 1)
    off = jnp.dot(jnp.where(lane8 < N_EXPERTS, padded, 0.0), excl, precision=HI, preferred_element_type=F32)
    nblk = jnp.floor((cnt + (win - 1)) * (1.0 / win))
    meta_ref[...] = jnp.where(lane8 < N_EXPERTS, off, nblk).astype(jnp.int32)
    sub_id = lax.broadcasted_iota(jnp.int32, (tm, 1), 0) // sub
    off_tok = jnp.zeros_like(rank)
    for s in range(n_sub):
        off_tok = jnp.where(sub_id == s, off[s:s + 1, :], off_tok)
    dest = off_tok + rank
    d1 = jnp.sum(jnp.where(lane == i1, dest, 0.0), -1, keepdims=True)
    d2 = jnp.sum(jnp.where(lane == i2, dest, 0.0), -1, keepdims=True)
    route = (jnp.where(lane == 0, w1, 0.0) + jnp.where(lane == 1, w2, 0.0)
             + jnp.where(lane == 2, d1, 0.0) + jnp.where(lane == 3, d2, 0.0))
    route_ref[...] = route
    pick = (lane8 == lax.broadcasted_iota(jnp.int32, (8, ROUTE_LANES), 0)).astype(F32)
    routet_ref[...] = lax.dot_general(pick, route, (((1,), (1,)), ((), ())), precision=HI,
                                      preferred_element_type=F32)


def _mid(x2d, att2d, rw2d, kv, p, tm, t, sub, win):
    n = x2d.shape[0]
    per_b = t // tm
    nt = n // tm
    vec = _const_spec((1, D_MODEL))
    return pl.pallas_call(
        functools.partial(_mid_kernel, sub=sub, win=win),
        grid=(nt,),
        in_specs=[pl.BlockSpec((tm, D_MODEL), lambda i: (i, 0)),
                  pl.BlockSpec((tm, D_ATT), lambda i: (i, 0)),
                  pl.BlockSpec((tm, D_RWKV), lambda i: (i, 0)),
                  pl.BlockSpec((N_MEM, 2 * D_CROSS), lambda i: (i // per_b, 0)),
                  _const_spec((D_MODEL, D_MODEL)), vec, vec,
                  _const_spec((D_MODEL, D_CROSS)), _const_spec((D_CROSS, D_MODEL)), vec, vec,
                  _const_spec((D_MODEL, ROUTE_LANES)), _const_spec((1, ROUTE_LANES))],
        out_specs=[pl.BlockSpec((tm, D_MODEL), lambda i: (i, 0)),
                   pl.BlockSpec((tm, ROUTE_LANES), lambda i: (i, 0)),
                   pl.BlockSpec((8, tm), lambda i: (i, 0)),
                   pl.BlockSpec((8, ROUTE_LANES), lambda i: (i, 0))],
        out_shape=[jax.ShapeDtypeStruct((n, D_MODEL), F32),
                   jax.ShapeDtypeStruct((n, ROUTE_LANES), F32),
                   jax.ShapeDtypeStruct((nt * 8, tm), F32),
                   jax.ShapeDtypeStruct((nt * 8, ROUTE_LANES), jnp.int32)],
        compiler_params=_cparams("arbitrary"),
        name="mid",
    )(x2d, att2d, rw2d, kv, p['w_out'], p['ln1_g'], p['ln1_b'], p['w_cq'], p['w_co'], p['ln2_g'], p['ln2_b'],
      p['w_route'], p['b_route'])


def _moe_kernel(meta_ref, x_ref, route_ref, routet_ref, wg_ref, wu_ref, wd_ref, l3g_ref, l3b_ref, o_ref,
                xs_ref, ys_ref, *, sub, win, rows):
    i = pl.program_id(0)
    e = pl.program_id(1)
    n_sub = xs_ref.shape[0]
    dummy = rows - win

    @pl.when(e == 0)
    def _():
        ys_ref[...] = jnp.zeros_like(ys_ref)
        r_id = lax.broadcasted_iota(jnp.int32, (rows, sub), 0).astype(F32)
        for s in range(n_sub):
            d1 = routet_ref[2:3, s * sub:(s + 1) * sub]
            d2 = routet_ref[3:4, s * sub:(s + 1) * sub]
            perm = ((r_id == d1) | (r_id == d2)).astype(BF16)
            xb = x_ref[s * sub:(s + 1) * sub, :].astype(BF16)
            xs_ref[s] = jnp.dot(perm, xb, preferred_element_type=F32).astype(BF16)

    base = i * (n_sub * 32)
    offs = [meta_ref[base + s * 32 + e] for s in range(n_sub)]
    nbs = [meta_ref[base + s * 32 + N_EXPERTS + e] for s in range(n_sub)]
    nb_max = functools.reduce(jnp.maximum, nbs)

    def block(j, carry):
        starts = [pl.multiple_of(jnp.where(j < nbs[s], offs[s] + j * win, dummy), SEG_ALIGN) for s in range(n_sub)]
        xw = jnp.concatenate([xs_ref[s, pl.ds(starts[s], win), :] for s in range(n_sub)], axis=0)
        gate = jnp.dot(xw, wg_ref[0], preferred_element_type=F32)
        up = jnp.dot(xw, wu_ref[0], preferred_element_type=F32)
        hdn = (gate * jax.nn.sigmoid(gate)) * up
        ye = jnp.dot(hdn.astype(BF16), wd_ref[0], preferred_element_type=F32).astype(BF16)
        for s in range(n_sub):
            ys_ref[s, pl.ds(starts[s], win), :] = ye[s * win:(s + 1) * win]
        return carry

    lax.fori_loop(0, nb_max, block, 0)

    @pl.when(e == N_EXPERTS - 1)
    def _():
        c_id = lax.broadcasted_iota(jnp.int32, (sub, rows), 1).astype(F32)
        for s in range(n_sub):
            rt = route_ref[s * sub:(s + 1) * sub, :]
            w1, w2, d1, d2 = rt[:, 0:1], rt[:, 1:2], rt[:, 2:3], rt[:, 3:4]
            comb = jnp.where(c_id == d1, w1, 0.0) + jnp.where(c_id == d2, w2, 0.0)
            hi = comb.astype(BF16)
            lo = (comb - hi.astype(F32)).astype(BF16)
            ysv = ys_ref[s]
            ff = (jnp.dot(hi, ysv, preferred_element_type=F32) + jnp.dot(lo, ysv, preferred_element_type=F32))
            xr = x_ref[s * sub:(s + 1) * sub, :]
            o_ref[s * sub:(s + 1) * sub, :] = _ln(DEEPNORM_ALPHA * xr + ff, l3g_ref[...], l3b_ref[...])


def _moe(x2, route, routet, meta, p, tm, sub, win):
    n = x2.shape[0]
    nt = n // tm
    n_sub = tm // sub
    rows = 2 * sub + N_EXPERTS * SEG_ALIGN + 2 * win
    meta_flat = meta.reshape(nt, 8, ROUTE_LANES)[:, :n_sub, :32].reshape(-1)
    vec = pl.BlockSpec((1, D_MODEL), lambda i, e, m: (0, 0))
    grid_spec = pltpu.PrefetchScalarGridSpec(
        num_scalar_prefetch=1,
        grid=(nt, N_EXPERTS),
        in_specs=[pl.BlockSpec((tm, D_MODEL), lambda i, e, m: (i, 0)),
                  pl.BlockSpec((tm, ROUTE_LANES), lambda i, e, m: (i, 0)),
                  pl.BlockSpec((8, tm), lambda i, e, m: (i, 0)),
                  pl.BlockSpec((1, D_MODEL, D_EXPERT), lambda i, e, m: (e, 0, 0)),
                  pl.BlockSpec((1, D_MODEL, D_EXPERT), lambda i, e, m: (e, 0, 0)),
                  pl.BlockSpec((1, D_EXPERT, D_MODEL), lambda i, e, m: (e, 0, 0)),
                  vec, vec],
        out_specs=pl.BlockSpec((tm, D_MODEL), lambda i, e, m: (i, 0)),
        scratch_shapes=[pltpu.VMEM((n_sub, rows, D_MODEL), BF16),
                        pltpu.VMEM((n_sub, rows, D_MODEL), BF16)])
    return pl.pallas_call(
        functools.partial(_moe_kernel, sub=sub, win=win, rows=rows),
        grid_spec=grid_spec,
        out_shape=jax.ShapeDtypeStruct((n, D_MODEL), F32),
        compiler_params=_cparams("arbitrary", "arbitrary"),
        name="moe",
    )(meta_flat, x2, route, routet, p['w_gate'], p['w_up'], p['w_down'], p['ln3_g'], p['ln3_b'])


def _prep_params(w_in, tshift_mu, attn_sink, rwkv_w0, rwkv_w_up, rwkv_a0, rwkv_a_up, rwkv_g_up, rwkv_k_k,
                 rwkv_k_a, rwkv_r_k, rwkv_gn_g, rwkv_gn_b, w_out, ln1_g, ln1_b, mem_ln_g, mem_ln_b, w_cq, w_ckv,
                 w_co, ln2_g, ln2_b, w_route_group, b_route_group, w_route_expert, b_route_expert,
                 w_exp_gate, w_exp_up, w_exp_down, ln3_g, ln3_b):
    row = lambda a: a.reshape(1, -1).astype(F32)
    zeros_lora = jnp.zeros((2, W_LORA, D_RWKV), F32)
    hid = jnp.arange(D_RWKV) // HEAD_DIM
    pad = ROUTE_LANES - N_EXPERTS - N_EXPERT_GROUPS
    return {
        'w_in_att': w_in[:, :D_QKV].astype(BF16),
        'w_in_rw': w_in[:, D_QKV:].astype(BF16),
        'sink': attn_sink.astype(F32),
        'mu': tshift_mu.astype(F32),
        'w0': rwkv_w0.reshape(2, 1, D_RWKV).astype(F32),
        'rw_w_up': jnp.concatenate([rwkv_w_up, zeros_lora], axis=1).astype(BF16),
        'a0': rwkv_a0.reshape(2, 1, D_RWKV).astype(F32),
        'rw_a_up': jnp.concatenate([zeros_lora, rwkv_a_up], axis=1).astype(BF16),
        'g_up': rwkv_g_up.astype(BF16),
        'k_k': row(rwkv_k_k), 'k_a': row(rwkv_k_a), 'r_k': row(rwkv_r_k),
        'gn_g': row(rwkv_gn_g), 'gn_b': row(rwkv_gn_b),
        'bd': (hid[:, None] == hid[None, :]).astype(BF16),
        'w_out': w_out.astype(BF16),
        'ln1_g': row(ln1_g), 'ln1_b': row(ln1_b),
        'mem_ln_g': row(mem_ln_g), 'mem_ln_b': row(mem_ln_b),
        'w_cq': w_cq.astype(BF16), 'w_ckv': w_ckv.astype(BF16), 'w_co': w_co.astype(BF16),
        'ln2_g': row(ln2_g), 'ln2_b': row(ln2_b),
        'w_route': jnp.pad(jnp.concatenate([w_route_expert, w_route_group], axis=1), ((0, 0), (0, pad))).astype(F32),
        'b_route': jnp.pad(jnp.concatenate([b_route_expert, b_route_group]), (0, pad)).reshape(1, -1).astype(F32),
        'w_gate': w_exp_gate.astype(BF16), 'w_up': w_exp_up.astype(BF16), 'w_down': w_exp_down.astype(BF16),
        'ln3_g': row(ln3_g), 'ln3_b': row(ln3_b),
    }


def _tile(n, pref):
    t = pref
    while n % t:
        t //= 2
    return t


def _layer(x, mem, p):
    b, t, _ = x.shape
    n = b * t
    x2d = x.reshape(n, D_MODEL)
    tm = _tile(t, 512)
    qkv, zr = _inproj(x2d, p['w_in_att'], p['w_in_rw'], tm)
    att = _attention(qkv.reshape(b, t, D_QKV), p['sink'])
    zr3 = zr.reshape(b, t, D_RWKV_IN)
    lt = _tile(t, 512)
    y_fwd = _rwkv_dir(zr3, None, p, 0, lt)
    rw = _rwkv_dir(zr3, y_fwd, p, 1, lt)
    kv = _memkv(mem.reshape(b * N_MEM, D_MODEL), p['mem_ln_g'], p['mem_ln_b'], p['w_ckv'])
    tme = _tile(t, 1024)
    sub = _tile(tme, 256)
    win = sub // 4
    x2, route, routet, meta = _mid(x2d, att.reshape(n, D_ATT), rw.reshape(n, D_RWKV), kv, p, tme, t, sub, win)
    y = _moe(x2, route, routet, meta, p, tme, sub, win)
    return y.reshape(b, t, D_MODEL)


def kernel(x_prompt, x_sample, mem_prompt, mem_sample, w_in, tshift_mu, attn_sink, rwkv_w0, rwkv_w_up, rwkv_a0, rwkv_a_up, rwkv_g_up, rwkv_k_k, rwkv_k_a, rwkv_r_k, rwkv_gn_g, rwkv_gn_b, w_out, ln1_g, ln1_b, mem_ln_g, mem_ln_b, w_cq, w_ckv, w_co, ln2_g, ln2_b, w_route_group, b_route_group, w_route_expert, b_route_expert, w_exp_gate, w_exp_up, w_exp_down, ln3_g, ln3_b):
    weights = (w_in, tshift_mu, attn_sink, rwkv_w0, rwkv_w_up, rwkv_a0, rwkv_a_up, rwkv_g_up, rwkv_k_k, rwkv_k_a,
               rwkv_r_k, rwkv_gn_g, rwkv_gn_b, w_out, ln1_g, ln1_b, mem_ln_g, mem_ln_b, w_cq, w_ckv, w_co,
               ln2_g, ln2_b, w_route_group, b_route_group, w_route_expert, b_route_expert,
               w_exp_gate, w_exp_up, w_exp_down, ln3_g, ln3_b)
    p = _prep_params(*[w[0] for w in weights])
    return (_layer(x_prompt, mem_prompt, p), _layer(x_sample, mem_sample, p))
```

```python
import functools

import jax
import jax.numpy as jnp
from jax import lax
from jax.experimental import pallas as pl
from jax.experimental.pallas import tpu as pltpu

F32 = jnp.float32
BF16 = jnp.bfloat16
HI = lax.Precision.HIGHEST

D_MODEL = 1024
HEAD_DIM = 64
ATT_Q_HEADS = 8
ATT_KV_HEADS = 2
ATT_GROUP = ATT_Q_HEADS // ATT_KV_HEADS
WINDOW = 128
BLOCK = 128
RWKV_HEADS = 8
D_ATT = ATT_Q_HEADS * HEAD_DIM
D_KV = ATT_KV_HEADS * HEAD_DIM
D_QKV = D_ATT + 2 * D_KV
D_RWKV = RWKV_HEADS * HEAD_DIM
W_LORA = 64
A_LORA = 64
G_LORA = 128
D_RWKV_IN = 3 * D_RWKV + W_LORA + A_LORA + G_LORA
N_MEM = 256
CROSS_HEADS = 4
CROSS_HEAD_DIM = 128
D_CROSS = CROSS_HEADS * CROSS_HEAD_DIM
N_EXPERT_GROUPS = 4
EXPERTS_PER_GROUP = 4
N_EXPERTS = N_EXPERT_GROUPS * EXPERTS_PER_GROUP
D_EXPERT = 512
LN_EPS = 1e-5
GN_EPS = 64e-5
DEEPNORM_ALPHA = 2.0 ** 0.25
NEG_INF = -1e30
ROUTE_LANES = 128
CHUNK = 64
SEG_ALIGN = 16
VMEM_LIMIT = 56 * 1024 * 1024


def _cparams(*sem):
    return pltpu.CompilerParams(dimension_semantics=sem, vmem_limit_bytes=VMEM_LIMIT)


def _ln(x, g, b):
    mu = jnp.mean(x, -1, keepdims=True)
    xc = x - mu
    var = jnp.mean(xc * xc, -1, keepdims=True)
    return xc * lax.rsqrt(var + LN_EPS) * g + b


def _const_spec(shape):
    nd = len(shape)
    return pl.BlockSpec(shape, lambda *_: (0,) * nd)


def _inproj_kernel(x_ref, wa_ref, wr_ref, qkv_ref, zr_ref):
    xb = x_ref[...].astype(BF16)
    qkv_ref[...] = jnp.dot(xb, wa_ref[...], preferred_element_type=F32).astype(BF16)
    zr_ref[...] = jnp.dot(xb, wr_ref[...], preferred_element_type=F32)


def _inproj(x2d, w_att, w_rw, tm):
    n = x2d.shape[0]
    return pl.pallas_call(
        _inproj_kernel,
        grid=(n // tm,),
        in_specs=[pl.BlockSpec((tm, D_MODEL), lambda i: (i, 0)),
                  _const_spec((D_MODEL, D_QKV)),
                  _const_spec((D_MODEL, D_RWKV_IN))],
        out_specs=[pl.BlockSpec((tm, D_QKV), lambda i: (i, 0)),
                   pl.BlockSpec((tm, D_RWKV_IN), lambda i: (i, 0))],
        out_shape=[jax.ShapeDtypeStruct((n, D_QKV), BF16),
                   jax.ShapeDtypeStruct((n, D_RWKV_IN), F32)],
        compiler_params=_cparams("arbitrary"),
        name="inproj",
    )(x2d, w_att, w_rw)


def _attn_kernel(sink_ref, cur_ref, prv_ref, nxt_ref, o_ref, *, ns, qb):
    n = pl.program_id(1)
    blocks = [prv_ref[0]] + [cur_ref[0, a * BLOCK:(a + 1) * BLOCK, :] for a in range(qb)] + [nxt_ref[0]]
    qi = lax.broadcasted_iota(jnp.int32, (BLOCK, 3 * BLOCK), 0)
    ki = lax.broadcasted_iota(jnp.int32, (BLOCK, 3 * BLOCK), 1)
    adist = jnp.abs(qi - ki + BLOCK)
    band = adist <= WINDOW
    adist_f = adist.astype(F32)
    for a in range(qb):
        valid = band
        if a == 0:
            valid = valid & ((ki >= BLOCK) | (n > 0))
        if a == qb - 1:
            valid = valid & ((ki < 2 * BLOCK) | (n < ns - 1))
        cur = blocks[a + 1]
        for h in range(ATT_KV_HEADS):
            k0 = D_ATT + h * HEAD_DIM
            v0 = D_ATT + D_KV + h * HEAD_DIM
            kcat = jnp.concatenate([blk[:, k0:k0 + HEAD_DIM] for blk in blocks[a:a + 3]], axis=0)
            vcat = jnp.concatenate([blk[:, v0:v0 + HEAD_DIM] for blk in blocks[a:a + 3]], axis=0)
            for g in range(ATT_GROUP):
                hq = h * ATT_GROUP + g
                slope = 2.0 ** (-8.0 / ATT_Q_HEADS * (hq + 1))
                q = cur[:, hq * HEAD_DIM:(hq + 1) * HEAD_DIM] * (HEAD_DIM ** -0.5)
                s = lax.dot_general(q, kcat, (((1,), (1,)), ((), ())), preferred_element_type=F32)
                s = jnp.where(valid, s - slope * adist_f, NEG_INF)
                sk = sink_ref[hq]
                m = jnp.maximum(jnp.max(s, -1, keepdims=True), sk)
                p = jnp.exp(s - m)
                den = jnp.sum(p, -1, keepdims=True) + jnp.exp(sk - m)
                o = jnp.dot(p.astype(BF16), vcat, preferred_element_type=F32) / den
                o_ref[0, a * BLOCK:(a + 1) * BLOCK, hq * HEAD_DIM:(hq + 1) * HEAD_DIM] = o.astype(BF16)


def _attention(qkv, sink):
    b, t, _ = qkv.shape
    nb = t // BLOCK
    qb = _tile(nb, 4)
    ns = nb // qb
    return pl.pallas_call(
        functools.partial(_attn_kernel, ns=ns, qb=qb),
        grid=(b, ns),
        in_specs=[pl.BlockSpec(memory_space=pltpu.SMEM),
                  pl.BlockSpec((1, qb * BLOCK, D_QKV), lambda i, j: (i, j, 0)),
                  pl.BlockSpec((1, BLOCK, D_QKV), lambda i, j: (i, jnp.maximum(j * qb - 1, 0), 0)),
                  pl.BlockSpec((1, BLOCK, D_QKV), lambda i, j: (i, jnp.minimum((j + 1) * qb, nb - 1), 0))],
        out_specs=pl.BlockSpec((1, qb * BLOCK, D_ATT), lambda i, j: (i, j, 0)),
        out_shape=jax.ShapeDtypeStruct((b, t, D_ATT), BF16),
        compiler_params=_cparams("arbitrary", "arbitrary"),
        name="win_attn",
    )(sink, qkv, qkv, qkv)


def _heads(x):
    return jnp.stack([x[:, h * HEAD_DIM:(h + 1) * HEAD_DIM] for h in range(RWKV_HEADS)], axis=0)


def _bmm(a, b):
    return jnp.einsum('hlj,hjm->hlm', a.astype(BF16), b.astype(BF16), preferred_element_type=F32)


def _split3(x):
    hi = x.astype(BF16)
    r1 = x - hi.astype(F32)
    mid = r1.astype(BF16)
    lo = (r1 - mid.astype(F32)).astype(BF16)
    return hi, mid, lo


def _rwkv_kernel(*refs, reverse, lt, nt, final, grp):
    if final:
        (zc_ref, zp_ref, zn_ref, mu_ref, w0_ref, wup_ref, a0_ref, aup_ref, gup_ref, kk_ref, ka_ref, rk_ref,
         gng_ref, gnb_ref, bd_ref, yf_ref, out_ref,
         st_ref, y_scr, r_s, kd_s, v_s, a_s, b_s, lw_s, bonus_s, gate_s) = refs
    else:
        (zc_ref, zp_ref, zn_ref, mu_ref, w0_ref, wup_ref, a0_ref, aup_ref, kk_ref, ka_ref, bd_ref,
         out_ref, st_ref, y_scr, r_s, kd_s, v_s, a_s, b_s, lw_s) = refs
    i = pl.program_id(1)
    tt = (nt - 1 - i) if reverse else i

    @pl.when(i == 0)
    def _():
        st_ref[...] = jnp.zeros_like(st_ref)

    bd = bd_ref[...]

    def head_sum(x):
        hi = x.astype(BF16)
        lo = (x - hi.astype(F32)).astype(BF16)
        return jnp.dot(hi, bd, preferred_element_type=F32) + jnp.dot(lo, bd, preferred_element_type=F32)

    z = zc_ref[0]
    row = lax.broadcasted_iota(jnp.int32, (lt, 1), 0)
    zprev_edge = jnp.where(tt > 0, zp_ref[0, 7:8, :], 0.0)
    znext_edge = jnp.where(tt < nt - 1, zn_ref[0, 0:1, :], 0.0)
    prev = jnp.where(row == 0, zprev_edge, pltpu.roll(z, 1, 0))
    nxt = jnp.where(row == lt - 1, znext_edge, pltpu.roll(z, lt - 1, 0))
    zs = z + mu_ref[0:1, :] * (prev - z) + mu_ref[1:2, :] * (nxt - z)

    o1, o2, o3 = D_RWKV, 2 * D_RWKV, 3 * D_RWKV
    r = zs[:, :o1]
    k = zs[:, o1:o2]
    v = zs[:, o2:o3]
    wad = zs[:, o3:o3 + W_LORA + A_LORA]
    kk = k * kk_ref[...]
    kk = kk * lax.rsqrt(head_sum(kk * kk) + 1e-12)
    w_arg = w0_ref[...] + jnp.dot(jnp.tanh(wad).astype(BF16), wup_ref[...], preferred_element_type=F32)
    a_sig = jax.nn.sigmoid(a0_ref[...] + jnp.dot(wad.astype(BF16), aup_ref[...], preferred_element_type=F32))
    r_s[...] = r
    v_s[...] = v
    kd_s[...] = k * (1.0 + (a_sig - 1.0) * ka_ref[...])
    a_s[...] = -kk
    b_s[...] = kk * a_sig
    lw_s[...] = -jnp.exp(-jax.nn.softplus(-w_arg) - 0.5)
    if final:
        bonus_s[...] = head_sum(r * k * rk_ref[...]) * v
        gd = zs[:, o3 + W_LORA + A_LORA:]
        gate_s[...] = jnp.dot(jax.nn.sigmoid(gd).astype(BF16), gup_ref[...], preferred_element_type=F32)

    span = grp * CHUNK
    li = lax.broadcasted_iota(jnp.int32, (2 * CHUNK, 2 * CHUNK), 0)
    lj = lax.broadcasted_iota(jnp.int32, (2 * CHUNK, 2 * CHUNK), 1)
    lt_, jt_ = li & (CHUNK - 1), lj & (CHUNK - 1)
    if reverse:
        strict, incl = lt_ < jt_, lt_ <= jt_
    else:
        strict, incl = lt_ > jt_, lt_ >= jt_
    pmask = strict | (incl & (li >= CHUNK))
    si = lax.broadcasted_iota(jnp.int32, (span, span), 0)
    sj = lax.broadcasted_iota(jnp.int32, (span, span), 1)
    same = (si // CHUNK) == (sj // CHUNK)
    tri = (same & ((si <= sj) if reverse else (si >= sj))).astype(BF16)
    oi = lax.broadcasted_iota(jnp.int32, (span, grp * 2 * CHUNK), 0)
    oj = lax.broadcasted_iota(jnp.int32, (span, grp * 2 * CHUNK), 1)
    ones_blk = ((oi // CHUNK) == (oj // (2 * CHUNK))).astype(BF16)
    ci = lax.broadcasted_iota(jnp.int32, (CHUNK, CHUNK), 0)
    cj = lax.broadcasted_iota(jnp.int32, (CHUNK, CHUNK), 1)
    eye = (ci == cj).astype(F32)
    zeros_h = jnp.zeros((grp * RWKV_HEADS, CHUNK, HEAD_DIM), BF16)
    nc = lt // span

    def heads(x):
        return jnp.stack([x[g * CHUNK:(g + 1) * CHUNK, h * HEAD_DIM:(h + 1) * HEAD_DIM]
                          for g in range(grp) for h in range(RWKV_HEADS)], axis=0)

    def chunk(j, carry):
        c = (nc - 1 - j) if reverse else j
        base = pl.multiple_of(c * span, span)
        rows = pl.ds(base, span)
        lw_c = lw_s[rows, :]
        parts = _split3(lw_c)
        cum = sum(jnp.dot(tri, q, preferred_element_type=F32) for q in parts)
        tot = sum(lax.dot_general(q, ones_blk, (((0,), (0,)), ((), ())), preferred_element_type=F32) for q in parts)
        g_all = jnp.exp(tot).reshape(RWKV_HEADS, HEAD_DIM, grp * 2 * CHUNK)
        e_pos = jnp.exp(cum)
        e_neg = jnp.exp(-cum)
        e_prev = jnp.exp(cum - lw_c)
        ar = jnp.concatenate([heads((a_s[rows, :] * e_prev).astype(BF16)),
                              heads((r_s[rows, :] * e_pos).astype(BF16))], axis=1)
        bk = jnp.concatenate([heads((b_s[rows, :] * e_neg).astype(BF16)),
                              heads((kd_s[rows, :] * e_neg).astype(BF16))], axis=1)
        vh = heads(v_s[rows, :].astype(BF16))
        pw = jnp.einsum('hlc,hjc->hlj', ar, bk, preferred_element_type=F32)
        pw = jnp.where(pmask, pw, 0.0)
        pw_b = pw.astype(BF16)
        m_ab = pw[:, :CHUNK, :CHUNK]
        tinv = eye + m_ab
        mp = m_ab
        for _ in range(5):
            mp = _bmm(mp, mp)
            tinv = tinv + _bmm(tinv, mp)
        tinv_b = tinv.astype(BF16)
        mv = _bmm(pw_b[:, :CHUNK], jnp.concatenate([zeros_h, vh], axis=1))

        st = st_ref[...]
        for q in (range(grp - 1, -1, -1) if reverse else range(grp)):
            hs = slice(q * RWKV_HEADS, (q + 1) * RWKV_HEADS)
            ar_st = _bmm(ar[hs], st)
            u = _bmm(tinv_b[hs], ar_st[:, :CHUNK] + mv[hs])
            uv = jnp.concatenate([u.astype(BF16), vh[hs]], axis=1)
            y = ar_st[:, CHUNK:] + _bmm(pw_b[hs][:, CHUNK:], uv)
            g_col = g_all[:, :, q * 2 * CHUNK:q * 2 * CHUNK + HEAD_DIM]
            st = g_col * (st + jnp.einsum('hjc,hjv->hcv', bk[hs], uv, preferred_element_type=F32))
            for h in range(RWKV_HEADS):
                y_scr[pl.ds(base + q * CHUNK, CHUNK), h * HEAD_DIM:(h + 1) * HEAD_DIM] = y[h]
        st_ref[...] = st
        return carry

    lax.fori_loop(0, nc, chunk, 0)

    if not final:
        out_ref[0] = y_scr[...]
    else:
        yy = yf_ref[0] + y_scr[...]
        mu_y = head_sum(yy) * (1.0 / HEAD_DIM)
        yc = yy - mu_y
        var_y = head_sum(yc * yc) * (1.0 / HEAD_DIM)
        yn = yc * lax.rsqrt(var_y + GN_EPS) * gng_ref[...] + gnb_ref[...]
        out_ref[0] = ((yn + bonus_s[...]) * gate_s[...]).astype(BF16)


def _rwkv_dir(zr, y_fwd, p, d, lt):
    b, t, _ = zr.shape
    nt = t // lt
    reverse = d == 1
    final = y_fwd is not None
    tmap = (lambda j: nt - 1 - j) if reverse else (lambda j: j)
    r8 = lt // 8
    vec = _const_spec((1, D_RWKV))
    in_specs = [pl.BlockSpec((1, lt, D_RWKV_IN), lambda i, j: (i, tmap(j), 0)),
                pl.BlockSpec((1, 8, D_RWKV_IN), lambda i, j: (i, jnp.maximum(tmap(j) * r8 - 1, 0), 0)),
                pl.BlockSpec((1, 8, D_RWKV_IN), lambda i, j: (i, jnp.minimum((tmap(j) + 1) * r8, t // 8 - 1), 0)),
                _const_spec((2, D_RWKV_IN)), vec, _const_spec((W_LORA + A_LORA, D_RWKV)), vec,
                _const_spec((W_LORA + A_LORA, D_RWKV))]
    args = [zr, zr, zr, p['mu'], p['w0'][d], p['rw_w_up'][d], p['a0'][d], p['rw_a_up'][d]]
    n_tok_scratch = 6
    if final:
        in_specs += [_const_spec((G_LORA, D_RWKV)), vec, vec, vec, vec, vec, _const_spec((D_RWKV, D_RWKV)),
                     pl.BlockSpec((1, lt, D_RWKV), lambda i, j: (i, tmap(j), 0))]
        args += [p['g_up'], p['k_k'], p['k_a'], p['r_k'], p['gn_g'], p['gn_b'], p['bd'], y_fwd]
        n_tok_scratch = 8
    else:
        in_specs += [vec, vec, _const_spec((D_RWKV, D_RWKV))]
        args += [p['k_k'], p['k_a'], p['bd']]
    return pl.pallas_call(
        functools.partial(_rwkv_kernel, reverse=reverse, lt=lt, nt=nt, final=final, grp=min(8, lt // CHUNK)),
        grid=(b, nt),
        in_specs=in_specs,
        out_specs=pl.BlockSpec((1, lt, D_RWKV), lambda i, j: (i, tmap(j), 0)),
        out_shape=jax.ShapeDtypeStruct((b, t, D_RWKV), BF16 if final else F32),
        scratch_shapes=[pltpu.VMEM((RWKV_HEADS, HEAD_DIM, HEAD_DIM), F32)]
                       + [pltpu.VMEM((lt, D_RWKV), F32)] * (1 + n_tok_scratch),
        compiler_params=_cparams("arbitrary", "arbitrary"),
        name="rwkv_bwd_final" if final else "rwkv_fwd",
    )(*args)


def _memkv_kernel(m_ref, g_ref, b_ref, w_ref, kv_ref):
    m = _ln(m_ref[...], g_ref[...], b_ref[...])
    kv_ref[...] = jnp.dot(m.astype(BF16), w_ref[...], preferred_element_type=F32).astype(BF16)


def _memkv(mem2d, g, b, w_ckv):
    n = mem2d.shape[0]
    return pl.pallas_call(
        _memkv_kernel,
        grid=(n // N_MEM,),
        in_specs=[pl.BlockSpec((N_MEM, D_MODEL), lambda i: (i, 0)),
                  _const_spec((1, D_MODEL)), _const_spec((1, D_MODEL)),
                  _const_spec((D_MODEL, 2 * D_CROSS))],
        out_specs=pl.BlockSpec((N_MEM, 2 * D_CROSS), lambda i: (i, 0)),
        out_shape=jax.ShapeDtypeStruct((n, 2 * D_CROSS), BF16),
        compiler_params=_cparams("arbitrary"),
        name="mem_kv",
    )(mem2d, g, b, w_ckv)


def _mid_kernel(x_ref, att_ref, rw_ref, kv_ref, wo_ref, l1g_ref, l1b_ref, wq_ref, wco_ref, l2g_ref, l2b_ref,
                wr_ref, br_ref, x2_ref, route_ref, routet_ref, meta_ref, *, sub, win):
    mix = (jnp.dot(att_ref[...], wo_ref[:D_ATT, :], preferred_element_type=F32)
           + jnp.dot(rw_ref[...], wo_ref[D_ATT:, :], preferred_element_type=F32))
    x1 = _ln(DEEPNORM_ALPHA * x_ref[...] + mix, l1g_ref[...], l1b_ref[...])

    q = jnp.dot(x1.astype(BF16), wq_ref[...], preferred_element_type=F32)
    q = (q * (CROSS_HEAD_DIM ** -0.5)).astype(BF16)
    kv = kv_ref[...]
    cr = None
    for h in range(CROSS_HEADS):
        hs = slice(h * CROSS_HEAD_DIM, (h + 1) * CROSS_HEAD_DIM)
        kh = kv[:, h * CROSS_HEAD_DIM:(h + 1) * CROSS_HEAD_DIM]
        vh = kv[:, D_CROSS + h * CROSS_HEAD_DIM:D_CROSS + (h + 1) * CROSS_HEAD_DIM]
        s = lax.dot_general(q[:, hs], kh, (((1,), (1,)), ((), ())), preferred_element_type=F32)
        m = jnp.max(s, -1, keepdims=True)
        p = jnp.exp(s - m)
        den = jnp.sum(p, -1, keepdims=True)
        o = jnp.dot(p.astype(BF16), vh, preferred_element_type=F32) / den
        part = jnp.dot(o.astype(BF16), wco_ref[hs, :], preferred_element_type=F32)
        cr = part if cr is None else cr + part
    x2 = _ln(DEEPNORM_ALPHA * x1 + cr, l2g_ref[...], l2b_ref[...])
    x2_ref[...] = x2

    logits = jnp.dot(x2, wr_ref[...], precision=HI, preferred_element_type=F32) + br_ref[...]
    lane = lax.broadcasted_iota(jnp.int32, logits.shape, 1)
    is_g = (lane >= N_EXPERTS) & (lane < N_EXPERTS + N_EXPERT_GROUPS)
    lg = jnp.where(is_g, logits, NEG_INF)
    eg = jnp.where(is_g, jnp.exp(lg - jnp.max(lg, -1, keepdims=True)), 0.0)
    pg = eg / jnp.sum(eg, -1, keepdims=True)
    pg_top = jnp.max(pg, -1, keepdims=True)
    gi = jnp.min(jnp.where(is_g & (pg == pg_top), lane, 4 * ROUTE_LANES), -1, keepdims=True) - N_EXPERTS
    in_grp = (lane >= gi * EXPERTS_PER_GROUP) & (lane < (gi + 1) * EXPERTS_PER_GROUP)
    le = jnp.where(in_grp, logits, NEG_INF)
    ee = jnp.where(in_grp, jnp.exp(le - jnp.max(le, -1, keepdims=True)), 0.0)
    pe = ee / jnp.sum(ee, -1, keepdims=True)
    p1 = jnp.max(jnp.where(in_grp, pe, -1.0), -1, keepdims=True)
    i1 = jnp.min(jnp.where(in_grp & (pe == p1), lane, 4 * ROUTE_LANES), -1, keepdims=True)
    rest = in_grp & (lane != i1)
    p2 = jnp.max(jnp.where(rest, pe, -1.0), -1, keepdims=True)
    i2 = jnp.min(jnp.where(rest & (pe == p2), lane, 4 * ROUTE_LANES), -1, keepdims=True)
    tot = p1 + p2
    w1 = p1 / tot * pg_top
    w2 = p2 / tot * pg_top

    tm = logits.shape[0]
    n_sub = tm // sub
    e_lane = lane & (N_EXPERTS - 1)
    sel = ((e_lane == i1) | (e_lane == i2)) & (lane < 2 * N_EXPERTS)
    sel_b = sel.astype(BF16)
    ti = lax.broadcasted_iota(jnp.int32, (tm, tm), 0)
    tj = lax.broadcasted_iota(jnp.int32, (tm, tm), 1)
    before = ((tj < ti) & ((tj // sub) == (ti // sub))).astype(BF16)
    rank = jnp.dot(before, sel_b, preferred_element_type=F32)
    si = lax.broadcasted_iota(jnp.int32, (8, tm), 0)
    sj = lax.broadcasted_iota(jnp.int32, (8, tm), 1)
    cnt = jnp.dot(((sj // sub) == si).astype(BF16), sel_b, preferred_element_type=F32)
    padded = jnp.floor((cnt + (SEG_ALIGN - 1)) * (1.0 / SEG_ALIGN)) * SEG_ALIGN
    ui = lax.broadcasted_iota(jnp.int32, (ROUTE_LANES, ROUTE_LANES), 0)
    uj = lax.broadcasted_iota(jnp.int32, (ROUTE_LANES, ROUTE_LANES), 1)
    excl = ((ui < uj) & (uj < N_EXPERTS)).astype(F32)
    lane8 = lax.broadcasted_iota(jnp.int32, (8, ROUTE_LANES), 1)
    off = jnp.dot(jnp.where(lane8 < N_EXPERTS, padded, 0.0), excl, precision=HI, preferred_element_type=F32)
    nblk = jnp.floor((cnt + (win - 1)) * (1.0 / win))
    meta_ref[...] = jnp.where(lane8 < N_EXPERTS, off, nblk).astype(jnp.int32)
    sub_id = lax.broadcasted_iota(jnp.int32, (tm, 1), 0) // sub
    off_tok = jnp.zeros_like(rank)
    for s in range(n_sub):
        off_tok = jnp.where(sub_id == s, off[s:s + 1, :], off_tok)
    dest = off_tok + rank
    d1 = jnp.sum(jnp.where(lane == i1, dest, 0.0), -1, keepdims=True)
    d2 = jnp.sum(jnp.where(lane == i2, dest, 0.0), -1, keepdims=True)
    route = (jnp.where(lane == 0, w1, 0.0) + jnp.where(lane == 1, w2, 0.0)
             + jnp.where(lane == 2, d1, 0.0) + jnp.where(lane == 3, d2, 0.0))
    route_ref[...] = route
    pick = (lane8 == lax.broadcasted_iota(jnp.int32, (8, ROUTE_LANES), 0)).astype(F32)
    routet_ref[...] = lax.dot_general(pick, route, (((1,), (1,)), ((), ())), precision=HI,
                                      preferred_element_type=F32)


def _mid(x2d, att2d, rw2d, kv, p, tm, t, sub, win):
    n = x2d.shape[0]
    per_b = t // tm
    nt = n // tm
    vec = _const_spec((1, D_MODEL))
    return pl.pallas_call(
        functools.partial(_mid_kernel, sub=sub, win=win),
        grid=(nt,),
        in_specs=[pl.BlockSpec((tm, D_MODEL), lambda i: (i, 0)),
                  pl.BlockSpec((tm, D_ATT), lambda i: (i, 0)),
                  pl.BlockSpec((tm, D_RWKV), lambda i: (i, 0)),
                  pl.BlockSpec((N_MEM, 2 * D_CROSS), lambda i: (i // per_b, 0)),
                  _const_spec((D_MODEL, D_MODEL)), vec, vec,
                  _const_spec((D_MODEL, D_CROSS)), _const_spec((D_CROSS, D_MODEL)), vec, vec,
                  _const_spec((D_MODEL, ROUTE_LANES)), _const_spec((1, ROUTE_LANES))],
        out_specs=[pl.BlockSpec((tm, D_MODEL), lambda i: (i, 0)),
                   pl.BlockSpec((tm, ROUTE_LANES), lambda i: (i, 0)),
                   pl.BlockSpec((8, tm), lambda i: (i, 0)),
                   pl.BlockSpec((8, ROUTE_LANES), lambda i: (i, 0))],
        out_shape=[jax.ShapeDtypeStruct((n, D_MODEL), F32),
                   jax.ShapeDtypeStruct((n, ROUTE_LANES), F32),
                   jax.ShapeDtypeStruct((nt * 8, tm), F32),
                   jax.ShapeDtypeStruct((nt * 8, ROUTE_LANES), jnp.int32)],
        compiler_params=_cparams("arbitrary"),
        name="mid",
    )(x2d, att2d, rw2d, kv, p['w_out'], p['ln1_g'], p['ln1_b'], p['w_cq'], p['w_co'], p['ln2_g'], p['ln2_b'],
      p['w_route'], p['b_route'])


def _moe_kernel(meta_ref, x_ref, route_ref, routet_ref, wg_ref, wu_ref, wd_ref, l3g_ref, l3b_ref, o_ref,
                xs_ref, ys_ref, *, sub, win, rows):
    i = pl.program_id(0)
    e = pl.program_id(1)
    n_sub = xs_ref.shape[0]
    dummy = rows - win

    @pl.when(e == 0)
    def _():
        ys_ref[...] = jnp.zeros_like(ys_ref)
        r_id = lax.broadcasted_iota(jnp.int32, (rows, sub), 0).astype(F32)
        for s in range(n_sub):
            d1 = routet_ref[2:3, s * sub:(s + 1) * sub]
            d2 = routet_ref[3:4, s * sub:(s + 1) * sub]
            perm = ((r_id == d1) | (r_id == d2)).astype(BF16)
            xb = x_ref[s * sub:(s + 1) * sub, :].astype(BF16)
            xs_ref[s] = jnp.dot(perm, xb, preferred_element_type=F32).astype(BF16)

    base = i * (n_sub * 32)
    offs = [meta_ref[base + s * 32 + e] for s in range(n_sub)]
    nbs = [meta_ref[base + s * 32 + N_EXPERTS + e] for s in range(n_sub)]
    nb_max = functools.reduce(jnp.maximum, nbs)

    def block(j, carry):
        starts = [pl.multiple_of(jnp.where(j < nbs[s], offs[s] + j * win, dummy), SEG_ALIGN) for s in range(n_sub)]
        xw = jnp.concatenate([xs_ref[s, pl.ds(starts[s], win), :] for s in range(n_sub)], axis=0)
        gate = jnp.dot(xw, wg_ref[0], preferred_element_type=F32)
        up = jnp.dot(xw, wu_ref[0], preferred_element_type=F32)
        hdn = (gate * jax.nn.sigmoid(gate)) * up
        ye = jnp.dot(hdn.astype(BF16), wd_ref[0], preferred_element_type=F32).astype(BF16)
        for s in range(n_sub):
            ys_ref[s, pl.ds(starts[s], win), :] = ye[s * win:(s + 1) * win]
        return carry

    lax.fori_loop(0, nb_max, block, 0)

    @pl.when(e == N_EXPERTS - 1)
    def _():
        c_id = lax.broadcasted_iota(jnp.int32, (sub, rows), 1).astype(F32)
        for s in range(n_sub):
            rt = route_ref[s * sub:(s + 1) * sub, :]
            w1, w2, d1, d2 = rt[:, 0:1], rt[:, 1:2], rt[:, 2:3], rt[:, 3:4]
            comb = jnp.where(c_id == d1, w1, 0.0) + jnp.where(c_id == d2, w2, 0.0)
            hi = comb.astype(BF16)
            lo = (comb - hi.astype(F32)).astype(BF16)
            ysv = ys_ref[s]
            ff = (jnp.dot(hi, ysv, preferred_element_type=F32) + jnp.dot(lo, ysv, preferred_element_type=F32))
            xr = x_ref[s * sub:(s + 1) * sub, :]
            o_ref[s * sub:(s + 1) * sub, :] = _ln(DEEPNORM_ALPHA * xr + ff, l3g_ref[...], l3b_ref[...])


def _moe(x2, route, routet, meta, p, tm, sub, win):
    n = x2.shape[0]
    nt = n // tm
    n_sub = tm // sub
    rows = 2 * sub + N_EXPERTS * SEG_ALIGN + 2 * win
    meta_flat = meta.reshape(nt, 8, ROUTE_LANES)[:, :n_sub, :32].reshape(-1)
    vec = pl.BlockSpec((1, D_MODEL), lambda i, e, m: (0, 0))
    grid_spec = pltpu.PrefetchScalarGridSpec(
        num_scalar_prefetch=1,
        grid=(nt, N_EXPERTS),
        in_specs=[pl.BlockSpec((tm, D_MODEL), lambda i, e, m: (i, 0)),
                  pl.BlockSpec((tm, ROUTE_LANES), lambda i, e, m: (i, 0)),
                  pl.BlockSpec((8, tm), lambda i, e, m: (i, 0)),
                  pl.BlockSpec((1, D_MODEL, D_EXPERT), lambda i, e, m: (e, 0, 0)),
                  pl.BlockSpec((1, D_MODEL, D_EXPERT), lambda i, e, m: (e, 0, 0)),
                  pl.BlockSpec((1, D_EXPERT, D_MODEL), lambda i, e, m: (e, 0, 0)),
                  vec, vec],
        out_specs=pl.BlockSpec((tm, D_MODEL), lambda i, e, m: (i, 0)),
        scratch_shapes=[pltpu.VMEM((n_sub, rows, D_MODEL), BF16),
                        pltpu.VMEM((n_sub, rows, D_MODEL), BF16)])
    return pl.pallas_call(
        functools.partial(_moe_kernel, sub=sub, win=win, rows=rows),
        grid_spec=grid_spec,
        out_shape=jax.ShapeDtypeStruct((n, D_MODEL), F32),
        compiler_params=_cparams("arbitrary", "arbitrary"),
        name="moe",
    )(meta_flat, x2, route, routet, p['w_gate'], p['w_up'], p['w_down'], p['ln3_g'], p['ln3_b'])


def _prep_params(w_in, tshift_mu, attn_sink, rwkv_w0, rwkv_w_up, rwkv_a0, rwkv_a_up, rwkv_g_up, rwkv_k_k,
                 rwkv_k_a, rwkv_r_k, rwkv_gn_g, rwkv_gn_b, w_out, ln1_g, ln1_b, mem_ln_g, mem_ln_b, w_cq, w_ckv,
                 w_co, ln2_g, ln2_b, w_route_group, b_route_group, w_route_expert, b_route_expert,
                 w_exp_gate, w_exp_up, w_exp_down, ln3_g, ln3_b):
    row = lambda a: a.reshape(1, -1).astype(F32)
    zeros_lora = jnp.zeros((2, W_LORA, D_RWKV), F32)
    hid = jnp.arange(D_RWKV) // HEAD_DIM
    pad = ROUTE_LANES - N_EXPERTS - N_EXPERT_GROUPS
    return {
        'w_in_att': w_in[:, :D_QKV].astype(BF16),
        'w_in_rw': w_in[:, D_QKV:].astype(BF16),
        'sink': attn_sink.astype(F32),
        'mu': tshift_mu.astype(F32),
        'w0': rwkv_w0.reshape(2, 1, D_RWKV).astype(F32),
        'rw_w_up': jnp.concatenate([rwkv_w_up, zeros_lora], axis=1).astype(BF16),
        'a0': rwkv_a0.reshape(2, 1, D_RWKV).astype(F32),
        'rw_a_up': jnp.concatenate([zeros_lora, rwkv_a_up], axis=1).astype(BF16),
        'g_up': rwkv_g_up.astype(BF16),
        'k_k': row(rwkv_k_k), 'k_a': row(rwkv_k_a), 'r_k': row(rwkv_r_k),
        'gn_g': row(rwkv_gn_g), 'gn_b': row(rwkv_gn_b),
        'bd': (hid[:, None] == hid[None, :]).astype(BF16),
        'w_out': w_out.astype(BF16),
        'ln1_g': row(ln1_g), 'ln1_b': row(ln1_b),
        'mem_ln_g': row(mem_ln_g), 'mem_ln_b': row(mem_ln_b),
        'w_cq': w_cq.astype(BF16), 'w_ckv': w_ckv.astype(BF16), 'w_co': w_co.astype(BF16),
        'ln2_g': row(ln2_g), 'ln2_b': row(ln2_b),
        'w_route': jnp.pad(jnp.concatenate([w_route_expert, w_route_group], axis=1), ((0, 0), (0, pad))).astype(F32),
        'b_route': jnp.pad(jnp.concatenate([b_route_expert, b_route_group]), (0, pad)).reshape(1, -1).astype(F32),
        'w_gate': w_exp_gate.astype(BF16), 'w_up': w_exp_up.astype(BF16), 'w_down': w_exp_down.astype(BF16),
        'ln3_g': row(ln3_g), 'ln3_b': row(ln3_b),
    }


def _tile(n, pref):
    t = pref
    while n % t:
        t //= 2
    return t


def _layer(x, mem, p):
    b, t, _ = x.shape
    n = b * t
    x2d = x.reshape(n, D_MODEL)
    tm = _tile(t, 512)
    qkv, zr = _inproj(x2d, p['w_in_att'], p['w_in_rw'], tm)
    att = _attention(qkv.reshape(b, t, D_QKV), p['sink'])
    zr3 = zr.reshape(b, t, D_RWKV_IN)
    lt = _tile(t, 512)
    y_fwd = _rwkv_dir(zr3, None, p, 0, lt)
    rw = _rwkv_dir(zr3, y_fwd, p, 1, lt)
    kv = _memkv(mem.reshape(b * N_MEM, D_MODEL), p['mem_ln_g'], p['mem_ln_b'], p['w_ckv'])
    tme = _tile(t, 1024)
    sub = _tile(tme, 256)
    win = sub // 4
    x2, route, routet, meta = _mid(x2d, att.reshape(n, D_ATT), rw.reshape(n, D_RWKV), kv, p, tme, t, sub, win)
    y = _moe(x2, route, routet, meta, p, tme, sub, win)
    return y.reshape(b, t, D_MODEL)


def kernel(x_prompt, x_sample, mem_prompt, mem_sample, w_in, tshift_mu, attn_sink, rwkv_w0, rwkv_w_up, rwkv_a0, rwkv_a_up, rwkv_g_up, rwkv_k_k, rwkv_k_a, rwkv_r_k, rwkv_gn_g, rwkv_gn_b, w_out, ln1_g, ln1_b, mem_ln_g, mem_ln_b, w_cq, w_ckv, w_co, ln2_g, ln2_b, w_route_group, b_route_group, w_route_expert, b_route_expert, w_exp_gate, w_exp_up, w_exp_down, ln3_g, ln3_b):
    weights = (w_in, tshift_mu, attn_sink, rwkv_w0, rwkv_w_up, rwkv_a0, rwkv_a_up, rwkv_g_up, rwkv_k_k, rwkv_k_a,
               rwkv_r_k, rwkv_gn_g, rwkv_gn_b, w_out, ln1_g, ln1_b, mem_ln_g, mem_ln_b, w_cq, w_ckv, w_co,
               ln2_g, ln2_b, w_route_group, b_route_group, w_route_expert, b_route_expert,
               w_exp_gate, w_exp_up, w_exp_down, ln3_g, ln3_b)
    p = _prep_params(*[w[0] for w in weights])
    return (_layer(x_prompt, mem_prompt, p), _layer(x_sample, mem_sample, p))
```

```python
import functools

import jax
import jax.numpy as jnp
from jax import lax
from jax.experimental import pallas as pl
from jax.experimental.pallas import tpu as pltpu

F32 = jnp.float32
BF16 = jnp.bfloat16
HI = lax.Precision.HIGHEST

D_MODEL = 1024
HEAD_DIM = 64
ATT_Q_HEADS = 8
ATT_KV_HEADS = 2
ATT_GROUP = ATT_Q_HEADS // ATT_KV_HEADS
WINDOW = 128
BLOCK = 128
RWKV_HEADS = 8
D_ATT = ATT_Q_HEADS * HEAD_DIM
D_KV = ATT_KV_HEADS * HEAD_DIM
D_QKV = D_ATT + 2 * D_KV
D_RWKV = RWKV_HEADS * HEAD_DIM
W_LORA = 64
A_LORA = 64
G_LORA = 128
D_RWKV_IN = 3 * D_RWKV + W_LORA + A_LORA + G_LORA
N_MEM = 256
CROSS_HEADS = 4
CROSS_HEAD_DIM = 128
D_CROSS = CROSS_HEADS * CROSS_HEAD_DIM
N_EXPERT_GROUPS = 4
EXPERTS_PER_GROUP = 4
N_EXPERTS = N_EXPERT_GROUPS * EXPERTS_PER_GROUP
D_EXPERT = 512
LN_EPS = 1e-5
GN_EPS = 64e-5
DEEPNORM_ALPHA = 2.0 ** 0.25
NEG_INF = -1e30
ROUTE_LANES = 128
CHUNK = 64
SEG_ALIGN = 16
VMEM_LIMIT = 56 * 1024 * 1024


def _cparams(*sem):
    return pltpu.CompilerParams(dimension_semantics=sem, vmem_limit_bytes=VMEM_LIMIT)


def _ln(x, g, b):
    mu = jnp.mean(x, -1, keepdims=True)
    xc = x - mu
    var = jnp.mean(xc * xc, -1, keepdims=True)
    return xc * lax.rsqrt(var + LN_EPS) * g + b


def _const_spec(shape):
    nd = len(shape)
    return pl.BlockSpec(shape, lambda *_: (0,) * nd)


def _inproj_kernel(x_ref, wa_ref, wr_ref, qkv_ref, zr_ref):
    xb = x_ref[...].astype(BF16)
    qkv_ref[...] = jnp.dot(xb, wa_ref[...], preferred_element_type=F32).astype(BF16)
    zr_ref[...] = jnp.dot(xb, wr_ref[...], preferred_element_type=F32)


def _inproj(x2d, w_att, w_rw, tm):
    n = x2d.shape[0]
    return pl.pallas_call(
        _inproj_kernel,
        grid=(n // tm,),
        in_specs=[pl.BlockSpec((tm, D_MODEL), lambda i: (i, 0)),
                  _const_spec((D_MODEL, D_QKV)),
                  _const_spec((D_MODEL, D_RWKV_IN))],
        out_specs=[pl.BlockSpec((tm, D_QKV), lambda i: (i, 0)),
                   pl.BlockSpec((tm, D_RWKV_IN), lambda i: (i, 0))],
        out_shape=[jax.ShapeDtypeStruct((n, D_QKV), BF16),
                   jax.ShapeDtypeStruct((n, D_RWKV_IN), F32)],
        compiler_params=_cparams("arbitrary"),
        name="inproj",
    )(x2d, w_att, w_rw)


def _attn_kernel(sink_ref, cur_ref, prv_ref, nxt_ref, o_ref, *, ns, qb):
    n = pl.program_id(1)
    blocks = [prv_ref[0]] + [cur_ref[0, a * BLOCK:(a + 1) * BLOCK, :] for a in range(qb)] + [nxt_ref[0]]
    rows = ATT_GROUP * BLOCK
    ri = lax.broadcasted_iota(jnp.int32, (rows, 3 * BLOCK), 0)
    ki = lax.broadcasted_iota(jnp.int32, (rows, 3 * BLOCK), 1)
    adist = jnp.abs((ri & (BLOCK - 1)) - ki + BLOCK)
    band = adist <= WINDOW
    adist_f = adist.astype(F32)
    grp_id = lax.broadcasted_iota(jnp.int32, (rows, 1), 0) // BLOCK
    biases, sinks = [], []
    for h in range(ATT_KV_HEADS):
        slope = jnp.zeros((rows, 1), F32)
        sink = jnp.zeros((rows, 1), F32)
        for g in range(ATT_GROUP):
            hq = h * ATT_GROUP + g
            slope = jnp.where(grp_id == g, 2.0 ** (-8.0 / ATT_Q_HEADS * (hq + 1)), slope)
            sink = jnp.where(grp_id == g, sink_ref[hq], sink)
        biases.append(slope * adist_f)
        sinks.append(sink)
    for a in range(qb):
        valid = band
        if a == 0:
            valid = valid & ((ki >= BLOCK) | (n > 0))
        if a == qb - 1:
            valid = valid & ((ki < 2 * BLOCK) | (n < ns - 1))
        cur = blocks[a + 1]
        for h in range(ATT_KV_HEADS):
            k0 = D_ATT + h * HEAD_DIM
            v0 = D_ATT + D_KV + h * HEAD_DIM
            kcat = jnp.concatenate([blk[:, k0:k0 + HEAD_DIM] for blk in blocks[a:a + 3]], axis=0)
            vcat = jnp.concatenate([blk[:, v0:v0 + HEAD_DIM] for blk in blocks[a:a + 3]], axis=0)
            q0 = h * ATT_GROUP * HEAD_DIM
            q = jnp.concatenate([cur[:, q0 + g * HEAD_DIM:q0 + (g + 1) * HEAD_DIM] for g in range(ATT_GROUP)], axis=0)
            q = q * (HEAD_DIM ** -0.5)
            s = lax.dot_general(q, kcat, (((1,), (1,)), ((), ())), preferred_element_type=F32)
            s = jnp.where(valid, s - biases[h], NEG_INF)
            sk = sinks[h]
            m = jnp.maximum(jnp.max(s, -1, keepdims=True), sk)
            p = jnp.exp(s - m)
            den = jnp.sum(p, -1, keepdims=True) + jnp.exp(sk - m)
            o = (jnp.dot(p.astype(BF16), vcat, preferred_element_type=F32) / den).astype(BF16)
            for g in range(ATT_GROUP):
                o_ref[0, a * BLOCK:(a + 1) * BLOCK, q0 + g * HEAD_DIM:q0 + (g + 1) * HEAD_DIM] = (
                    o[g * BLOCK:(g + 1) * BLOCK])


def _attention(qkv, sink):
    b, t, _ = qkv.shape
    nb = t // BLOCK
    qb = _tile(nb, 4)
    ns = nb // qb
    return pl.pallas_call(
        functools.partial(_attn_kernel, ns=ns, qb=qb),
        grid=(b, ns),
        in_specs=[pl.BlockSpec(memory_space=pltpu.SMEM),
                  pl.BlockSpec((1, qb * BLOCK, D_QKV), lambda i, j: (i, j, 0)),
                  pl.BlockSpec((1, BLOCK, D_QKV), lambda i, j: (i, jnp.maximum(j * qb - 1, 0), 0)),
                  pl.BlockSpec((1, BLOCK, D_QKV), lambda i, j: (i, jnp.minimum((j + 1) * qb, nb - 1), 0))],
        out_specs=pl.BlockSpec((1, qb * BLOCK, D_ATT), lambda i, j: (i, j, 0)),
        out_shape=jax.ShapeDtypeStruct((b, t, D_ATT), BF16),
        compiler_params=_cparams("arbitrary", "arbitrary"),
        name="win_attn",
    )(sink, qkv, qkv, qkv)


def _heads(x):
    return jnp.stack([x[:, h * HEAD_DIM:(h + 1) * HEAD_DIM] for h in range(RWKV_HEADS)], axis=0)


def _bmm(a, b):
    return jnp.einsum('hlj,hjm->hlm', a.astype(BF16), b.astype(BF16), preferred_element_type=F32)


def _split3(x):
    hi = x.astype(BF16)
    r1 = x - hi.astype(F32)
    mid = r1.astype(BF16)
    lo = (r1 - mid.astype(F32)).astype(BF16)
    return hi, mid, lo


def _rwkv_kernel(*refs, reverse, lt, nt, final, grp):
    if final:
        (zc_ref, zp_ref, zn_ref, mu_ref, w0_ref, wup_ref, a0_ref, aup_ref, gup_ref, kk_ref, ka_ref, rk_ref,
         gng_ref, gnb_ref, bd_ref, yf_ref, out_ref,
         st_ref, y_scr, r_s, kd_s, v_s, a_s, b_s, lw_s, bonus_s, gate_s) = refs
    else:
        (zc_ref, zp_ref, zn_ref, mu_ref, w0_ref, wup_ref, a0_ref, aup_ref, kk_ref, ka_ref, bd_ref,
         out_ref, st_ref, y_scr, r_s, kd_s, v_s, a_s, b_s, lw_s) = refs
    i = pl.program_id(1)
    tt = (nt - 1 - i) if reverse else i

    @pl.when(i == 0)
    def _():
        st_ref[...] = jnp.zeros_like(st_ref)

    bd = bd_ref[...]

    def head_sum(x):
        hi = x.astype(BF16)
        lo = (x - hi.astype(F32)).astype(BF16)
        return jnp.dot(hi, bd, preferred_element_type=F32) + jnp.dot(lo, bd, preferred_element_type=F32)

    z = zc_ref[0]
    row = lax.broadcasted_iota(jnp.int32, (lt, 1), 0)
    zprev_edge = jnp.where(tt > 0, zp_ref[0, 7:8, :], 0.0)
    znext_edge = jnp.where(tt < nt - 1, zn_ref[0, 0:1, :], 0.0)
    prev = jnp.where(row == 0, zprev_edge, pltpu.roll(z, 1, 0))
    nxt = jnp.where(row == lt - 1, znext_edge, pltpu.roll(z, lt - 1, 0))
    zs = z + mu_ref[0:1, :] * (prev - z) + mu_ref[1:2, :] * (nxt - z)

    o1, o2, o3 = D_RWKV, 2 * D_RWKV, 3 * D_RWKV
    r = zs[:, :o1]
    k = zs[:, o1:o2]
    v = zs[:, o2:o3]
    wad = zs[:, o3:o3 + W_LORA + A_LORA]
    kk = k * kk_ref[...]
    kk = kk * lax.rsqrt(head_sum(kk * kk) + 1e-12)
    w_arg = w0_ref[...] + jnp.dot(jnp.tanh(wad).astype(BF16), wup_ref[...], preferred_element_type=F32)
    a_sig = jax.nn.sigmoid(a0_ref[...] + jnp.dot(wad.astype(BF16), aup_ref[...], preferred_element_type=F32))
    r_s[...] = r
    v_s[...] = v
    kd_s[...] = k * (1.0 + (a_sig - 1.0) * ka_ref[...])
    a_s[...] = -kk
    b_s[...] = kk * a_sig
    lw_s[...] = -jnp.exp(-jax.nn.softplus(-w_arg) - 0.5)
    if final:
        bonus_s[...] = head_sum(r * k * rk_ref[...]) * v
        gd = zs[:, o3 + W_LORA + A_LORA:]
        gate_s[...] = jnp.dot(jax.nn.sigmoid(gd).astype(BF16), gup_ref[...], preferred_element_type=F32)

    span = grp * CHUNK
    li = lax.broadcasted_iota(jnp.int32, (2 * CHUNK, 2 * CHUNK), 0)
    lj = lax.broadcasted_iota(jnp.int32, (2 * CHUNK, 2 * CHUNK), 1)
    lt_, jt_ = li & (CHUNK - 1), lj & (CHUNK - 1)
    if reverse:
        strict, incl = lt_ < jt_, lt_ <= jt_
    else:
        strict, incl = lt_ > jt_, lt_ >= jt_
    pmask = strict | (incl & (li >= CHUNK))
    si = lax.broadcasted_iota(jnp.int32, (span, span), 0)
    sj = lax.broadcasted_iota(jnp.int32, (span, span), 1)
    same = (si // CHUNK) == (sj // CHUNK)
    tri = (same & ((si <= sj) if reverse else (si >= sj))).astype(BF16)
    oi = lax.broadcasted_iota(jnp.int32, (span, grp * 2 * CHUNK), 0)
    oj = lax.broadcasted_iota(jnp.int32, (span, grp * 2 * CHUNK), 1)
    ones_blk = ((oi // CHUNK) == (oj // (2 * CHUNK))).astype(BF16)
    ci = lax.broadcasted_iota(jnp.int32, (CHUNK, CHUNK), 0)
    cj = lax.broadcasted_iota(jnp.int32, (CHUNK, CHUNK), 1)
    eye = (ci == cj).astype(F32)
    zeros_h = jnp.zeros((grp * RWKV_HEADS, CHUNK, HEAD_DIM), BF16)
    nc = lt // span

    def heads(x):
        return jnp.stack([x[g * CHUNK:(g + 1) * CHUNK, h * HEAD_DIM:(h + 1) * HEAD_DIM]
                          for g in range(grp) for h in range(RWKV_HEADS)], axis=0)

    def chunk(j, carry):
        c = (nc - 1 - j) if reverse else j
        base = pl.multiple_of(c * span, span)
        rows = pl.ds(base, span)
        lw_c = lw_s[rows, :]
        parts = _split3(lw_c)
        cum = sum(jnp.dot(tri, q, preferred_element_type=F32) for q in parts)
        tot = sum(lax.dot_general(q, ones_blk, (((0,), (0,)), ((), ())), preferred_element_type=F32) for q in parts)
        g_all = jnp.exp(tot).reshape(RWKV_HEADS, HEAD_DIM, grp * 2 * CHUNK)
        e_pos = jnp.exp(cum)
        e_neg = jnp.exp(-cum)
        e_prev = jnp.exp(cum - lw_c)
        ar = jnp.concatenate([heads((a_s[rows, :] * e_prev).astype(BF16)),
                              heads((r_s[rows, :] * e_pos).astype(BF16))], axis=1)
        bk = jnp.concatenate([heads((b_s[rows, :] * e_neg).astype(BF16)),
                              heads((kd_s[rows, :] * e_neg).astype(BF16))], axis=1)
        vh = heads(v_s[rows, :].astype(BF16))
        pw = jnp.einsum('hlc,hjc->hlj', ar, bk, preferred_element_type=F32)
        pw = jnp.where(pmask, pw, 0.0)
        pw_b = pw.astype(BF16)
        m_ab = pw[:, :CHUNK, :CHUNK]
        tinv = eye + m_ab
        mp = m_ab
        for _ in range(5):
            mp = _bmm(mp, mp)
            tinv = tinv + _bmm(tinv, mp)
        tinv_b = tinv.astype(BF16)
        mv = _bmm(pw_b[:, :CHUNK], jnp.concatenate([zeros_h, vh], axis=1))

        st = st_ref[...]
        for q in (range(grp - 1, -1, -1) if reverse else range(grp)):
            hs = slice(q * RWKV_HEADS, (q + 1) * RWKV_HEADS)
            ar_st = _bmm(ar[hs], st)
            u = _bmm(tinv_b[hs], ar_st[:, :CHUNK] + mv[hs])
            uv = jnp.concatenate([u.astype(BF16), vh[hs]], axis=1)
            y = ar_st[:, CHUNK:] + _bmm(pw_b[hs][:, CHUNK:], uv)
            g_col = g_all[:, :, q * 2 * CHUNK:q * 2 * CHUNK + HEAD_DIM]
            st = g_col * (st + jnp.einsum('hjc,hjv->hcv', bk[hs], uv, preferred_element_type=F32))
            for h in range(RWKV_HEADS):
                y_scr[pl.ds(base + q * CHUNK, CHUNK), h * HEAD_DIM:(h + 1) * HEAD_DIM] = y[h]
        st_ref[...] = st
        return carry

    lax.fori_loop(0, nc, chunk, 0)

    if not final:
        out_ref[0] = y_scr[...]
    else:
        yy = yf_ref[0] + y_scr[...]
        mu_y = head_sum(yy) * (1.0 / HEAD_DIM)
        yc = yy - mu_y
        var_y = head_sum(yc * yc) * (1.0 / HEAD_DIM)
        yn = yc * lax.rsqrt(var_y + GN_EPS) * gng_ref[...] + gnb_ref[...]
        out_ref[0] = ((yn + bonus_s[...]) * gate_s[...]).astype(BF16)


def _rwkv_dir(zr, y_fwd, p, d, lt):
    b, t, _ = zr.shape
    nt = t // lt
    reverse = d == 1
    final = y_fwd is not None
    tmap = (lambda j: nt - 1 - j) if reverse else (lambda j: j)
    r8 = lt // 8
    vec = _const_spec((1, D_RWKV))
    in_specs = [pl.BlockSpec((1, lt, D_RWKV_IN), lambda i, j: (i, tmap(j), 0)),
                pl.BlockSpec((1, 8, D_RWKV_IN), lambda i, j: (i, jnp.maximum(tmap(j) * r8 - 1, 0), 0)),
                pl.BlockSpec((1, 8, D_RWKV_IN), lambda i, j: (i, jnp.minimum((tmap(j) + 1) * r8, t // 8 - 1), 0)),
                _const_spec((2, D_RWKV_IN)), vec, _const_spec((W_LORA + A_LORA, D_RWKV)), vec,
                _const_spec((W_LORA + A_LORA, D_RWKV))]
    args = [zr, zr, zr, p['mu'], p['w0'][d], p['rw_w_up'][d], p['a0'][d], p['rw_a_up'][d]]
    n_tok_scratch = 6
    if final:
        in_specs += [_const_spec((G_LORA, D_RWKV)), vec, vec, vec, vec, vec, _const_spec((D_RWKV, D_RWKV)),
                     pl.BlockSpec((1, lt, D_RWKV), lambda i, j: (i, tmap(j), 0))]
        args += [p['g_up'], p['k_k'], p['k_a'], p['r_k'], p['gn_g'], p['gn_b'], p['bd'], y_fwd]
        n_tok_scratch = 8
    else:
        in_specs += [vec, vec, _const_spec((D_RWKV, D_RWKV))]
        args += [p['k_k'], p['k_a'], p['bd']]
    return pl.pallas_call(
        functools.partial(_rwkv_kernel, reverse=reverse, lt=lt, nt=nt, final=final, grp=min(8, lt // CHUNK)),
        grid=(b, nt),
        in_specs=in_specs,
        out_specs=pl.BlockSpec((1, lt, D_RWKV), lambda i, j: (i, tmap(j), 0)),
        out_shape=jax.ShapeDtypeStruct((b, t, D_RWKV), BF16 if final else F32),
        scratch_shapes=[pltpu.VMEM((RWKV_HEADS, HEAD_DIM, HEAD_DIM), F32)]
                       + [pltpu.VMEM((lt, D_RWKV), F32)] * (1 + n_tok_scratch),
        compiler_params=_cparams("arbitrary", "arbitrary"),
        name="rwkv_bwd_final" if final else "rwkv_fwd",
    )(*args)


def _memkv_kernel(m_ref, g_ref, b_ref, w_ref, kv_ref):
    m = _ln(m_ref[...], g_ref[...], b_ref[...])
    kv_ref[...] = jnp.dot(m.astype(BF16), w_ref[...], preferred_element_type=F32).astype(BF16)


def _memkv(mem2d, g, b, w_ckv):
    n = mem2d.shape[0]
    return pl.pallas_call(
        _memkv_kernel,
        grid=(n // N_MEM,),
        in_specs=[pl.BlockSpec((N_MEM, D_MODEL), lambda i: (i, 0)),
                  _const_spec((1, D_MODEL)), _const_spec((1, D_MODEL)),
                  _const_spec((D_MODEL, 2 * D_CROSS))],
        out_specs=pl.BlockSpec((N_MEM, 2 * D_CROSS), lambda i: (i, 0)),
        out_shape=jax.ShapeDtypeStruct((n, 2 * D_CROSS), BF16),
        compiler_params=_cparams("arbitrary"),
        name="mem_kv",
    )(mem2d, g, b, w_ckv)


def _mid_kernel(x_ref, att_ref, rw_ref, kv_ref, wo_ref, l1g_ref, l1b_ref, wq_ref, wco_ref, l2g_ref, l2b_ref,
                wr_ref, br_ref, before_ref, x2_ref, route_ref, routet_ref, meta_ref, *, sub, win):
    mix = (jnp.dot(att_ref[...], wo_ref[:D_ATT, :], preferred_element_type=F32)
           + jnp.dot(rw_ref[...], wo_ref[D_ATT:, :], preferred_element_type=F32))
    x1 = _ln(DEEPNORM_ALPHA * x_ref[...] + mix, l1g_ref[...], l1b_ref[...])

    q = jnp.dot(x1.astype(BF16), wq_ref[...], preferred_element_type=F32)
    q = (q * (CROSS_HEAD_DIM ** -0.5)).astype(BF16)
    kv = kv_ref[...]
    cr = None
    for h in range(CROSS_HEADS):
        hs = slice(h * CROSS_HEAD_DIM, (h + 1) * CROSS_HEAD_DIM)
        kh = kv[:, h * CROSS_HEAD_DIM:(h + 1) * CROSS_HEAD_DIM]
        vh = kv[:, D_CROSS + h * CROSS_HEAD_DIM:D_CROSS + (h + 1) * CROSS_HEAD_DIM]
        s = lax.dot_general(q[:, hs], kh, (((1,), (1,)), ((), ())), preferred_element_type=F32)
        m = jnp.max(s, -1, keepdims=True)
        p = jnp.exp(s - m)
        den = jnp.sum(p, -1, keepdims=True)
        o = jnp.dot(p.astype(BF16), vh, preferred_element_type=F32) / den
        part = jnp.dot(o.astype(BF16), wco_ref[hs, :], preferred_element_type=F32)
        cr = part if cr is None else cr + part
    x2 = _ln(DEEPNORM_ALPHA * x1 + cr, l2g_ref[...], l2b_ref[...])
    x2_ref[...] = x2

    x2_hi = x2.astype(BF16)
    x2_lo = (x2 - x2_hi.astype(F32)).astype(BF16)
    both = jnp.dot(x2_hi, wr_ref[...], preferred_element_type=F32)
    logits = (both[:, :ROUTE_LANES] + both[:, ROUTE_LANES:]
              + jnp.dot(x2_lo, wr_ref[:, :ROUTE_LANES], preferred_element_type=F32)) + br_ref[...]
    lane = lax.broadcasted_iota(jnp.int32, logits.shape, 1)
    is_g = (lane >= N_EXPERTS) & (lane < N_EXPERTS + N_EXPERT_GROUPS)
    lg = jnp.where(is_g, logits, NEG_INF)
    eg = jnp.where(is_g, jnp.exp(lg - jnp.max(lg, -1, keepdims=True)), 0.0)
    pg = eg / jnp.sum(eg, -1, keepdims=True)
    pg_top = jnp.max(pg, -1, keepdims=True)
    gi = jnp.min(jnp.where(is_g & (pg == pg_top), lane, 4 * ROUTE_LANES), -1, keepdims=True) - N_EXPERTS
    in_grp = (lane >= gi * EXPERTS_PER_GROUP) & (lane < (gi + 1) * EXPERTS_PER_GROUP)
    le = jnp.where(in_grp, logits, NEG_INF)
    ee = jnp.where(in_grp, jnp.exp(le - jnp.max(le, -1, keepdims=True)), 0.0)
    pe = ee / jnp.sum(ee, -1, keepdims=True)
    p1 = jnp.max(jnp.where(in_grp, pe, -1.0), -1, keepdims=True)
    i1 = jnp.min(jnp.where(in_grp & (pe == p1), lane, 4 * ROUTE_LANES), -1, keepdims=True)
    rest = in_grp & (lane != i1)
    p2 = jnp.max(jnp.where(rest, pe, -1.0), -1, keepdims=True)
    i2 = jnp.min(jnp.where(rest & (pe == p2), lane, 4 * ROUTE_LANES), -1, keepdims=True)
    tot = p1 + p2
    w1 = p1 / tot * pg_top
    w2 = p2 / tot * pg_top

    tm = logits.shape[0]
    n_sub = tm // sub
    e_lane = lane & (N_EXPERTS - 1)
    sel = ((e_lane == i1) | (e_lane == i2)) & (lane < 2 * N_EXPERTS)
    sel_b = sel.astype(BF16)
    rank = jnp.dot(before_ref[...], sel_b, preferred_element_type=F32)
    si = lax.broadcasted_iota(jnp.int32, (8, tm), 0)
    sj = lax.broadcasted_iota(jnp.int32, (8, tm), 1)
    cnt = jnp.dot(((sj // sub) == si).astype(BF16), sel_b, preferred_element_type=F32)
    padded = jnp.floor((cnt + (SEG_ALIGN - 1)) * (1.0 / SEG_ALIGN)) * SEG_ALIGN
    ui = lax.broadcasted_iota(jnp.int32, (ROUTE_LANES, ROUTE_LANES), 0)
    uj = lax.broadcasted_iota(jnp.int32, (ROUTE_LANES, ROUTE_LANES), 1)
    excl = ((ui < uj) & (uj < N_EXPERTS)).astype(F32)
    lane8 = lax.broadcasted_iota(jnp.int32, (8, ROUTE_LANES), 1)
    off = jnp.dot(jnp.where(lane8 < N_EXPERTS, padded, 0.0), excl, precision=HI, preferred_element_type=F32)
    nblk = jnp.floor((cnt + (win - 1)) * (1.0 / win))
    meta_ref[...] = jnp.where(lane8 < N_EXPERTS, off, nblk).astype(jnp.int32)
    sub_id = lax.broadcasted_iota(jnp.int32, (tm, 1), 0) // sub
    off_tok = jnp.zeros_like(rank)
    for s in range(n_sub):
        off_tok = jnp.where(sub_id == s, off[s:s + 1, :], off_tok)
    dest = off_tok + rank
    d1 = jnp.sum(jnp.where(lane == i1, dest, 0.0), -1, keepdims=True)
    d2 = jnp.sum(jnp.where(lane == i2, dest, 0.0), -1, keepdims=True)
    route = (jnp.where(lane == 0, w1, 0.0) + jnp.where(lane == 1, w2, 0.0)
             + jnp.where(lane == 2, d1, 0.0) + jnp.where(lane == 3, d2, 0.0))
    route_ref[...] = route
    pick = (lane8 == lax.broadcasted_iota(jnp.int32, (8, ROUTE_LANES), 0)).astype(F32)
    routet_ref[...] = lax.dot_general(pick, route, (((1,), (1,)), ((), ())), precision=HI,
                                      preferred_element_type=F32)


def _mid(x2d, att2d, rw2d, kv, p, tm, t, sub, win):
    n = x2d.shape[0]
    per_b = t // tm
    nt = n // tm
    vec = _const_spec((1, D_MODEL))
    tok = jnp.arange(tm)
    before = ((tok[None, :] < tok[:, None]) & ((tok[None, :] // sub) == (tok[:, None] // sub))).astype(BF16)
    return pl.pallas_call(
        functools.partial(_mid_kernel, sub=sub, win=win),
        grid=(nt,),
        in_specs=[pl.BlockSpec((tm, D_MODEL), lambda i: (i, 0)),
                  pl.BlockSpec((tm, D_ATT), lambda i: (i, 0)),
                  pl.BlockSpec((tm, D_RWKV), lambda i: (i, 0)),
                  pl.BlockSpec((N_MEM, 2 * D_CROSS), lambda i: (i // per_b, 0)),
                  _const_spec((D_MODEL, D_MODEL)), vec, vec,
                  _const_spec((D_MODEL, D_CROSS)), _const_spec((D_CROSS, D_MODEL)), vec, vec,
                  _const_spec((D_MODEL, 2 * ROUTE_LANES)), _const_spec((1, ROUTE_LANES)), _const_spec((tm, tm))],
        out_specs=[pl.BlockSpec((tm, D_MODEL), lambda i: (i, 0)),
                   pl.BlockSpec((tm, ROUTE_LANES), lambda i: (i, 0)),
                   pl.BlockSpec((8, tm), lambda i: (i, 0)),
                   pl.BlockSpec((8, ROUTE_LANES), lambda i: (i, 0))],
        out_shape=[jax.ShapeDtypeStruct((n, D_MODEL), F32),
                   jax.ShapeDtypeStruct((n, ROUTE_LANES), F32),
                   jax.ShapeDtypeStruct((nt * 8, tm), F32),
                   jax.ShapeDtypeStruct((nt * 8, ROUTE_LANES), jnp.int32)],
        compiler_params=_cparams("arbitrary"),
        name="mid",
    )(x2d, att2d, rw2d, kv, p['w_out'], p['ln1_g'], p['ln1_b'], p['w_cq'], p['w_co'], p['ln2_g'], p['ln2_b'],
      p['w_route'], p['b_route'], before)


def _moe_kernel(meta_ref, x_ref, route_ref, routet_ref, wg_ref, wu_ref, wd_ref, l3g_ref, l3b_ref, o_ref,
                xs_ref, ys_ref, *, sub, win, rows):
    i = pl.program_id(0)
    e = pl.program_id(1)
    n_sub = xs_ref.shape[0]
    dummy = rows - win

    @pl.when(e == 0)
    def _():
        ys_ref[...] = jnp.zeros_like(ys_ref)
        r_id = lax.broadcasted_iota(jnp.int32, (rows, sub), 0).astype(F32)
        for s in range(n_sub):
            d1 = routet_ref[2:3, s * sub:(s + 1) * sub]
            d2 = routet_ref[3:4, s * sub:(s + 1) * sub]
            perm = ((r_id == d1) | (r_id == d2)).astype(BF16)
            xb = x_ref[s * sub:(s + 1) * sub, :].astype(BF16)
            xs_ref[s] = jnp.dot(perm, xb, preferred_element_type=F32).astype(BF16)

    base = i * (n_sub * 32)
    offs = [meta_ref[base + s * 32 + e] for s in range(n_sub)]
    nbs = [meta_ref[base + s * 32 + N_EXPERTS + e] for s in range(n_sub)]
    nb_max = functools.reduce(jnp.maximum, nbs)

    def block(j, carry):
        starts = [pl.multiple_of(jnp.where(j < nbs[s], offs[s] + j * win, dummy), SEG_ALIGN) for s in range(n_sub)]
        xw = jnp.concatenate([xs_ref[s, pl.ds(starts[s], win), :] for s in range(n_sub)], axis=0)
        gate = jnp.dot(xw, wg_ref[0], preferred_element_type=F32)
        up = jnp.dot(xw, wu_ref[0], preferred_element_type=F32)
        hdn = (gate * jax.nn.sigmoid(gate)) * up
        ye = jnp.dot(hdn.astype(BF16), wd_ref[0], preferred_element_type=F32).astype(BF16)
        for s in range(n_sub):
            ys_ref[s, pl.ds(starts[s], win), :] = ye[s * win:(s + 1) * win]
        return carry

    lax.fori_loop(0, nb_max, block, 0)

    @pl.when(e == N_EXPERTS - 1)
    def _():
        c_id = lax.broadcasted_iota(jnp.int32, (sub, rows), 1).astype(F32)
        for s in range(n_sub):
            rt = route_ref[s * sub:(s + 1) * sub, :]
            w1, w2, d1, d2 = rt[:, 0:1], rt[:, 1:2], rt[:, 2:3], rt[:, 3:4]
            comb = jnp.where(c_id == d1, w1, 0.0) + jnp.where(c_id == d2, w2, 0.0)
            hi = comb.astype(BF16)
            lo = (comb - hi.astype(F32)).astype(BF16)
            ysv = ys_ref[s]
            ff = (jnp.dot(hi, ysv, preferred_element_type=F32) + jnp.dot(lo, ysv, preferred_element_type=F32))
            xr = x_ref[s * sub:(s + 1) * sub, :]
            o_ref[s * sub:(s + 1) * sub, :] = _ln(DEEPNORM_ALPHA * xr + ff, l3g_ref[...], l3b_ref[...])


def _moe(x2, route, routet, meta, p, tm, sub, win):
    n = x2.shape[0]
    nt = n // tm
    n_sub = tm // sub
    rows = 2 * sub + N_EXPERTS * SEG_ALIGN + 2 * win
    meta_flat = meta.reshape(nt, 8, ROUTE_LANES)[:, :n_sub, :32].reshape(-1)
    vec = pl.BlockSpec((1, D_MODEL), lambda i, e, m: (0, 0))
    grid_spec = pltpu.PrefetchScalarGridSpec(
        num_scalar_prefetch=1,
        grid=(nt, N_EXPERTS),
        in_specs=[pl.BlockSpec((tm, D_MODEL), lambda i, e, m: (i, 0)),
                  pl.BlockSpec((tm, ROUTE_LANES), lambda i, e, m: (i, 0)),
                  pl.BlockSpec((8, tm), lambda i, e, m: (i, 0)),
                  pl.BlockSpec((1, D_MODEL, D_EXPERT), lambda i, e, m: (e, 0, 0)),
                  pl.BlockSpec((1, D_MODEL, D_EXPERT), lambda i, e, m: (e, 0, 0)),
                  pl.BlockSpec((1, D_EXPERT, D_MODEL), lambda i, e, m: (e, 0, 0)),
                  vec, vec],
        out_specs=pl.BlockSpec((tm, D_MODEL), lambda i, e, m: (i, 0)),
        scratch_shapes=[pltpu.VMEM((n_sub, rows, D_MODEL), BF16),
                        pltpu.VMEM((n_sub, rows, D_MODEL), BF16)])
    return pl.pallas_call(
        functools.partial(_moe_kernel, sub=sub, win=win, rows=rows),
        grid_spec=grid_spec,
        out_shape=jax.ShapeDtypeStruct((n, D_MODEL), F32),
        compiler_params=_cparams("arbitrary", "arbitrary"),
        name="moe",
    )(meta_flat, x2, route, routet, p['w_gate'], p['w_up'], p['w_down'], p['ln3_g'], p['ln3_b'])


def _prep_params(w_in, tshift_mu, attn_sink, rwkv_w0, rwkv_w_up, rwkv_a0, rwkv_a_up, rwkv_g_up, rwkv_k_k,
                 rwkv_k_a, rwkv_r_k, rwkv_gn_g, rwkv_gn_b, w_out, ln1_g, ln1_b, mem_ln_g, mem_ln_b, w_cq, w_ckv,
                 w_co, ln2_g, ln2_b, w_route_group, b_route_group, w_route_expert, b_route_expert,
                 w_exp_gate, w_exp_up, w_exp_down, ln3_g, ln3_b):
    row = lambda a: a.reshape(1, -1).astype(F32)
    zeros_lora = jnp.zeros((2, W_LORA, D_RWKV), F32)
    hid = jnp.arange(D_RWKV) // HEAD_DIM
    pad = ROUTE_LANES - N_EXPERTS - N_EXPERT_GROUPS
    return {
        'w_in_att': w_in[:, :D_QKV].astype(BF16),
        'w_in_rw': w_in[:, D_QKV:].astype(BF16),
        'sink': attn_sink.astype(F32),
        'mu': tshift_mu.astype(F32),
        'w0': rwkv_w0.reshape(2, 1, D_RWKV).astype(F32),
        'rw_w_up': jnp.concatenate([rwkv_w_up, zeros_lora], axis=1).astype(BF16),
        'a0': rwkv_a0.reshape(2, 1, D_RWKV).astype(F32),
        'rw_a_up': jnp.concatenate([zeros_lora, rwkv_a_up], axis=1).astype(BF16),
        'g_up': rwkv_g_up.astype(BF16),
        'k_k': row(rwkv_k_k), 'k_a': row(rwkv_k_a), 'r_k': row(rwkv_r_k),
        'gn_g': row(rwkv_gn_g), 'gn_b': row(rwkv_gn_b),
        'bd': (hid[:, None] == hid[None, :]).astype(BF16),
        'w_out': w_out.astype(BF16),
        'ln1_g': row(ln1_g), 'ln1_b': row(ln1_b),
        'mem_ln_g': row(mem_ln_g), 'mem_ln_b': row(mem_ln_b),
        'w_cq': w_cq.astype(BF16), 'w_ckv': w_ckv.astype(BF16), 'w_co': w_co.astype(BF16),
        'ln2_g': row(ln2_g), 'ln2_b': row(ln2_b),
        'w_route': _hi_lo(jnp.pad(jnp.concatenate([w_route_expert, w_route_group], axis=1), ((0, 0), (0, pad)))),
        'b_route': jnp.pad(jnp.concatenate([b_route_expert, b_route_group]), (0, pad)).reshape(1, -1).astype(F32),
        'w_gate': w_exp_gate.astype(BF16), 'w_up': w_exp_up.astype(BF16), 'w_down': w_exp_down.astype(BF16),
        'ln3_g': row(ln3_g), 'ln3_b': row(ln3_b),
    }


def _hi_lo(w):
    w = w.astype(F32)
    hi = w.astype(BF16)
    return jnp.concatenate([hi, (w - hi.astype(F32)).astype(BF16)], axis=-1)


def _tile(n, pref):
    t = pref
    while n % t:
        t //= 2
    return t


def _layer(x, mem, p):
    b, t, _ = x.shape
    n = b * t
    x2d = x.reshape(n, D_MODEL)
    tm = _tile(t, 512)
    qkv, zr = _inproj(x2d, p['w_in_att'], p['w_in_rw'], tm)
    att = _attention(qkv.reshape(b, t, D_QKV), p['sink'])
    zr3 = zr.reshape(b, t, D_RWKV_IN)
    lt = _tile(t, 512)
    y_fwd = _rwkv_dir(zr3, None, p, 0, lt)
    rw = _rwkv_dir(zr3, y_fwd, p, 1, lt)
    kv = _memkv(mem.reshape(b * N_MEM, D_MODEL), p['mem_ln_g'], p['mem_ln_b'], p['w_ckv'])
    tme = _tile(t, 1024)
    sub = _tile(tme, 256)
    win = sub // 4
    x2, route, routet, meta = _mid(x2d, att.reshape(n, D_ATT), rw.reshape(n, D_RWKV), kv, p, tme, t, sub, win)
    y = _moe(x2, route, routet, meta, p, tme, sub, win)
    return y.reshape(b, t, D_MODEL)


def kernel(x_prompt, x_sample, mem_prompt, mem_sample, w_in, tshift_mu, attn_sink, rwkv_w0, rwkv_w_up, rwkv_a0, rwkv_a_up, rwkv_g_up, rwkv_k_k, rwkv_k_a, rwkv_r_k, rwkv_gn_g, rwkv_gn_b, w_out, ln1_g, ln1_b, mem_ln_g, mem_ln_b, w_cq, w_ckv, w_co, ln2_g, ln2_b, w_route_group, b_route_group, w_route_expert, b_route_expert, w_exp_gate, w_exp_up, w_exp_down, ln3_g, ln3_b):
    weights = (w_in, tshift_mu, attn_sink, rwkv_w0, rwkv_w_up, rwkv_a0, rwkv_a_up, rwkv_g_up, rwkv_k_k, rwkv_k_a,
               rwkv_r_k, rwkv_gn_g, rwkv_gn_b, w_out, ln1_g, ln1_b, mem_ln_g, mem_ln_b, w_cq, w_ckv, w_co,
               ln2_g, ln2_b, w_route_group, b_route_group, w_route_expert, b_route_expert,
               w_exp_gate, w_exp_up, w_exp_down, ln3_g, ln3_b)
    p = _prep_params(*[w[0] for w in weights])
    return (_layer(x_prompt, mem_prompt, p), _layer(x_sample, mem_sample, p))
```

```python
import functools

import jax
import jax.numpy as jnp
from jax import lax
from jax.experimental import pallas as pl
from jax.experimental.pallas import tpu as pltpu

F32 = jnp.float32
BF16 = jnp.bfloat16
HI = lax.Precision.HIGHEST

D_MODEL = 1024
HEAD_DIM = 64
ATT_Q_HEADS = 8
ATT_KV_HEADS = 2
ATT_GROUP = ATT_Q_HEADS // ATT_KV_HEADS
WINDOW = 128
BLOCK = 128
RWKV_HEADS = 8
D_ATT = ATT_Q_HEADS * HEAD_DIM
D_KV = ATT_KV_HEADS * HEAD_DIM
D_QKV = D_ATT + 2 * D_KV
D_RWKV = RWKV_HEADS * HEAD_DIM
W_LORA = 64
A_LORA = 64
G_LORA = 128
D_RWKV_IN = 3 * D_RWKV + W_LORA + A_LORA + G_LORA
N_MEM = 256
CROSS_HEADS = 4
CROSS_HEAD_DIM = 128
D_CROSS = CROSS_HEADS * CROSS_HEAD_DIM
N_EXPERT_GROUPS = 4
EXPERTS_PER_GROUP = 4
N_EXPERTS = N_EXPERT_GROUPS * EXPERTS_PER_GROUP
D_EXPERT = 512
LN_EPS = 1e-5
GN_EPS = 64e-5
DEEPNORM_ALPHA = 2.0 ** 0.25
NEG_INF = -1e30
LANES = 128
ROUTE_LANES = LANES
CHUNK = 64
SEG_ALIGN = 16
EXPERTS_PER_STEP = 2
VMEM_LIMIT = 56 * 1024 * 1024


def _cparams(*sem):
    return pltpu.CompilerParams(dimension_semantics=sem, vmem_limit_bytes=VMEM_LIMIT)


def _ln(x, g, b):
    mu = jnp.mean(x, -1, keepdims=True)
    xc = x - mu
    var = jnp.mean(xc * xc, -1, keepdims=True)
    return xc * lax.rsqrt(var + LN_EPS) * g + b


def _const_spec(shape):
    nd = len(shape)
    return pl.BlockSpec(shape, lambda *_: (0,) * nd)


def _inproj_kernel(x_ref, wa_ref, wr_ref, qkv_ref, zr_ref):
    xb = x_ref[...].astype(BF16)
    qkv_ref[...] = jnp.dot(xb, wa_ref[...], preferred_element_type=F32).astype(BF16)
    zr_ref[...] = jnp.dot(xb, wr_ref[...], preferred_element_type=F32)


def _inproj(x2d, w_att, w_rw, tm):
    n = x2d.shape[0]
    return pl.pallas_call(
        _inproj_kernel,
        grid=(n // tm,),
        in_specs=[pl.BlockSpec((tm, D_MODEL), lambda i: (i, 0)),
                  _const_spec((D_MODEL, D_QKV)),
                  _const_spec((D_MODEL, D_RWKV_IN))],
        out_specs=[pl.BlockSpec((tm, D_QKV), lambda i: (i, 0)),
                   pl.BlockSpec((tm, D_RWKV_IN), lambda i: (i, 0))],
        out_shape=[jax.ShapeDtypeStruct((n, D_QKV), BF16),
                   jax.ShapeDtypeStruct((n, D_RWKV_IN), F32)],
        compiler_params=_cparams("arbitrary"),
        name="inproj",
    )(x2d, w_att, w_rw)


def _attn_kernel(sink_ref, cur_ref, prv_ref, nxt_ref, o_ref, *, ns, qb):
    n = pl.program_id(1)
    blocks = [prv_ref[0]] + [cur_ref[0, a * BLOCK:(a + 1) * BLOCK, :] for a in range(qb)] + [nxt_ref[0]]
    rows = ATT_GROUP * BLOCK
    ri = lax.broadcasted_iota(jnp.int32, (rows, 3 * BLOCK), 0)
    ki = lax.broadcasted_iota(jnp.int32, (rows, 3 * BLOCK), 1)
    adist = jnp.abs((ri & (BLOCK - 1)) - ki + BLOCK)
    band = adist <= WINDOW
    adist_f = adist.astype(F32)
    grp_id = lax.broadcasted_iota(jnp.int32, (rows, 1), 0) // BLOCK
    biases, sinks = [], []
    for h in range(ATT_KV_HEADS):
        slope = jnp.zeros((rows, 1), F32)
        sink = jnp.zeros((rows, 1), F32)
        for g in range(ATT_GROUP):
            hq = h * ATT_GROUP + g
            slope = jnp.where(grp_id == g, 2.0 ** (-8.0 / ATT_Q_HEADS * (hq + 1)), slope)
            sink = jnp.where(grp_id == g, sink_ref[hq], sink)
        biases.append(slope * adist_f)
        sinks.append(sink)
    for a in range(qb):
        valid = band
        if a == 0:
            valid = valid & ((ki >= BLOCK) | (n > 0))
        if a == qb - 1:
            valid = valid & ((ki < 2 * BLOCK) | (n < ns - 1))
        cur = blocks[a + 1]
        for h in range(ATT_KV_HEADS):
            k0 = D_ATT + h * HEAD_DIM
            v0 = D_ATT + D_KV + h * HEAD_DIM
            kcat = jnp.concatenate([blk[:, k0:k0 + HEAD_DIM] for blk in blocks[a:a + 3]], axis=0)
            vcat = jnp.concatenate([blk[:, v0:v0 + HEAD_DIM] for blk in blocks[a:a + 3]], axis=0)
            q0 = h * ATT_GROUP * HEAD_DIM
            q = jnp.concatenate([cur[:, q0 + g * HEAD_DIM:q0 + (g + 1) * HEAD_DIM] for g in range(ATT_GROUP)], axis=0)
            q = q * (HEAD_DIM ** -0.5)
            s = lax.dot_general(q, kcat, (((1,), (1,)), ((), ())), preferred_element_type=F32)
            s = jnp.where(valid, s - biases[h], NEG_INF)
            sk = sinks[h]
            m = jnp.maximum(jnp.max(s, -1, keepdims=True), sk)
            p = jnp.exp(s - m)
            den = jnp.sum(p, -1, keepdims=True) + jnp.exp(sk - m)
            o = (jnp.dot(p.astype(BF16), vcat, preferred_element_type=F32) / den).astype(BF16)
            for g in range(ATT_GROUP):
                o_ref[0, a * BLOCK:(a + 1) * BLOCK, q0 + g * HEAD_DIM:q0 + (g + 1) * HEAD_DIM] = (
                    o[g * BLOCK:(g + 1) * BLOCK])


def _attention(qkv, sink):
    b, t, _ = qkv.shape
    nb = t // BLOCK
    qb = _tile(nb, 4)
    ns = nb // qb
    return pl.pallas_call(
        functools.partial(_attn_kernel, ns=ns, qb=qb),
        grid=(b, ns),
        in_specs=[pl.BlockSpec(memory_space=pltpu.SMEM),
                  pl.BlockSpec((1, qb * BLOCK, D_QKV), lambda i, j: (i, j, 0)),
                  pl.BlockSpec((1, BLOCK, D_QKV), lambda i, j: (i, jnp.maximum(j * qb - 1, 0), 0)),
                  pl.BlockSpec((1, BLOCK, D_QKV), lambda i, j: (i, jnp.minimum((j + 1) * qb, nb - 1), 0))],
        out_specs=pl.BlockSpec((1, qb * BLOCK, D_ATT), lambda i, j: (i, j, 0)),
        out_shape=jax.ShapeDtypeStruct((b, t, D_ATT), BF16),
        compiler_params=_cparams("arbitrary", "arbitrary"),
        name="win_attn",
    )(sink, qkv, qkv, qkv)


def _heads(x):
    return jnp.stack([x[:, h * HEAD_DIM:(h + 1) * HEAD_DIM] for h in range(RWKV_HEADS)], axis=0)


def _bmm(a, b):
    return jnp.einsum('hlj,hjm->hlm', a.astype(BF16), b.astype(BF16), preferred_element_type=F32)


def _split3(x):
    hi = x.astype(BF16)
    r1 = x - hi.astype(F32)
    mid = r1.astype(BF16)
    lo = (r1 - mid.astype(F32)).astype(BF16)
    return hi, mid, lo


def _rwkv_kernel(*refs, reverse, lt, nt, final, grp):
    if final:
        (zc_ref, zp_ref, zn_ref, mu_ref, w0_ref, wup_ref, a0_ref, aup_ref, gup_ref, kk_ref, ka_ref, rk_ref,
         gng_ref, gnb_ref, bd_ref, yf_ref, out_ref,
         st_ref, y_scr, r_s, kd_s, v_s, a_s, b_s, lw_s, bonus_s, gate_s) = refs
    else:
        (zc_ref, zp_ref, zn_ref, mu_ref, w0_ref, wup_ref, a0_ref, aup_ref, kk_ref, ka_ref, bd_ref,
         out_ref, st_ref, y_scr, r_s, kd_s, v_s, a_s, b_s, lw_s) = refs
    i = pl.program_id(1)
    tt = (nt - 1 - i) if reverse else i

    @pl.when(i == 0)
    def _():
        st_ref[...] = jnp.zeros_like(st_ref)

    bd = bd_ref[...]

    def head_sum(x):
        hi = x.astype(BF16)
        lo = (x - hi.astype(F32)).astype(BF16)
        tiles = []
        for t in range(D_RWKV // LANES):
            sl = slice(t * LANES, (t + 1) * LANES)
            tiles.append(jnp.dot(hi[:, sl], bd, preferred_element_type=F32)
                         + jnp.dot(lo[:, sl], bd, preferred_element_type=F32))
        return jnp.concatenate(tiles, axis=-1)

    z = zc_ref[0]
    row = lax.broadcasted_iota(jnp.int32, (lt, 1), 0)
    zprev_edge = jnp.where(tt > 0, zp_ref[0, 7:8, :], 0.0)
    znext_edge = jnp.where(tt < nt - 1, zn_ref[0, 0:1, :], 0.0)
    prev = jnp.where(row == 0, zprev_edge, pltpu.roll(z, 1, 0))
    nxt = jnp.where(row == lt - 1, znext_edge, pltpu.roll(z, lt - 1, 0))
    zs = z + mu_ref[0:1, :] * (prev - z) + mu_ref[1:2, :] * (nxt - z)

    o1, o2, o3 = D_RWKV, 2 * D_RWKV, 3 * D_RWKV
    r = zs[:, :o1]
    k = zs[:, o1:o2]
    v = zs[:, o2:o3]
    wad = zs[:, o3:o3 + W_LORA + A_LORA]
    kk = k * kk_ref[...]
    kk = kk * lax.rsqrt(head_sum(kk * kk) + 1e-12)
    w_arg = w0_ref[...] + jnp.dot(jnp.tanh(wad).astype(BF16), wup_ref[...], preferred_element_type=F32)
    a_sig = jax.nn.sigmoid(a0_ref[...] + jnp.dot(wad.astype(BF16), aup_ref[...], preferred_element_type=F32))
    r_s[...] = r
    v_s[...] = v
    kd_s[...] = k * (1.0 + (a_sig - 1.0) * ka_ref[...])
    a_s[...] = -kk
    b_s[...] = kk * a_sig
    lw_s[...] = -jnp.exp(-jax.nn.softplus(-w_arg) - 0.5)
    if final:
        bonus_s[...] = head_sum(r * k * rk_ref[...]) * v
        gd = zs[:, o3 + W_LORA + A_LORA:]
        gate_s[...] = jnp.dot(jax.nn.sigmoid(gd).astype(BF16), gup_ref[...], preferred_element_type=F32)

    span = grp * CHUNK
    li = lax.broadcasted_iota(jnp.int32, (2 * CHUNK, 2 * CHUNK), 0)
    lj = lax.broadcasted_iota(jnp.int32, (2 * CHUNK, 2 * CHUNK), 1)
    lt_, jt_ = li & (CHUNK - 1), lj & (CHUNK - 1)
    if reverse:
        strict, incl = lt_ < jt_, lt_ <= jt_
    else:
        strict, incl = lt_ > jt_, lt_ >= jt_
    pmask = strict | (incl & (li >= CHUNK))
    si = lax.broadcasted_iota(jnp.int32, (span, span), 0)
    sj = lax.broadcasted_iota(jnp.int32, (span, span), 1)
    same = (si // CHUNK) == (sj // CHUNK)
    tri = (same & ((si <= sj) if reverse else (si >= sj))).astype(BF16)
    oi = lax.broadcasted_iota(jnp.int32, (16, grp * 2 * CHUNK), 0)
    oj = lax.broadcasted_iota(jnp.int32, (16, grp * 2 * CHUNK), 1)
    spread = (oi == (oj // (2 * CHUNK))).astype(BF16)
    ci = lax.broadcasted_iota(jnp.int32, (CHUNK, CHUNK), 0)
    cj = lax.broadcasted_iota(jnp.int32, (CHUNK, CHUNK), 1)
    eye = (ci == cj).astype(F32)
    zeros_h = jnp.zeros((grp * RWKV_HEADS, CHUNK, HEAD_DIM), BF16)
    nc = lt // span

    def heads(x):
        return jnp.stack([x[g * CHUNK:(g + 1) * CHUNK, h * HEAD_DIM:(h + 1) * HEAD_DIM]
                          for g in range(grp) for h in range(RWKV_HEADS)], axis=0)

    def chunk(j, carry):
        c = (nc - 1 - j) if reverse else j
        base = pl.multiple_of(c * span, span)
        rows = pl.ds(base, span)
        lw_c = lw_s[rows, :]
        parts = _split3(lw_c)
        cum = sum(jnp.dot(tri, q, preferred_element_type=F32) for q in parts)
        g_rows = jnp.exp(cum.reshape(grp, CHUNK, D_RWKV)[:, (0 if reverse else CHUNK - 1), :])
        g_rows = jnp.concatenate([g_rows, jnp.zeros((spread.shape[0] - grp, D_RWKV), F32)], axis=0)
        g_all = sum(lax.dot_general(q, spread, (((0,), (0,)), ((), ())), preferred_element_type=F32)
                    for q in _split3(g_rows)).reshape(RWKV_HEADS, HEAD_DIM, grp * 2 * CHUNK)
        e_pos = jnp.exp(cum)
        e_neg = jnp.exp(-cum)
        e_prev = jnp.exp(cum - lw_c)
        ar = jnp.concatenate([heads((a_s[rows, :] * e_prev).astype(BF16)),
                              heads((r_s[rows, :] * e_pos).astype(BF16))], axis=1)
        bk = jnp.concatenate([heads((b_s[rows, :] * e_neg).astype(BF16)),
                              heads((kd_s[rows, :] * e_neg).astype(BF16))], axis=1)
        vh = heads(v_s[rows, :].astype(BF16))
        pw = jnp.einsum('hlc,hjc->hlj', ar, bk, preferred_element_type=F32)
        pw = jnp.where(pmask, pw, 0.0)
        pw_b = pw.astype(BF16)
        m_ab = pw[:, :CHUNK, :CHUNK]
        tinv = eye + m_ab
        mp = m_ab
        for _ in range(5):
            mp = _bmm(mp, mp)
            tinv = tinv + _bmm(tinv, mp)
        tinv_b = tinv.astype(BF16)
        mv = _bmm(pw_b[:, :CHUNK], jnp.concatenate([zeros_h, vh], axis=1))

        st = st_ref[...]
        for q in (range(grp - 1, -1, -1) if reverse else range(grp)):
            hs = slice(q * RWKV_HEADS, (q + 1) * RWKV_HEADS)
            ar_st = _bmm(ar[hs], st)
            u = _bmm(tinv_b[hs], ar_st[:, :CHUNK] + mv[hs])
            uv = jnp.concatenate([u.astype(BF16), vh[hs]], axis=1)
            y = ar_st[:, CHUNK:] + _bmm(pw_b[hs][:, CHUNK:], uv)
            g_col = g_all[:, :, q * 2 * CHUNK:q * 2 * CHUNK + HEAD_DIM]
            st = g_col * (st + jnp.einsum('hjc,hjv->hcv', bk[hs], uv, preferred_element_type=F32))
            for h in range(RWKV_HEADS):
                y_scr[pl.ds(base + q * CHUNK, CHUNK), h * HEAD_DIM:(h + 1) * HEAD_DIM] = y[h]
        st_ref[...] = st
        return carry

    lax.fori_loop(0, nc, chunk, 0)

    if not final:
        out_ref[0] = y_scr[...]
    else:
        yy = yf_ref[0] + y_scr[...]
        mu_y = head_sum(yy) * (1.0 / HEAD_DIM)
        yc = yy - mu_y
        var_y = head_sum(yc * yc) * (1.0 / HEAD_DIM)
        yn = yc * lax.rsqrt(var_y + GN_EPS) * gng_ref[...] + gnb_ref[...]
        out_ref[0] = ((yn + bonus_s[...]) * gate_s[...]).astype(BF16)


def _rwkv_dir(zr, y_fwd, p, d, lt):
    b, t, _ = zr.shape
    nt = t // lt
    reverse = d == 1
    final = y_fwd is not None
    tmap = (lambda j: nt - 1 - j) if reverse else (lambda j: j)
    r8 = lt // 8
    vec = _const_spec((1, D_RWKV))
    in_specs = [pl.BlockSpec((1, lt, D_RWKV_IN), lambda i, j: (i, tmap(j), 0)),
                pl.BlockSpec((1, 8, D_RWKV_IN), lambda i, j: (i, jnp.maximum(tmap(j) * r8 - 1, 0), 0)),
                pl.BlockSpec((1, 8, D_RWKV_IN), lambda i, j: (i, jnp.minimum((tmap(j) + 1) * r8, t // 8 - 1), 0)),
                _const_spec((2, D_RWKV_IN)), vec, _const_spec((W_LORA + A_LORA, D_RWKV)), vec,
                _const_spec((W_LORA + A_LORA, D_RWKV))]
    args = [zr, zr, zr, p['mu'], p['w0'][d], p['rw_w_up'][d], p['a0'][d], p['rw_a_up'][d]]
    n_tok_scratch = 6
    if final:
        in_specs += [_const_spec((G_LORA, D_RWKV)), vec, vec, vec, vec, vec, _const_spec((LANES, LANES)),
                     pl.BlockSpec((1, lt, D_RWKV), lambda i, j: (i, tmap(j), 0))]
        args += [p['g_up'], p['k_k'], p['k_a'], p['r_k'], p['gn_g'], p['gn_b'], p['bd'], y_fwd]
        n_tok_scratch = 8
    else:
        in_specs += [vec, vec, _const_spec((LANES, LANES))]
        args += [p['k_k'], p['k_a'], p['bd']]
    return pl.pallas_call(
        functools.partial(_rwkv_kernel, reverse=reverse, lt=lt, nt=nt, final=final, grp=min(8, lt // CHUNK)),
        grid=(b, nt),
        in_specs=in_specs,
        out_specs=pl.BlockSpec((1, lt, D_RWKV), lambda i, j: (i, tmap(j), 0)),
        out_shape=jax.ShapeDtypeStruct((b, t, D_RWKV), BF16 if final else F32),
        scratch_shapes=[pltpu.VMEM((RWKV_HEADS, HEAD_DIM, HEAD_DIM), F32)]
                       + [pltpu.VMEM((lt, D_RWKV), F32)] * (1 + n_tok_scratch),
        compiler_params=_cparams("arbitrary", "arbitrary"),
        name="rwkv_bwd_final" if final else "rwkv_fwd",
    )(*args)


def _memkv_kernel(m_ref, g_ref, b_ref, w_ref, kv_ref):
    m = _ln(m_ref[...], g_ref[...], b_ref[...])
    kv_ref[...] = jnp.dot(m.astype(BF16), w_ref[...], preferred_element_type=F32).astype(BF16)


def _memkv(mem2d, g, b, w_ckv):
    n = mem2d.shape[0]
    return pl.pallas_call(
        _memkv_kernel,
        grid=(n // N_MEM,),
        in_specs=[pl.BlockSpec((N_MEM, D_MODEL), lambda i: (i, 0)),
                  _const_spec((1, D_MODEL)), _const_spec((1, D_MODEL)),
                  _const_spec((D_MODEL, 2 * D_CROSS))],
        out_specs=pl.BlockSpec((N_MEM, 2 * D_CROSS), lambda i: (i, 0)),
        out_shape=jax.ShapeDtypeStruct((n, 2 * D_CROSS), BF16),
        compiler_params=_cparams("arbitrary"),
        name="mem_kv",
    )(mem2d, g, b, w_ckv)


def _mid_kernel(x_ref, att_ref, rw_ref, kv_ref, wo_ref, l1g_ref, l1b_ref, wq_ref, wco_ref, l2g_ref, l2b_ref,
                wr_ref, br_ref, before_ref, x2_ref, route_ref, routet_ref, meta_ref, *, sub, win):
    mix = (jnp.dot(att_ref[...], wo_ref[:D_ATT, :], preferred_element_type=F32)
           + jnp.dot(rw_ref[...], wo_ref[D_ATT:, :], preferred_element_type=F32))
    x1 = _ln(DEEPNORM_ALPHA * x_ref[...] + mix, l1g_ref[...], l1b_ref[...])

    q = jnp.dot(x1.astype(BF16), wq_ref[...], preferred_element_type=F32)
    q = (q * (CROSS_HEAD_DIM ** -0.5)).astype(BF16)
    kv = kv_ref[...]
    cr = None
    for h in range(CROSS_HEADS):
        hs = slice(h * CROSS_HEAD_DIM, (h + 1) * CROSS_HEAD_DIM)
        kh = kv[:, h * CROSS_HEAD_DIM:(h + 1) * CROSS_HEAD_DIM]
        vh = kv[:, D_CROSS + h * CROSS_HEAD_DIM:D_CROSS + (h + 1) * CROSS_HEAD_DIM]
        s = lax.dot_general(q[:, hs], kh, (((1,), (1,)), ((), ())), preferred_element_type=F32)
        m = jnp.max(s, -1, keepdims=True)
        p = jnp.exp(s - m)
        den = jnp.sum(p, -1, keepdims=True)
        o = jnp.dot(p.astype(BF16), vh, preferred_element_type=F32) / den
        part = jnp.dot(o.astype(BF16), wco_ref[hs, :], preferred_element_type=F32)
        cr = part if cr is None else cr + part
    x2 = _ln(DEEPNORM_ALPHA * x1 + cr, l2g_ref[...], l2b_ref[...])
    x2_ref[...] = x2

    x2_hi = x2.astype(BF16)
    x2_lo = (x2 - x2_hi.astype(F32)).astype(BF16)
    both = jnp.dot(x2_hi, wr_ref[...], preferred_element_type=F32)
    logits = (both[:, :ROUTE_LANES] + both[:, ROUTE_LANES:]
              + jnp.dot(x2_lo, wr_ref[:, :ROUTE_LANES], preferred_element_type=F32)) + br_ref[...]
    lane = lax.broadcasted_iota(jnp.int32, logits.shape, 1)
    is_g = (lane >= N_EXPERTS) & (lane < N_EXPERTS + N_EXPERT_GROUPS)
    lg = jnp.where(is_g, logits, NEG_INF)
    eg = jnp.where(is_g, jnp.exp(lg - jnp.max(lg, -1, keepdims=True)), 0.0)
    pg = eg / jnp.sum(eg, -1, keepdims=True)
    pg_top = jnp.max(pg, -1, keepdims=True)
    gi = jnp.min(jnp.where(is_g & (pg == pg_top), lane, 4 * ROUTE_LANES), -1, keepdims=True) - N_EXPERTS
    in_grp = (lane >= gi * EXPERTS_PER_GROUP) & (lane < (gi + 1) * EXPERTS_PER_GROUP)
    le = jnp.where(in_grp, logits, NEG_INF)
    ee = jnp.where(in_grp, jnp.exp(le - jnp.max(le, -1, keepdims=True)), 0.0)
    pe = ee / jnp.sum(ee, -1, keepdims=True)
    p1 = jnp.max(jnp.where(in_grp, pe, -1.0), -1, keepdims=True)
    i1 = jnp.min(jnp.where(in_grp & (pe == p1), lane, 4 * ROUTE_LANES), -1, keepdims=True)
    rest = in_grp & (lane != i1)
    p2 = jnp.max(jnp.where(rest, pe, -1.0), -1, keepdims=True)
    i2 = jnp.min(jnp.where(rest & (pe == p2), lane, 4 * ROUTE_LANES), -1, keepdims=True)
    tot = p1 + p2
    w1 = p1 / tot * pg_top
    w2 = p2 / tot * pg_top

    tm = logits.shape[0]
    n_sub = tm // sub
    e_lane = lane & (N_EXPERTS - 1)
    sel = ((e_lane == i1) | (e_lane == i2)) & (lane < 2 * N_EXPERTS)
    sel_b = sel.astype(BF16)
    rank = jnp.dot(before_ref[...], sel_b, preferred_element_type=F32)
    si = lax.broadcasted_iota(jnp.int32, (8, tm), 0)
    sj = lax.broadcasted_iota(jnp.int32, (8, tm), 1)
    cnt = jnp.dot(((sj // sub) == si).astype(BF16), sel_b, preferred_element_type=F32)
    padded = jnp.floor((cnt + (SEG_ALIGN - 1)) * (1.0 / SEG_ALIGN)) * SEG_ALIGN
    ui = lax.broadcasted_iota(jnp.int32, (ROUTE_LANES, ROUTE_LANES), 0)
    uj = lax.broadcasted_iota(jnp.int32, (ROUTE_LANES, ROUTE_LANES), 1)
    excl = ((ui < uj) & (uj < N_EXPERTS)).astype(F32)
    lane8 = lax.broadcasted_iota(jnp.int32, (8, ROUTE_LANES), 1)
    off = jnp.dot(jnp.where(lane8 < N_EXPERTS, padded, 0.0), excl, precision=HI, preferred_element_type=F32)
    nblk = jnp.floor((cnt + (win - 1)) * (1.0 / win))
    meta_ref[...] = jnp.where(lane8 < N_EXPERTS, off, nblk).astype(jnp.int32)
    sub_id = lax.broadcasted_iota(jnp.int32, (tm, 1), 0) // sub
    off_tok = jnp.zeros_like(rank)
    for s in range(n_sub):
        off_tok = jnp.where(sub_id == s, off[s:s + 1, :], off_tok)
    dest = off_tok + rank
    d1 = jnp.sum(jnp.where(lane == i1, dest, 0.0), -1, keepdims=True)
    d2 = jnp.sum(jnp.where(lane == i2, dest, 0.0), -1, keepdims=True)
    route = (jnp.where(lane == 0, w1, 0.0) + jnp.where(lane == 1, w2, 0.0)
             + jnp.where(lane == 2, d1, 0.0) + jnp.where(lane == 3, d2, 0.0))
    route_ref[...] = route
    pick = (lane8 == lax.broadcasted_iota(jnp.int32, (8, ROUTE_LANES), 0)).astype(F32)
    routet_ref[...] = lax.dot_general(pick, route, (((1,), (1,)), ((), ())), precision=HI,
                                      preferred_element_type=F32)


def _mid(x2d, att2d, rw2d, kv, p, tm, t, sub, win):
    n = x2d.shape[0]
    per_b = t // tm
    nt = n // tm
    vec = _const_spec((1, D_MODEL))
    tok = jnp.arange(tm)
    before = ((tok[None, :] < tok[:, None]) & ((tok[None, :] // sub) == (tok[:, None] // sub))).astype(BF16)
    return pl.pallas_call(
        functools.partial(_mid_kernel, sub=sub, win=win),
        grid=(nt,),
        in_specs=[pl.BlockSpec((tm, D_MODEL), lambda i: (i, 0)),
                  pl.BlockSpec((tm, D_ATT), lambda i: (i, 0)),
                  pl.BlockSpec((tm, D_RWKV), lambda i: (i, 0)),
                  pl.BlockSpec((N_MEM, 2 * D_CROSS), lambda i: (i // per_b, 0)),
                  _const_spec((D_MODEL, D_MODEL)), vec, vec,
                  _const_spec((D_MODEL, D_CROSS)), _const_spec((D_CROSS, D_MODEL)), vec, vec,
                  _const_spec((D_MODEL, 2 * ROUTE_LANES)), _const_spec((1, ROUTE_LANES)), _const_spec((tm, tm))],
        out_specs=[pl.BlockSpec((tm, D_MODEL), lambda i: (i, 0)),
                   pl.BlockSpec((tm, ROUTE_LANES), lambda i: (i, 0)),
                   pl.BlockSpec((8, tm), lambda i: (i, 0)),
                   pl.BlockSpec((8, ROUTE_LANES), lambda i: (i, 0))],
        out_shape=[jax.ShapeDtypeStruct((n, D_MODEL), F32),
                   jax.ShapeDtypeStruct((n, ROUTE_LANES), F32),
                   jax.ShapeDtypeStruct((nt * 8, tm), F32),
                   jax.ShapeDtypeStruct((nt * 8, ROUTE_LANES), jnp.int32)],
        compiler_params=_cparams("arbitrary"),
        name="mid",
    )(x2d, att2d, rw2d, kv, p['w_out'], p['ln1_g'], p['ln1_b'], p['w_cq'], p['w_co'], p['ln2_g'], p['ln2_b'],
      p['w_route'], p['b_route'], before)


def _moe_kernel(meta_ref, x_ref, route_ref, routet_ref, wg_ref, wu_ref, wd_ref, l3g_ref, l3b_ref, o_ref,
                xs_ref, ys_ref, *, sub, win, rows):
    i = pl.program_id(0)
    step = pl.program_id(1)
    n_sub = xs_ref.shape[0]
    dummy = rows - win

    @pl.when(step == 0)
    def _():
        ys_ref[...] = jnp.zeros_like(ys_ref)
        r_id = lax.broadcasted_iota(jnp.int32, (rows, sub), 0).astype(F32)
        for s in range(n_sub):
            d1 = routet_ref[2:3, s * sub:(s + 1) * sub]
            d2 = routet_ref[3:4, s * sub:(s + 1) * sub]
            perm = ((r_id == d1) | (r_id == d2)).astype(BF16)
            xb = x_ref[s * sub:(s + 1) * sub, :].astype(BF16)
            xs_ref[s] = jnp.dot(perm, xb, preferred_element_type=F32).astype(BF16)

    base = i * (n_sub * 32)
    for ee in range(EXPERTS_PER_STEP):
        e = step * EXPERTS_PER_STEP + ee
        offs = [meta_ref[base + s * 32 + e] for s in range(n_sub)]
        nbs = [meta_ref[base + s * 32 + N_EXPERTS + e] for s in range(n_sub)]
        nb_max = functools.reduce(jnp.maximum, nbs)

        def block(j, carry, ee=ee, offs=offs, nbs=nbs):
            starts = [pl.multiple_of(jnp.where(j < nbs[s], offs[s] + j * win, dummy), SEG_ALIGN)
                      for s in range(n_sub)]
            xw = jnp.concatenate([xs_ref[s, pl.ds(starts[s], win), :] for s in range(n_sub)], axis=0)
            gate = jnp.dot(xw, wg_ref[ee], preferred_element_type=F32)
            up = jnp.dot(xw, wu_ref[ee], preferred_element_type=F32)
            hdn = (gate * jax.nn.sigmoid(gate)) * up
            ye = jnp.dot(hdn.astype(BF16), wd_ref[ee], preferred_element_type=F32).astype(BF16)
            for s in range(n_sub):
                ys_ref[s, pl.ds(starts[s], win), :] = ye[s * win:(s + 1) * win]
            return carry

        lax.fori_loop(0, nb_max, block, 0)

    @pl.when(step == N_EXPERTS // EXPERTS_PER_STEP - 1)
    def _():
        c_id = lax.broadcasted_iota(jnp.int32, (sub, rows), 1).astype(F32)
        for s in range(n_sub):
            rt = route_ref[s * sub:(s + 1) * sub, :]
            w1, w2, d1, d2 = rt[:, 0:1], rt[:, 1:2], rt[:, 2:3], rt[:, 3:4]
            comb = jnp.where(c_id == d1, w1, 0.0) + jnp.where(c_id == d2, w2, 0.0)
            hi = comb.astype(BF16)
            lo = (comb - hi.astype(F32)).astype(BF16)
            ysv = ys_ref[s]
            ff = (jnp.dot(hi, ysv, preferred_element_type=F32) + jnp.dot(lo, ysv, preferred_element_type=F32))
            xr = x_ref[s * sub:(s + 1) * sub, :]
            o_ref[s * sub:(s + 1) * sub, :] = _ln(DEEPNORM_ALPHA * xr + ff, l3g_ref[...], l3b_ref[...])


def _moe(x2, route, routet, meta, p, tm, sub, win):
    n = x2.shape[0]
    nt = n // tm
    n_sub = tm // sub
    rows = 2 * sub + N_EXPERTS * SEG_ALIGN + 2 * win
    meta_flat = meta.reshape(nt, 8, ROUTE_LANES)[:, :n_sub, :32].reshape(-1)
    vec = pl.BlockSpec((1, D_MODEL), lambda i, e, m: (0, 0))
    grid_spec = pltpu.PrefetchScalarGridSpec(
        num_scalar_prefetch=1,
        grid=(nt, N_EXPERTS // EXPERTS_PER_STEP),
        in_specs=[pl.BlockSpec((tm, D_MODEL), lambda i, e, m: (i, 0)),
                  pl.BlockSpec((tm, ROUTE_LANES), lambda i, e, m: (i, 0)),
                  pl.BlockSpec((8, tm), lambda i, e, m: (i, 0)),
                  pl.BlockSpec((EXPERTS_PER_STEP, D_MODEL, D_EXPERT), lambda i, e, m: (e, 0, 0)),
                  pl.BlockSpec((EXPERTS_PER_STEP, D_MODEL, D_EXPERT), lambda i, e, m: (e, 0, 0)),
                  pl.BlockSpec((EXPERTS_PER_STEP, D_EXPERT, D_MODEL), lambda i, e, m: (e, 0, 0)),
                  vec, vec],
        out_specs=pl.BlockSpec((tm, D_MODEL), lambda i, e, m: (i, 0)),
        scratch_shapes=[pltpu.VMEM((n_sub, rows, D_MODEL), BF16),
                        pltpu.VMEM((n_sub, rows, D_MODEL), BF16)])
    return pl.pallas_call(
        functools.partial(_moe_kernel, sub=sub, win=win, rows=rows),
        grid_spec=grid_spec,
        out_shape=jax.ShapeDtypeStruct((n, D_MODEL), F32),
        compiler_params=_cparams("arbitrary", "arbitrary"),
        name="moe",
    )(meta_flat, x2, route, routet, p['w_gate'], p['w_up'], p['w_down'], p['ln3_g'], p['ln3_b'])


def _prep_params(w_in, tshift_mu, attn_sink, rwkv_w0, rwkv_w_up, rwkv_a0, rwkv_a_up, rwkv_g_up, rwkv_k_k,
                 rwkv_k_a, rwkv_r_k, rwkv_gn_g, rwkv_gn_b, w_out, ln1_g, ln1_b, mem_ln_g, mem_ln_b, w_cq, w_ckv,
                 w_co, ln2_g, ln2_b, w_route_group, b_route_group, w_route_expert, b_route_expert,
                 w_exp_gate, w_exp_up, w_exp_down, ln3_g, ln3_b):
    row = lambda a: a.reshape(1, -1).astype(F32)
    zeros_lora = jnp.zeros((2, W_LORA, D_RWKV), F32)
    hid = jnp.arange(LANES) // HEAD_DIM
    pad = ROUTE_LANES - N_EXPERTS - N_EXPERT_GROUPS
    return {
        'w_in_att': w_in[:, :D_QKV].astype(BF16),
        'w_in_rw': w_in[:, D_QKV:].astype(BF16),
        'sink': attn_sink.astype(F32),
        'mu': tshift_mu.astype(F32),
        'w0': rwkv_w0.reshape(2, 1, D_RWKV).astype(F32),
        'rw_w_up': jnp.concatenate([rwkv_w_up, zeros_lora], axis=1).astype(BF16),
        'a0': rwkv_a0.reshape(2, 1, D_RWKV).astype(F32),
        'rw_a_up': jnp.concatenate([zeros_lora, rwkv_a_up], axis=1).astype(BF16),
        'g_up': rwkv_g_up.astype(BF16),
        'k_k': row(rwkv_k_k), 'k_a': row(rwkv_k_a), 'r_k': row(rwkv_r_k),
        'gn_g': row(rwkv_gn_g), 'gn_b': row(rwkv_gn_b),
        'bd': (hid[:, None] == hid[None, :]).astype(BF16),
        'w_out': w_out.astype(BF16),
        'ln1_g': row(ln1_g), 'ln1_b': row(ln1_b),
        'mem_ln_g': row(mem_ln_g), 'mem_ln_b': row(mem_ln_b),
        'w_cq': w_cq.astype(BF16), 'w_ckv': w_ckv.astype(BF16), 'w_co': w_co.astype(BF16),
        'ln2_g': row(ln2_g), 'ln2_b': row(ln2_b),
        'w_route': _hi_lo(jnp.pad(jnp.concatenate([w_route_expert, w_route_group], axis=1), ((0, 0), (0, pad)))),
        'b_route': jnp.pad(jnp.concatenate([b_route_expert, b_route_group]), (0, pad)).reshape(1, -1).astype(F32),
        'w_gate': w_exp_gate.astype(BF16), 'w_up': w_exp_up.astype(BF16), 'w_down': w_exp_down.astype(BF16),
        'ln3_g': row(ln3_g), 'ln3_b': row(ln3_b),
    }


def _hi_lo(w):
    w = w.astype(F32)
    hi = w.astype(BF16)
    return jnp.concatenate([hi, (w - hi.astype(F32)).astype(BF16)], axis=-1)


def _tile(n, pref):
    t = pref
    while n % t:
        t //= 2
    return t


def _layer(x, mem, p):
    b, t, _ = x.shape
    n = b * t
    x2d = x.reshape(n, D_MODEL)
    tm = _tile(t, 512)
    qkv, zr = _inproj(x2d, p['w_in_att'], p['w_in_rw'], tm)
    att = _attention(qkv.reshape(b, t, D_QKV), p['sink'])
    zr3 = zr.reshape(b, t, D_RWKV_IN)
    lt = _tile(t, 512)
    y_fwd = _rwkv_dir(zr3, None, p, 0, lt)
    rw = _rwkv_dir(zr3, y_fwd, p, 1, lt)
    kv = _memkv(mem.reshape(b * N_MEM, D_MODEL), p['mem_ln_g'], p['mem_ln_b'], p['w_ckv'])
    tme = _tile(t, 1024)
    sub = _tile(tme, 256)
    win = sub // 4
    x2, route, routet, meta = _mid(x2d, att.reshape(n, D_ATT), rw.reshape(n, D_RWKV), kv, p, tme, t, sub, win)
    y = _moe(x2, route, routet, meta, p, tme, sub, win)
    return y.reshape(b, t, D_MODEL)


def kernel(x_prompt, x_sample, mem_prompt, mem_sample, w_in, tshift_mu, attn_sink, rwkv_w0, rwkv_w_up, rwkv_a0, rwkv_a_up, rwkv_g_up, rwkv_k_k, rwkv_k_a, rwkv_r_k, rwkv_gn_g, rwkv_gn_b, w_out, ln1_g, ln1_b, mem_ln_g, mem_ln_b, w_cq, w_ckv, w_co, ln2_g, ln2_b, w_route_group, b_route_group, w_route_expert, b_route_expert, w_exp_gate, w_exp_up, w_exp_down, ln3_g, ln3_b):
    weights = (w_in, tshift_mu, attn_sink, rwkv_w0, rwkv_w_up, rwkv_a0, rwkv_a_up, rwkv_g_up, rwkv_k_k, rwkv_k_a,
               rwkv_r_k, rwkv_gn_g, rwkv_gn_b, w_out, ln1_g, ln1_b, mem_ln_g, mem_ln_b, w_cq, w_ckv, w_co,
               ln2_g, ln2_b, w_route_group, b_route_group, w_route_expert, b_route_expert,
               w_exp_gate, w_exp_up, w_exp_down, ln3_g, ln3_b)
    p = _prep_params(*[w[0] for w in weights])
    return (_layer(x_prompt, mem_prompt, p), _layer(x_sample, mem_sample, p))
```

```python
import functools
import math

import jax
import jax.numpy as jnp
from jax import lax
from jax.experimental import pallas as pl
from jax.experimental.pallas import tpu as pltpu

F32 = jnp.float32
BF16 = jnp.bfloat16
HI = lax.Precision.HIGHEST

D_MODEL = 1024
HEAD_DIM = 64
ATT_Q_HEADS = 8
ATT_KV_HEADS = 2
ATT_GROUP = ATT_Q_HEADS // ATT_KV_HEADS
WINDOW = 128
BLOCK = 128
RWKV_HEADS = 8
D_ATT = ATT_Q_HEADS * HEAD_DIM
D_KV = ATT_KV_HEADS * HEAD_DIM
D_QKV = D_ATT + 2 * D_KV
D_RWKV = RWKV_HEADS * HEAD_DIM
W_LORA = 64
A_LORA = 64
G_LORA = 128
D_RWKV_IN = 3 * D_RWKV + W_LORA + A_LORA + G_LORA
N_MEM = 256
CROSS_HEADS = 4
CROSS_HEAD_DIM = 128
D_CROSS = CROSS_HEADS * CROSS_HEAD_DIM
N_EXPERT_GROUPS = 4
EXPERTS_PER_GROUP = 4
N_EXPERTS = N_EXPERT_GROUPS * EXPERTS_PER_GROUP
D_EXPERT = 512
LN_EPS = 1e-5
GN_EPS = 64e-5
DEEPNORM_ALPHA = 2.0 ** 0.25
NEG_INF = -1e30
LANES = 128
ROUTE_LANES = LANES
DECAY_SCALE = math.exp(-0.5)
CHUNK = 64
SEG_ALIGN = 16
EXPERTS_PER_STEP = 2
VMEM_LIMIT = 56 * 1024 * 1024


def _cparams(*sem):
    return pltpu.CompilerParams(dimension_semantics=sem, vmem_limit_bytes=VMEM_LIMIT)


def _ln(x, g, b):
    mu = jnp.mean(x, -1, keepdims=True)
    xc = x - mu
    var = jnp.mean(xc * xc, -1, keepdims=True)
    return xc * lax.rsqrt(var + LN_EPS) * g + b


def _const_spec(shape):
    nd = len(shape)
    return pl.BlockSpec(shape, lambda *_: (0,) * nd)


def _inproj_kernel(x_ref, wa_ref, wr_ref, qkv_ref, zr_ref):
    xb = x_ref[...].astype(BF16)
    qkv_ref[...] = jnp.dot(xb, wa_ref[...], preferred_element_type=F32).astype(BF16)
    zr_ref[...] = jnp.dot(xb, wr_ref[...], preferred_element_type=F32)


def _inproj(x2d, w_att, w_rw, tm):
    n = x2d.shape[0]
    return pl.pallas_call(
        _inproj_kernel,
        grid=(n // tm,),
        in_specs=[pl.BlockSpec((tm, D_MODEL), lambda i: (i, 0)),
                  _const_spec((D_MODEL, D_QKV)),
                  _const_spec((D_MODEL, D_RWKV_IN))],
        out_specs=[pl.BlockSpec((tm, D_QKV), lambda i: (i, 0)),
                   pl.BlockSpec((tm, D_RWKV_IN), lambda i: (i, 0))],
        out_shape=[jax.ShapeDtypeStruct((n, D_QKV), BF16),
                   jax.ShapeDtypeStruct((n, D_RWKV_IN), F32)],
        compiler_params=_cparams("arbitrary"),
        name="inproj",
    )(x2d, w_att, w_rw)


def _attn_kernel(sink_ref, cur_ref, prv_ref, nxt_ref, o_ref, *, ns, qb):
    n = pl.program_id(1)
    blocks = [prv_ref[0]] + [cur_ref[0, a * BLOCK:(a + 1) * BLOCK, :] for a in range(qb)] + [nxt_ref[0]]
    rows = ATT_GROUP * BLOCK
    ri = lax.broadcasted_iota(jnp.int32, (rows, 3 * BLOCK), 0)
    ki = lax.broadcasted_iota(jnp.int32, (rows, 3 * BLOCK), 1)
    adist = jnp.abs((ri & (BLOCK - 1)) - ki + BLOCK)
    band = adist <= WINDOW
    adist_f = adist.astype(F32)
    grp_id = lax.broadcasted_iota(jnp.int32, (rows, 1), 0) // BLOCK
    biases, sinks = [], []
    for h in range(ATT_KV_HEADS):
        slope = jnp.zeros((rows, 1), F32)
        sink = jnp.zeros((rows, 1), F32)
        for g in range(ATT_GROUP):
            hq = h * ATT_GROUP + g
            slope = jnp.where(grp_id == g, 2.0 ** (-8.0 / ATT_Q_HEADS * (hq + 1)), slope)
            sink = jnp.where(grp_id == g, sink_ref[hq], sink)
        biases.append(slope * adist_f)
        sinks.append(sink)
    for a in range(qb):
        valid = band
        if a == 0:
            valid = valid & ((ki >= BLOCK) | (n > 0))
        if a == qb - 1:
            valid = valid & ((ki < 2 * BLOCK) | (n < ns - 1))
        cur = blocks[a + 1]
        for h in range(ATT_KV_HEADS):
            k0 = D_ATT + h * HEAD_DIM
            v0 = D_ATT + D_KV + h * HEAD_DIM
            kcat = jnp.concatenate([blk[:, k0:k0 + HEAD_DIM] for blk in blocks[a:a + 3]], axis=0)
            vcat = jnp.concatenate([blk[:, v0:v0 + HEAD_DIM] for blk in blocks[a:a + 3]], axis=0)
            q0 = h * ATT_GROUP * HEAD_DIM
            q = jnp.concatenate([cur[:, q0 + g * HEAD_DIM:q0 + (g + 1) * HEAD_DIM] for g in range(ATT_GROUP)], axis=0)
            q = q * (HEAD_DIM ** -0.5)
            s = lax.dot_general(q, kcat, (((1,), (1,)), ((), ())), preferred_element_type=F32)
            s = jnp.where(valid, s - biases[h], NEG_INF)
            sk = sinks[h]
            m = jnp.maximum(jnp.max(s, -1, keepdims=True), sk)
            p = jnp.exp(s - m)
            den = jnp.sum(p, -1, keepdims=True) + jnp.exp(sk - m)
            o = (jnp.dot(p.astype(BF16), vcat, preferred_element_type=F32) / den).astype(BF16)
            for g in range(ATT_GROUP):
                o_ref[0, a * BLOCK:(a + 1) * BLOCK, q0 + g * HEAD_DIM:q0 + (g + 1) * HEAD_DIM] = (
                    o[g * BLOCK:(g + 1) * BLOCK])


def _attention(qkv, sink):
    b, t, _ = qkv.shape
    nb = t // BLOCK
    qb = _tile(nb, 4)
    ns = nb // qb
    return pl.pallas_call(
        functools.partial(_attn_kernel, ns=ns, qb=qb),
        grid=(b, ns),
        in_specs=[pl.BlockSpec(memory_space=pltpu.SMEM),
                  pl.BlockSpec((1, qb * BLOCK, D_QKV), lambda i, j: (i, j, 0)),
                  pl.BlockSpec((1, BLOCK, D_QKV), lambda i, j: (i, jnp.maximum(j * qb - 1, 0), 0)),
                  pl.BlockSpec((1, BLOCK, D_QKV), lambda i, j: (i, jnp.minimum((j + 1) * qb, nb - 1), 0))],
        out_specs=pl.BlockSpec((1, qb * BLOCK, D_ATT), lambda i, j: (i, j, 0)),
        out_shape=jax.ShapeDtypeStruct((b, t, D_ATT), BF16),
        compiler_params=_cparams("arbitrary", "arbitrary"),
        name="win_attn",
    )(sink, qkv, qkv, qkv)


def _heads(x):
    return jnp.stack([x[:, h * HEAD_DIM:(h + 1) * HEAD_DIM] for h in range(RWKV_HEADS)], axis=0)


def _bmm(a, b):
    return jnp.einsum('hlj,hjm->hlm', a.astype(BF16), b.astype(BF16), preferred_element_type=F32)


def _split3(x):
    hi = x.astype(BF16)
    r1 = x - hi.astype(F32)
    mid = r1.astype(BF16)
    lo = (r1 - mid.astype(F32)).astype(BF16)
    return hi, mid, lo


def _rwkv_kernel(*refs, reverse, lt, nt, final, grp):
    if final:
        (zc_ref, zp_ref, zn_ref, mu_ref, w0_ref, wup_ref, a0_ref, aup_ref, gup_ref, kk_ref, ka_ref, rk_ref,
         gng_ref, gnb_ref, bd_ref, yf_ref, out_ref,
         st_ref, y_scr, r_s, kd_s, v_s, a_s, b_s, lw_s, bonus_s, gate_s) = refs
    else:
        (zc_ref, zp_ref, zn_ref, mu_ref, w0_ref, wup_ref, a0_ref, aup_ref, kk_ref, ka_ref, bd_ref,
         out_ref, st_ref, y_scr, r_s, kd_s, v_s, a_s, b_s, lw_s) = refs
    i = pl.program_id(1)
    tt = (nt - 1 - i) if reverse else i

    @pl.when(i == 0)
    def _():
        st_ref[...] = jnp.zeros_like(st_ref)

    bd = bd_ref[...]

    def head_sum(x):
        hi = x.astype(BF16)
        lo = (x - hi.astype(F32)).astype(BF16)
        tiles = []
        for t in range(D_RWKV // LANES):
            sl = slice(t * LANES, (t + 1) * LANES)
            tiles.append(jnp.dot(hi[:, sl], bd, preferred_element_type=F32)
                         + jnp.dot(lo[:, sl], bd, preferred_element_type=F32))
        return jnp.concatenate(tiles, axis=-1)

    z = zc_ref[0]
    row = lax.broadcasted_iota(jnp.int32, (lt, 1), 0)
    zprev_edge = jnp.where(tt > 0, zp_ref[0, 7:8, :], 0.0)
    znext_edge = jnp.where(tt < nt - 1, zn_ref[0, 0:1, :], 0.0)
    prev = jnp.where(row == 0, zprev_edge, pltpu.roll(z, 1, 0))
    nxt = jnp.where(row == lt - 1, znext_edge, pltpu.roll(z, lt - 1, 0))
    zs = z + mu_ref[0:1, :] * (prev - z) + mu_ref[1:2, :] * (nxt - z)

    o1, o2, o3 = D_RWKV, 2 * D_RWKV, 3 * D_RWKV
    r = zs[:, :o1]
    k = zs[:, o1:o2]
    v = zs[:, o2:o3]
    wad = zs[:, o3:o3 + W_LORA + A_LORA]
    kk = k * kk_ref[...]
    kk = kk * lax.rsqrt(head_sum(kk * kk) + 1e-12)
    w_arg = w0_ref[...] + jnp.dot(jnp.tanh(wad).astype(BF16), wup_ref[...], preferred_element_type=F32)
    a_sig = jax.nn.sigmoid(a0_ref[...] + jnp.dot(wad.astype(BF16), aup_ref[...], preferred_element_type=F32))
    r_s[...] = r
    v_s[...] = v
    kd_s[...] = k * (1.0 + (a_sig - 1.0) * ka_ref[...])
    a_s[...] = -kk
    b_s[...] = kk * a_sig
    lw_s[...] = (-DECAY_SCALE) * jax.nn.sigmoid(w_arg)
    if final:
        bonus_s[...] = head_sum(r * k * rk_ref[...]) * v
        gd = zs[:, o3 + W_LORA + A_LORA:]
        gate_s[...] = jnp.dot(jax.nn.sigmoid(gd).astype(BF16), gup_ref[...], preferred_element_type=F32)

    span = grp * CHUNK
    li = lax.broadcasted_iota(jnp.int32, (2 * CHUNK, 2 * CHUNK), 0)
    lj = lax.broadcasted_iota(jnp.int32, (2 * CHUNK, 2 * CHUNK), 1)
    lt_, jt_ = li & (CHUNK - 1), lj & (CHUNK - 1)
    if reverse:
        strict, incl = lt_ < jt_, lt_ <= jt_
    else:
        strict, incl = lt_ > jt_, lt_ >= jt_
    pmask = strict | (incl & (li >= CHUNK))
    oi = lax.broadcasted_iota(jnp.int32, (16, grp * 2 * CHUNK), 0)
    oj = lax.broadcasted_iota(jnp.int32, (16, grp * 2 * CHUNK), 1)
    spread = (oi == (oj // (2 * CHUNK))).astype(BF16)
    ci = lax.broadcasted_iota(jnp.int32, (CHUNK, CHUNK), 0)
    cj = lax.broadcasted_iota(jnp.int32, (CHUNK, CHUNK), 1)
    eye = (ci == cj).astype(F32)
    tri = ((ci <= cj) if reverse else (ci >= cj)).astype(BF16)
    zeros_h = jnp.zeros((grp * RWKV_HEADS, CHUNK, HEAD_DIM), BF16)
    nc = lt // span

    def heads(x):
        return jnp.stack([x[g * CHUNK:(g + 1) * CHUNK, h * HEAD_DIM:(h + 1) * HEAD_DIM]
                          for g in range(grp) for h in range(RWKV_HEADS)], axis=0)

    def chunk(j, carry):
        c = (nc - 1 - j) if reverse else j
        base = pl.multiple_of(c * span, span)
        rows = pl.ds(base, span)
        lw_c = lw_s[rows, :]
        parts = _split3(lw_c)
        cum = jnp.concatenate(
            [sum(jnp.dot(tri, q[g * CHUNK:(g + 1) * CHUNK], preferred_element_type=F32) for q in parts)
             for g in range(grp)], axis=0)
        g_rows = jnp.exp(cum.reshape(grp, CHUNK, D_RWKV)[:, (0 if reverse else CHUNK - 1), :])
        g_rows = jnp.concatenate([g_rows, jnp.zeros((spread.shape[0] - grp, D_RWKV), F32)], axis=0)
        g_all = sum(lax.dot_general(q, spread, (((0,), (0,)), ((), ())), preferred_element_type=F32)
                    for q in _split3(g_rows)[:2]).reshape(RWKV_HEADS, HEAD_DIM, grp * 2 * CHUNK)
        e_pos = jnp.exp(cum)
        e_neg = jnp.exp(-cum)
        e_prev = jnp.exp(cum - lw_c)
        ar = jnp.concatenate([heads((a_s[rows, :] * e_prev).astype(BF16)),
                              heads((r_s[rows, :] * e_pos).astype(BF16))], axis=1)
        bk = jnp.concatenate([heads((b_s[rows, :] * e_neg).astype(BF16)),
                              heads((kd_s[rows, :] * e_neg).astype(BF16))], axis=1)
        vh = heads(v_s[rows, :].astype(BF16))
        pw = jnp.einsum('hlc,hjc->hlj', ar, bk, preferred_element_type=F32)
        pw = jnp.where(pmask, pw, 0.0)
        pw_b = pw.astype(BF16)
        m_ab = pw[:, :CHUNK, :CHUNK]
        tinv = eye + m_ab
        mp = m_ab
        for _ in range(5):
            mp = _bmm(mp, mp)
            tinv = tinv + _bmm(tinv, mp)
        tinv_b = tinv.astype(BF16)
        mv = _bmm(pw_b[:, :CHUNK], jnp.concatenate([zeros_h, vh], axis=1))

        st = st_ref[...]
        for q in (range(grp - 1, -1, -1) if reverse else range(grp)):
            hs = slice(q * RWKV_HEADS, (q + 1) * RWKV_HEADS)
            ar_st = _bmm(ar[hs], st)
            u = _bmm(tinv_b[hs], ar_st[:, :CHUNK] + mv[hs])
            uv = jnp.concatenate([u.astype(BF16), vh[hs]], axis=1)
            y = ar_st[:, CHUNK:] + _bmm(pw_b[hs][:, CHUNK:], uv)
            g_col = g_all[:, :, q * 2 * CHUNK:q * 2 * CHUNK + HEAD_DIM]
            st = g_col * (st + jnp.einsum('hjc,hjv->hcv', bk[hs], uv, preferred_element_type=F32))
            for h in range(RWKV_HEADS):
                y_scr[pl.ds(base + q * CHUNK, CHUNK), h * HEAD_DIM:(h + 1) * HEAD_DIM] = y[h]
        st_ref[...] = st
        return carry

    lax.fori_loop(0, nc, chunk, 0)

    if not final:
        out_ref[0] = y_scr[...]
    else:
        yy = yf_ref[0] + y_scr[...]
        mu_y = head_sum(yy) * (1.0 / HEAD_DIM)
        yc = yy - mu_y
        var_y = head_sum(yc * yc) * (1.0 / HEAD_DIM)
        yn = yc * lax.rsqrt(var_y + GN_EPS) * gng_ref[...] + gnb_ref[...]
        out_ref[0] = ((yn + bonus_s[...]) * gate_s[...]).astype(BF16)


def _rwkv_dir(zr, y_fwd, p, d, lt):
    b, t, _ = zr.shape
    nt = t // lt
    reverse = d == 1
    final = y_fwd is not None
    tmap = (lambda j: nt - 1 - j) if reverse else (lambda j: j)
    r8 = lt // 8
    vec = _const_spec((1, D_RWKV))
    in_specs = [pl.BlockSpec((1, lt, D_RWKV_IN), lambda i, j: (i, tmap(j), 0)),
                pl.BlockSpec((1, 8, D_RWKV_IN), lambda i, j: (i, jnp.maximum(tmap(j) * r8 - 1, 0), 0)),
                pl.BlockSpec((1, 8, D_RWKV_IN), lambda i, j: (i, jnp.minimum((tmap(j) + 1) * r8, t // 8 - 1), 0)),
                _const_spec((2, D_RWKV_IN)), vec, _const_spec((W_LORA + A_LORA, D_RWKV)), vec,
                _const_spec((W_LORA + A_LORA, D_RWKV))]
    args = [zr, zr, zr, p['mu'], p['w0'][d], p['rw_w_up'][d], p['a0'][d], p['rw_a_up'][d]]
    n_tok_scratch = 6
    if final:
        in_specs += [_const_spec((G_LORA, D_RWKV)), vec, vec, vec, vec, vec, _const_spec((LANES, LANES)),
                     pl.BlockSpec((1, lt, D_RWKV), lambda i, j: (i, tmap(j), 0))]
        args += [p['g_up'], p['k_k'], p['k_a'], p['r_k'], p['gn_g'], p['gn_b'], p['bd'], y_fwd]
        n_tok_scratch = 8
    else:
        in_specs += [vec, vec, _const_spec((LANES, LANES))]
        args += [p['k_k'], p['k_a'], p['bd']]
    return pl.pallas_call(
        functools.partial(_rwkv_kernel, reverse=reverse, lt=lt, nt=nt, final=final, grp=min(8, lt // CHUNK)),
        grid=(b, nt),
        in_specs=in_specs,
        out_specs=pl.BlockSpec((1, lt, D_RWKV), lambda i, j: (i, tmap(j), 0)),
        out_shape=jax.ShapeDtypeStruct((b, t, D_RWKV), BF16 if final else F32),
        scratch_shapes=[pltpu.VMEM((RWKV_HEADS, HEAD_DIM, HEAD_DIM), F32)]
                       + [pltpu.VMEM((lt, D_RWKV), F32)] * (1 + n_tok_scratch),
        compiler_params=_cparams("arbitrary", "arbitrary"),
        name="rwkv_bwd_final" if final else "rwkv_fwd",
    )(*args)


def _memkv_kernel(m_ref, g_ref, b_ref, w_ref, kv_ref):
    m = _ln(m_ref[...], g_ref[...], b_ref[...])
    kv_ref[...] = jnp.dot(m.astype(BF16), w_ref[...], preferred_element_type=F32).astype(BF16)


def _memkv(mem2d, g, b, w_ckv):
    n = mem2d.shape[0]
    return pl.pallas_call(
        _memkv_kernel,
        grid=(n // N_MEM,),
        in_specs=[pl.BlockSpec((N_MEM, D_MODEL), lambda i: (i, 0)),
                  _const_spec((1, D_MODEL)), _const_spec((1, D_MODEL)),
                  _const_spec((D_MODEL, 2 * D_CROSS))],
        out_specs=pl.BlockSpec((N_MEM, 2 * D_CROSS), lambda i: (i, 0)),
        out_shape=jax.ShapeDtypeStruct((n, 2 * D_CROSS), BF16),
        compiler_params=_cparams("arbitrary"),
        name="mem_kv",
    )(mem2d, g, b, w_ckv)


def _mid_kernel(x_ref, att_ref, rw_ref, kv_ref, wo_ref, l1g_ref, l1b_ref, wq_ref, wco_ref, l2g_ref, l2b_ref,
                wr_ref, br_ref, before_ref, x2_ref, route_ref, routet_ref, meta_ref, *, sub, win):
    mix = (jnp.dot(att_ref[...], wo_ref[:D_ATT, :], preferred_element_type=F32)
           + jnp.dot(rw_ref[...], wo_ref[D_ATT:, :], preferred_element_type=F32))
    x1 = _ln(DEEPNORM_ALPHA * x_ref[...] + mix, l1g_ref[...], l1b_ref[...])

    q = jnp.dot(x1.astype(BF16), wq_ref[...], preferred_element_type=F32)
    q = (q * (CROSS_HEAD_DIM ** -0.5)).astype(BF16)
    kv = kv_ref[...]
    cr = None
    for h in range(CROSS_HEADS):
        hs = slice(h * CROSS_HEAD_DIM, (h + 1) * CROSS_HEAD_DIM)
        kh = kv[:, h * CROSS_HEAD_DIM:(h + 1) * CROSS_HEAD_DIM]
        vh = kv[:, D_CROSS + h * CROSS_HEAD_DIM:D_CROSS + (h + 1) * CROSS_HEAD_DIM]
        s = lax.dot_general(q[:, hs], kh, (((1,), (1,)), ((), ())), preferred_element_type=F32)
        m = jnp.max(s, -1, keepdims=True)
        p = jnp.exp(s - m)
        den = jnp.sum(p, -1, keepdims=True)
        o = jnp.dot(p.astype(BF16), vh, preferred_element_type=F32) / den
        part = jnp.dot(o.astype(BF16), wco_ref[hs, :], preferred_element_type=F32)
        cr = part if cr is None else cr + part
    x2 = _ln(DEEPNORM_ALPHA * x1 + cr, l2g_ref[...], l2b_ref[...])
    x2_ref[...] = x2

    x2_hi = x2.astype(BF16)
    x2_lo = (x2 - x2_hi.astype(F32)).astype(BF16)
    both = jnp.dot(x2_hi, wr_ref[...], preferred_element_type=F32)
    logits = (both[:, :ROUTE_LANES] + both[:, ROUTE_LANES:]
              + jnp.dot(x2_lo, wr_ref[:, :ROUTE_LANES], preferred_element_type=F32)) + br_ref[...]
    lane = lax.broadcasted_iota(jnp.int32, logits.shape, 1)
    is_g = (lane >= N_EXPERTS) & (lane < N_EXPERTS + N_EXPERT_GROUPS)
    lg = jnp.where(is_g, logits, NEG_INF)
    eg = jnp.where(is_g, jnp.exp(lg - jnp.max(lg, -1, keepdims=True)), 0.0)
    pg = eg / jnp.sum(eg, -1, keepdims=True)
    pg_top = jnp.max(pg, -1, keepdims=True)
    gi = jnp.min(jnp.where(is_g & (pg == pg_top), lane, 4 * ROUTE_LANES), -1, keepdims=True) - N_EXPERTS
    in_grp = (lane >= gi * EXPERTS_PER_GROUP) & (lane < (gi + 1) * EXPERTS_PER_GROUP)
    le = jnp.where(in_grp, logits, NEG_INF)
    ee = jnp.where(in_grp, jnp.exp(le - jnp.max(le, -1, keepdims=True)), 0.0)
    pe = ee / jnp.sum(ee, -1, keepdims=True)
    p1 = jnp.max(jnp.where(in_grp, pe, -1.0), -1, keepdims=True)
    i1 = jnp.min(jnp.where(in_grp & (pe == p1), lane, 4 * ROUTE_LANES), -1, keepdims=True)
    rest = in_grp & (lane != i1)
    p2 = jnp.max(jnp.where(rest, pe, -1.0), -1, keepdims=True)
    i2 = jnp.min(jnp.where(rest & (pe == p2), lane, 4 * ROUTE_LANES), -1, keepdims=True)
    tot = p1 + p2
    w1 = p1 / tot * pg_top
    w2 = p2 / tot * pg_top

    tm = logits.shape[0]
    n_sub = tm // sub
    e_lane = lane & (N_EXPERTS - 1)
    sel = ((e_lane == i1) | (e_lane == i2)) & (lane < 2 * N_EXPERTS)
    sel_b = sel.astype(BF16)
    rank = jnp.dot(before_ref[...], sel_b, preferred_element_type=F32)
    si = lax.broadcasted_iota(jnp.int32, (8, tm), 0)
    sj = lax.broadcasted_iota(jnp.int32, (8, tm), 1)
    cnt = jnp.dot(((sj // sub) == si).astype(BF16), sel_b, preferred_element_type=F32)
    padded = jnp.floor((cnt + (SEG_ALIGN - 1)) * (1.0 / SEG_ALIGN)) * SEG_ALIGN
    ui = lax.broadcasted_iota(jnp.int32, (ROUTE_LANES, ROUTE_LANES), 0)
    uj = lax.broadcasted_iota(jnp.int32, (ROUTE_LANES, ROUTE_LANES), 1)
    excl = ((ui < uj) & (uj < N_EXPERTS)).astype(F32)
    lane8 = lax.broadcasted_iota(jnp.int32, (8, ROUTE_LANES), 1)
    off = jnp.dot(jnp.where(lane8 < N_EXPERTS, padded, 0.0), excl, precision=HI, preferred_element_type=F32)
    nblk = jnp.floor((cnt + (win - 0.5)) * (1.0 / win))
    meta_ref[...] = jnp.where(lane8 < N_EXPERTS, off, nblk).astype(jnp.int32)
    sub_id = lax.broadcasted_iota(jnp.int32, (tm, 1), 0) // sub
    off_tok = jnp.zeros_like(rank)
    for s in range(n_sub):
        off_tok = jnp.where(sub_id == s, off[s:s + 1, :], off_tok)
    dest = off_tok + rank
    d1 = jnp.sum(jnp.where(lane == i1, dest, 0.0), -1, keepdims=True)
    d2 = jnp.sum(jnp.where(lane == i2, dest, 0.0), -1, keepdims=True)
    route = (jnp.where(lane == 0, w1, 0.0) + jnp.where(lane == 1, w2, 0.0)
             + jnp.where(lane == 2, d1, 0.0) + jnp.where(lane == 3, d2, 0.0))
    route_ref[...] = route
    pick = (lane8 == lax.broadcasted_iota(jnp.int32, (8, ROUTE_LANES), 0)).astype(F32)
    routet_ref[...] = lax.dot_general(pick, route, (((1,), (1,)), ((), ())), precision=HI,
                                      preferred_element_type=F32)


def _mid(x2d, att2d, rw2d, kv, p, tm, t, sub, win):
    n = x2d.shape[0]
    per_b = t // tm
    nt = n // tm
    vec = _const_spec((1, D_MODEL))
    tok = jnp.arange(tm)
    before = ((tok[None, :] < tok[:, None]) & ((tok[None, :] // sub) == (tok[:, None] // sub))).astype(BF16)
    return pl.pallas_call(
        functools.partial(_mid_kernel, sub=sub, win=win),
        grid=(nt,),
        in_specs=[pl.BlockSpec((tm, D_MODEL), lambda i: (i, 0)),
                  pl.BlockSpec((tm, D_ATT), lambda i: (i, 0)),
                  pl.BlockSpec((tm, D_RWKV), lambda i: (i, 0)),
                  pl.BlockSpec((N_MEM, 2 * D_CROSS), lambda i: (i // per_b, 0)),
                  _const_spec((D_MODEL, D_MODEL)), vec, vec,
                  _const_spec((D_MODEL, D_CROSS)), _const_spec((D_CROSS, D_MODEL)), vec, vec,
                  _const_spec((D_MODEL, 2 * ROUTE_LANES)), _const_spec((1, ROUTE_LANES)), _const_spec((tm, tm))],
        out_specs=[pl.BlockSpec((tm, D_MODEL), lambda i: (i, 0)),
                   pl.BlockSpec((tm, ROUTE_LANES), lambda i: (i, 0)),
                   pl.BlockSpec((8, tm), lambda i: (i, 0)),
                   pl.BlockSpec((8, ROUTE_LANES), lambda i: (i, 0))],
        out_shape=[jax.ShapeDtypeStruct((n, D_MODEL), F32),
                   jax.ShapeDtypeStruct((n, ROUTE_LANES), F32),
                   jax.ShapeDtypeStruct((nt * 8, tm), F32),
                   jax.ShapeDtypeStruct((nt * 8, ROUTE_LANES), jnp.int32)],
        compiler_params=_cparams("arbitrary"),
        name="mid",
    )(x2d, att2d, rw2d, kv, p['w_out'], p['ln1_g'], p['ln1_b'], p['w_cq'], p['w_co'], p['ln2_g'], p['ln2_b'],
      p['w_route'], p['b_route'], before)


def _moe_kernel(meta_ref, x_ref, route_ref, routet_ref, wg_ref, wu_ref, wd_ref, l3g_ref, l3b_ref, o_ref,
                xs_ref, ys_ref, *, sub, win, rows):
    i = pl.program_id(0)
    step = pl.program_id(1)
    n_sub = xs_ref.shape[0]
    dummy = rows - win

    @pl.when(step == 0)
    def _():
        ys_ref[...] = jnp.zeros_like(ys_ref)
        r_id = lax.broadcasted_iota(jnp.int32, (rows, sub), 0).astype(F32)
        for s in range(n_sub):
            d1 = routet_ref[2:3, s * sub:(s + 1) * sub]
            d2 = routet_ref[3:4, s * sub:(s + 1) * sub]
            perm = ((r_id == d1) | (r_id == d2)).astype(BF16)
            xb = x_ref[s * sub:(s + 1) * sub, :].astype(BF16)
            xs_ref[s] = jnp.dot(perm, xb, preferred_element_type=F32).astype(BF16)

    base = i * (n_sub * 32)
    for ee in range(EXPERTS_PER_STEP):
        e = step * EXPERTS_PER_STEP + ee
        offs = [meta_ref[base + s * 32 + e] for s in range(n_sub)]
        nbs = [meta_ref[base + s * 32 + N_EXPERTS + e] for s in range(n_sub)]
        nb_max = functools.reduce(jnp.maximum, nbs)

        def block(j, carry, ee=ee, offs=offs, nbs=nbs):
            starts = [pl.multiple_of(jnp.where(j < nbs[s], offs[s] + j * win, dummy), SEG_ALIGN)
                      for s in range(n_sub)]
            xw = jnp.concatenate([xs_ref[s, pl.ds(starts[s], win), :] for s in range(n_sub)], axis=0)
            gate = jnp.dot(xw, wg_ref[ee], preferred_element_type=F32)
            up = jnp.dot(xw, wu_ref[ee], preferred_element_type=F32)
            hdn = (gate * jax.nn.sigmoid(gate)) * up
            ye = jnp.dot(hdn.astype(BF16), wd_ref[ee], preferred_element_type=F32).astype(BF16)
            for s in range(n_sub):
                ys_ref[s, pl.ds(starts[s], win), :] = ye[s * win:(s + 1) * win]
            return carry

        lax.fori_loop(0, nb_max, block, 0)

    @pl.when(step == N_EXPERTS // EXPERTS_PER_STEP - 1)
    def _():
        c_id = lax.broadcasted_iota(jnp.int32, (sub, rows), 1).astype(F32)
        for s in range(n_sub):
            rt = route_ref[s * sub:(s + 1) * sub, :]
            w1, w2, d1, d2 = rt[:, 0:1], rt[:, 1:2], rt[:, 2:3], rt[:, 3:4]
            comb = jnp.where(c_id == d1, w1, 0.0) + jnp.where(c_id == d2, w2, 0.0)
            hi = comb.astype(BF16)
            lo = (comb - hi.astype(F32)).astype(BF16)
            ysv = ys_ref[s]
            ff = (jnp.dot(hi, ysv, preferred_element_type=F32) + jnp.dot(lo, ysv, preferred_element_type=F32))
            xr = x_ref[s * sub:(s + 1) * sub, :]
            o_ref[s * sub:(s + 1) * sub, :] = _ln(DEEPNORM_ALPHA * xr + ff, l3g_ref[...], l3b_ref[...])


def _moe(x2, route, routet, meta, p, tm, sub, win):
    n = x2.shape[0]
    nt = n // tm
    n_sub = tm // sub
    rows = 2 * sub + N_EXPERTS * SEG_ALIGN + 2 * win
    meta_flat = meta.reshape(nt, 8, ROUTE_LANES)[:, :n_sub, :32].reshape(-1)
    vec = pl.BlockSpec((1, D_MODEL), lambda i, e, m: (0, 0))
    grid_spec = pltpu.PrefetchScalarGridSpec(
        num_scalar_prefetch=1,
        grid=(nt, N_EXPERTS // EXPERTS_PER_STEP),
        in_specs=[pl.BlockSpec((tm, D_MODEL), lambda i, e, m: (i, 0)),
                  pl.BlockSpec((tm, ROUTE_LANES), lambda i, e, m: (i, 0)),
                  pl.BlockSpec((8, tm), lambda i, e, m: (i, 0)),
                  pl.BlockSpec((EXPERTS_PER_STEP, D_MODEL, D_EXPERT), lambda i, e, m: (e, 0, 0)),
                  pl.BlockSpec((EXPERTS_PER_STEP, D_MODEL, D_EXPERT), lambda i, e, m: (e, 0, 0)),
                  pl.BlockSpec((EXPERTS_PER_STEP, D_EXPERT, D_MODEL), lambda i, e, m: (e, 0, 0)),
                  vec, vec],
        out_specs=pl.BlockSpec((tm, D_MODEL), lambda i, e, m: (i, 0)),
        scratch_shapes=[pltpu.VMEM((n_sub, rows, D_MODEL), BF16),
                        pltpu.VMEM((n_sub, rows, D_MODEL), BF16)])
    return pl.pallas_call(
        functools.partial(_moe_kernel, sub=sub, win=win, rows=rows),
        grid_spec=grid_spec,
        out_shape=jax.ShapeDtypeStruct((n, D_MODEL), F32),
        compiler_params=_cparams("arbitrary", "arbitrary"),
        name="moe",
    )(meta_flat, x2, route, routet, p['w_gate'], p['w_up'], p['w_down'], p['ln3_g'], p['ln3_b'])


def _prep_params(w_in, tshift_mu, attn_sink, rwkv_w0, rwkv_w_up, rwkv_a0, rwkv_a_up, rwkv_g_up, rwkv_k_k,
                 rwkv_k_a, rwkv_r_k, rwkv_gn_g, rwkv_gn_b, w_out, ln1_g, ln1_b, mem_ln_g, mem_ln_b, w_cq, w_ckv,
                 w_co, ln2_g, ln2_b, w_route_group, b_route_group, w_route_expert, b_route_expert,
                 w_exp_gate, w_exp_up, w_exp_down, ln3_g, ln3_b):
    row = lambda a: a.reshape(1, -1).astype(F32)
    zeros_lora = jnp.zeros((2, W_LORA, D_RWKV), F32)
    hid = jnp.arange(LANES) // HEAD_DIM
    pad = ROUTE_LANES - N_EXPERTS - N_EXPERT_GROUPS
    return {
        'w_in_att': w_in[:, :D_QKV].astype(BF16),
        'w_in_rw': w_in[:, D_QKV:].astype(BF16),
        'sink': attn_sink.astype(F32),
        'mu': tshift_mu.astype(F32),
        'w0': rwkv_w0.reshape(2, 1, D_RWKV).astype(F32),
        'rw_w_up': jnp.concatenate([rwkv_w_up, zeros_lora], axis=1).astype(BF16),
        'a0': rwkv_a0.reshape(2, 1, D_RWKV).astype(F32),
        'rw_a_up': jnp.concatenate([zeros_lora, rwkv_a_up], axis=1).astype(BF16),
        'g_up': rwkv_g_up.astype(BF16),
        'k_k': row(rwkv_k_k), 'k_a': row(rwkv_k_a), 'r_k': row(rwkv_r_k),
        'gn_g': row(rwkv_gn_g), 'gn_b': row(rwkv_gn_b),
        'bd': (hid[:, None] == hid[None, :]).astype(BF16),
        'w_out': w_out.astype(BF16),
        'ln1_g': row(ln1_g), 'ln1_b': row(ln1_b),
        'mem_ln_g': row(mem_ln_g), 'mem_ln_b': row(mem_ln_b),
        'w_cq': w_cq.astype(BF16), 'w_ckv': w_ckv.astype(BF16), 'w_co': w_co.astype(BF16),
        'ln2_g': row(ln2_g), 'ln2_b': row(ln2_b),
        'w_route': _hi_lo(jnp.pad(jnp.concatenate([w_route_expert, w_route_group], axis=1), ((0, 0), (0, pad)))),
        'b_route': jnp.pad(jnp.concatenate([b_route_expert, b_route_group]), (0, pad)).reshape(1, -1).astype(F32),
        'w_gate': w_exp_gate.astype(BF16), 'w_up': w_exp_up.astype(BF16), 'w_down': w_exp_down.astype(BF16),
        'ln3_g': row(ln3_g), 'ln3_b': row(ln3_b),
    }


def _hi_lo(w):
    w = w.astype(F32)
    hi = w.astype(BF16)
    return jnp.concatenate([hi, (w - hi.astype(F32)).astype(BF16)], axis=-1)


def _tile(n, pref):
    t = pref
    while n % t:
        t //= 2
    return t


def _layer(x, mem, p):
    b, t, _ = x.shape
    n = b * t
    x2d = x.reshape(n, D_MODEL)
    tm = _tile(t, 512)
    qkv, zr = _inproj(x2d, p['w_in_att'], p['w_in_rw'], tm)
    att = _attention(qkv.reshape(b, t, D_QKV), p['sink'])
    zr3 = zr.reshape(b, t, D_RWKV_IN)
    lt = _tile(t, 512)
    y_fwd = _rwkv_dir(zr3, None, p, 0, lt)
    rw = _rwkv_dir(zr3, y_fwd, p, 1, lt)
    kv = _memkv(mem.reshape(b * N_MEM, D_MODEL), p['mem_ln_g'], p['mem_ln_b'], p['w_ckv'])
    tme = _tile(t, 1024)
    sub = _tile(tme, 256)
    win = max(SEG_ALIGN, (3 * sub // 16) // SEG_ALIGN * SEG_ALIGN)
    x2, route, routet, meta = _mid(x2d, att.reshape(n, D_ATT), rw.reshape(n, D_RWKV), kv, p, tme, t, sub, win)
    y = _moe(x2, route, routet, meta, p, tme, sub, win)
    return y.reshape(b, t, D_MODEL)


def kernel(x_prompt, x_sample, mem_prompt, mem_sample, w_in, tshift_mu, attn_sink, rwkv_w0, rwkv_w_up, rwkv_a0, rwkv_a_up, rwkv_g_up, rwkv_k_k, rwkv_k_a, rwkv_r_k, rwkv_gn_g, rwkv_gn_b, w_out, ln1_g, ln1_b, mem_ln_g, mem_ln_b, w_cq, w_ckv, w_co, ln2_g, ln2_b, w_route_group, b_route_group, w_route_expert, b_route_expert, w_exp_gate, w_exp_up, w_exp_down, ln3_g, ln3_b):
    weights = (w_in, tshift_mu, attn_sink, rwkv_w0, rwkv_w_up, rwkv_a0, rwkv_a_up, rwkv_g_up, rwkv_k_k, rwkv_k_a,
               rwkv_r_k, rwkv_gn_g, rwkv_gn_b, w_out, ln1_g, ln1_b, mem_ln_g, mem_ln_b, w_cq, w_ckv, w_co,
               ln2_g, ln2_b, w_route_group, b_route_group, w_route_expert, b_route_expert,
               w_exp_gate, w_exp_up, w_exp_down, ln3_g, ln3_b)
    p = _prep_params(*[w[0] for w in weights])
    return (_layer(x_prompt, mem_prompt, p), _layer(x_sample, mem_sample, p))
```

```python
import functools
import math

import jax
import jax.numpy as jnp
from jax import lax
from jax.experimental import pallas as pl
from jax.experimental.pallas import tpu as pltpu

F32 = jnp.float32
BF16 = jnp.bfloat16
HI = lax.Precision.HIGHEST

D_MODEL = 1024
HEAD_DIM = 64
ATT_Q_HEADS = 8
ATT_KV_HEADS = 2
ATT_GROUP = ATT_Q_HEADS // ATT_KV_HEADS
WINDOW = 128
BLOCK = 128
RWKV_HEADS = 8
D_ATT = ATT_Q_HEADS * HEAD_DIM
D_KV = ATT_KV_HEADS * HEAD_DIM
D_QKV = D_ATT + 2 * D_KV
D_RWKV = RWKV_HEADS * HEAD_DIM
W_LORA = 64
A_LORA = 64
G_LORA = 128
D_RWKV_IN = 3 * D_RWKV + W_LORA + A_LORA + G_LORA
N_MEM = 256
CROSS_HEADS = 4
CROSS_HEAD_DIM = 128
D_CROSS = CROSS_HEADS * CROSS_HEAD_DIM
N_EXPERT_GROUPS = 4
EXPERTS_PER_GROUP = 4
N_EXPERTS = N_EXPERT_GROUPS * EXPERTS_PER_GROUP
D_EXPERT = 512
LN_EPS = 1e-5
GN_EPS = 64e-5
DEEPNORM_ALPHA = 2.0 ** 0.25
NEG_INF = -1e30
LANES = 128
ROUTE_LANES = LANES
DECAY_SCALE = math.exp(-0.5)
CHUNK = 64
SEG_ALIGN = 16
EXPERTS_PER_STEP = 2
VMEM_LIMIT = 56 * 1024 * 1024


def _cparams(*sem):
    return pltpu.CompilerParams(dimension_semantics=sem, vmem_limit_bytes=VMEM_LIMIT)


def _ln(x, g, b):
    mu = jnp.mean(x, -1, keepdims=True)
    xc = x - mu
    var = jnp.mean(xc * xc, -1, keepdims=True)
    return xc * lax.rsqrt(var + LN_EPS) * g + b


def _const_spec(shape):
    nd = len(shape)
    return pl.BlockSpec(shape, lambda *_: (0,) * nd)


def _inproj_kernel(x_ref, wa_ref, wr_ref, qkv_ref, zr_ref):
    xb = x_ref[...].astype(BF16)
    qkv_ref[...] = jnp.dot(xb, wa_ref[...], preferred_element_type=F32).astype(BF16)
    zr_ref[...] = jnp.dot(xb, wr_ref[...], preferred_element_type=F32)


def _inproj(x2d, w_att, w_rw, tm):
    n = x2d.shape[0]
    return pl.pallas_call(
        _inproj_kernel,
        grid=(n // tm,),
        in_specs=[pl.BlockSpec((tm, D_MODEL), lambda i: (i, 0)),
                  _const_spec((D_MODEL, D_QKV)),
                  _const_spec((D_MODEL, D_RWKV_IN))],
        out_specs=[pl.BlockSpec((tm, D_QKV), lambda i: (i, 0)),
                   pl.BlockSpec((tm, D_RWKV_IN), lambda i: (i, 0))],
        out_shape=[jax.ShapeDtypeStruct((n, D_QKV), BF16),
                   jax.ShapeDtypeStruct((n, D_RWKV_IN), F32)],
        compiler_params=_cparams("arbitrary"),
        name="inproj",
    )(x2d, w_att, w_rw)


def _attn_kernel(sink_ref, cur_ref, prv_ref, nxt_ref, o_ref, *, ns, qb):
    n = pl.program_id(1)
    blocks = [prv_ref[0]] + [cur_ref[0, a * BLOCK:(a + 1) * BLOCK, :] for a in range(qb)] + [nxt_ref[0]]
    rows = ATT_GROUP * BLOCK
    ri = lax.broadcasted_iota(jnp.int32, (rows, 3 * BLOCK), 0)
    ki = lax.broadcasted_iota(jnp.int32, (rows, 3 * BLOCK), 1)
    adist = jnp.abs((ri & (BLOCK - 1)) - ki + BLOCK)
    band = adist <= WINDOW
    adist_f = adist.astype(F32)
    grp_id = lax.broadcasted_iota(jnp.int32, (rows, 1), 0) // BLOCK
    biases, sinks = [], []
    for h in range(ATT_KV_HEADS):
        slope = jnp.zeros((rows, 1), F32)
        sink = jnp.zeros((rows, 1), F32)
        for g in range(ATT_GROUP):
            hq = h * ATT_GROUP + g
            slope = jnp.where(grp_id == g, 2.0 ** (-8.0 / ATT_Q_HEADS * (hq + 1)), slope)
            sink = jnp.where(grp_id == g, sink_ref[hq], sink)
        biases.append(slope * adist_f)
        sinks.append(sink)
    for a in range(qb):
        valid = band
        if a == 0:
            valid = valid & ((ki >= BLOCK) | (n > 0))
        if a == qb - 1:
            valid = valid & ((ki < 2 * BLOCK) | (n < ns - 1))
        cur = blocks[a + 1]
        for h in range(ATT_KV_HEADS):
            k0 = D_ATT + h * HEAD_DIM
            v0 = D_ATT + D_KV + h * HEAD_DIM
            kcat = jnp.concatenate([blk[:, k0:k0 + HEAD_DIM] for blk in blocks[a:a + 3]], axis=0)
            vcat = jnp.concatenate([blk[:, v0:v0 + HEAD_DIM] for blk in blocks[a:a + 3]], axis=0)
            q0 = h * ATT_GROUP * HEAD_DIM
            q = jnp.concatenate([cur[:, q0 + g * HEAD_DIM:q0 + (g + 1) * HEAD_DIM] for g in range(ATT_GROUP)], axis=0)
            q = q * (HEAD_DIM ** -0.5)
            s = lax.dot_general(q, kcat, (((1,), (1,)), ((), ())), preferred_element_type=F32)
            s = jnp.where(valid, s - biases[h], NEG_INF)
            sk = sinks[h]
            m = jnp.maximum(jnp.max(s, -1, keepdims=True), sk)
            p = jnp.exp(s - m)
            den = jnp.sum(p, -1, keepdims=True) + jnp.exp(sk - m)
            o = (jnp.dot(p.astype(BF16), vcat, preferred_element_type=F32) / den).astype(BF16)
            for g in range(ATT_GROUP):
                o_ref[0, a * BLOCK:(a + 1) * BLOCK, q0 + g * HEAD_DIM:q0 + (g + 1) * HEAD_DIM] = (
                    o[g * BLOCK:(g + 1) * BLOCK])


def _attention(qkv, sink):
    b, t, _ = qkv.shape
    nb = t // BLOCK
    qb = _tile(nb, 4)
    ns = nb // qb
    return pl.pallas_call(
        functools.partial(_attn_kernel, ns=ns, qb=qb),
        grid=(b, ns),
        in_specs=[pl.BlockSpec(memory_space=pltpu.SMEM),
                  pl.BlockSpec((1, qb * BLOCK, D_QKV), lambda i, j: (i, j, 0)),
                  pl.BlockSpec((1, BLOCK, D_QKV), lambda i, j: (i, jnp.maximum(j * qb - 1, 0), 0)),
                  pl.BlockSpec((1, BLOCK, D_QKV), lambda i, j: (i, jnp.minimum((j + 1) * qb, nb - 1), 0))],
        out_specs=pl.BlockSpec((1, qb * BLOCK, D_ATT), lambda i, j: (i, j, 0)),
        out_shape=jax.ShapeDtypeStruct((b, t, D_ATT), BF16),
        compiler_params=_cparams("arbitrary", "arbitrary"),
        name="win_attn",
    )(sink, qkv, qkv, qkv)


def _bmm(a, b):
    return jnp.einsum('hlj,hjm->hlm', a.astype(BF16), b.astype(BF16), preferred_element_type=F32)


def _split3(x):
    hi = x.astype(BF16)
    r1 = x - hi.astype(F32)
    mid = r1.astype(BF16)
    lo = (r1 - mid.astype(F32)).astype(BF16)
    return hi, mid, lo


def _rwkv_kernel(*refs, reverse, lt, nt, final, grp):
    if final:
        (zc_ref, zp_ref, zn_ref, mu_ref, w0_ref, wup_ref, a0_ref, aup_ref, gup_ref, kk_ref, ka_ref, rk_ref,
         gng_ref, gnb_ref, bd_ref, yf_ref, out_ref,
         st_ref, y_scr, r_s, kd_s, v_s, a_s, b_s, lw_s, bonus_s, gate_s) = refs
    else:
        (zc_ref, zp_ref, zn_ref, mu_ref, w0_ref, wup_ref, a0_ref, aup_ref, kk_ref, ka_ref, bd_ref,
         out_ref, st_ref, y_scr, r_s, kd_s, v_s, a_s, b_s, lw_s) = refs
    i = pl.program_id(1)
    tt = (nt - 1 - i) if reverse else i

    @pl.when(i == 0)
    def _():
        st_ref[...] = jnp.zeros_like(st_ref)

    bd = bd_ref[...]

    def head_sum(x):
        hi = x.astype(BF16)
        lo = (x - hi.astype(F32)).astype(BF16)
        tiles = []
        for t in range(D_RWKV // LANES):
            sl = slice(t * LANES, (t + 1) * LANES)
            tiles.append(jnp.dot(hi[:, sl], bd, preferred_element_type=F32)
                         + jnp.dot(lo[:, sl], bd, preferred_element_type=F32))
        return jnp.concatenate(tiles, axis=-1)

    z = zc_ref[0]
    row = lax.broadcasted_iota(jnp.int32, (lt, 1), 0)
    zprev_edge = jnp.where(tt > 0, zp_ref[0, 7:8, :], 0.0)
    znext_edge = jnp.where(tt < nt - 1, zn_ref[0, 0:1, :], 0.0)
    prev = jnp.where(row == 0, zprev_edge, pltpu.roll(z, 1, 0))
    nxt = jnp.where(row == lt - 1, znext_edge, pltpu.roll(z, lt - 1, 0))
    zs = z + mu_ref[0:1, :] * (prev - z) + mu_ref[1:2, :] * (nxt - z)

    o1, o2, o3 = D_RWKV, 2 * D_RWKV, 3 * D_RWKV
    r = zs[:, :o1]
    k = zs[:, o1:o2]
    v = zs[:, o2:o3]
    wad = zs[:, o3:o3 + W_LORA + A_LORA]
    kk = k * kk_ref[...]
    kk = kk * lax.rsqrt(head_sum(kk * kk) + 1e-12)
    w_arg = w0_ref[...] + jnp.dot(jnp.tanh(wad).astype(BF16), wup_ref[...], preferred_element_type=F32)
    a_sig = jax.nn.sigmoid(a0_ref[...] + jnp.dot(wad.astype(BF16), aup_ref[...], preferred_element_type=F32))
    r_s[...] = r
    v_s[...] = v
    kd_s[...] = k * (1.0 + (a_sig - 1.0) * ka_ref[...])
    a_s[...] = -kk
    b_s[...] = kk * a_sig
    lw_s[...] = (-DECAY_SCALE) * jax.nn.sigmoid(w_arg)
    if final:
        bonus_s[...] = head_sum(r * k * rk_ref[...]) * v
        gd = zs[:, o3 + W_LORA + A_LORA:]
        gate_s[...] = jnp.dot(jax.nn.sigmoid(gd).astype(BF16), gup_ref[...], preferred_element_type=F32)

    span = grp * CHUNK
    pw_ = 2 * HEAD_DIM
    n_pair = RWKV_HEADS // 2
    ri = lax.broadcasted_iota(jnp.int32, (2 * CHUNK, 2 * pw_), 0)
    rj = lax.broadcasted_iota(jnp.int32, (2 * CHUNK, 2 * pw_), 1)
    rt, jt = ri & (CHUNK - 1), rj & (CHUNK - 1)
    if reverse:
        strict, incl = rt < jt, rt <= jt
    else:
        strict, incl = rt > jt, rt >= jt
    pmask = strict | (incl & (ri >= CHUNK))
    oi = lax.broadcasted_iota(jnp.int32, (16, grp * pw_), 0)
    oj = lax.broadcasted_iota(jnp.int32, (16, grp * pw_), 1)
    spread = (oi == (oj // pw_)).astype(BF16)
    ci = lax.broadcasted_iota(jnp.int32, (CHUNK, CHUNK), 0)
    cj = lax.broadcasted_iota(jnp.int32, (CHUNK, CHUNK), 1)
    tri = ((ci <= cj) if reverse else (ci >= cj)).astype(BF16)
    hi_ = lax.broadcasted_iota(jnp.int32, (CHUNK, pw_), 0)
    hj_ = lax.broadcasted_iota(jnp.int32, (CHUNK, pw_), 1)
    left = hj_ < HEAD_DIM
    eye_h = (hi_ == (hj_ & (HEAD_DIM - 1))).astype(F32)
    bi = lax.broadcasted_iota(jnp.int32, (pw_, pw_), 0)
    bj = lax.broadcasted_iota(jnp.int32, (pw_, pw_), 1)
    bdmask = (bi // HEAD_DIM) == (bj // HEAD_DIM)
    nc = lt // span

    def pairs(x):
        return jnp.stack([x[g * CHUNK:(g + 1) * CHUNK, p * pw_:(p + 1) * pw_]
                          for g in range(grp) for p in range(n_pair)], axis=0)

    def stack_mask(x):
        zero = jnp.zeros_like(x)
        return jnp.concatenate([jnp.where(left, x, zero), jnp.where(left, zero, x)], axis=1)

    def chunk(j, carry):
        c = (nc - 1 - j) if reverse else j
        base = pl.multiple_of(c * span, span)
        rows = pl.ds(base, span)
        lw_c = lw_s[rows, :]
        parts = _split3(lw_c)
        cum = jnp.concatenate(
            [sum(jnp.dot(tri, q[g * CHUNK:(g + 1) * CHUNK], preferred_element_type=F32) for q in parts)
             for g in range(grp)], axis=0)
        g_rows = jnp.exp(cum.reshape(grp, CHUNK, D_RWKV)[:, (0 if reverse else CHUNK - 1), :])
        if spread.shape[0] > grp:
            g_rows = jnp.concatenate([g_rows, jnp.zeros((spread.shape[0] - grp, D_RWKV), F32)], axis=0)
        g_all = sum(lax.dot_general(q, spread, (((0,), (0,)), ((), ())), preferred_element_type=F32)
                    for q in _split3(g_rows)[:2]).reshape(n_pair, pw_, grp * pw_)
        e_pos = jnp.exp(cum)
        e_neg = jnp.exp(-cum)
        e_prev = jnp.exp(cum - lw_c)
        a_p = pairs((a_s[rows, :] * e_prev).astype(BF16))
        r_p = pairs((r_s[rows, :] * e_pos).astype(BF16))
        b_p = pairs((b_s[rows, :] * e_neg).astype(BF16))
        k_p = pairs((kd_s[rows, :] * e_neg).astype(BF16))
        v_p = pairs(v_s[rows, :].astype(BF16))
        ar = jnp.concatenate([a_p, r_p], axis=1)
        bk = jnp.concatenate([b_p, k_p], axis=1)
        bkm = jnp.concatenate([stack_mask(b_p), stack_mask(k_p)], axis=1)
        pw = jnp.einsum('blc,bjc->blj', ar, bkm, preferred_element_type=F32)
        pw = jnp.where(pmask, pw, 0.0)
        pw_b = pw.astype(BF16)
        m_ab = pw[:, :CHUNK, :pw_]
        tinv = eye_h + m_ab
        mp = m_ab
        for _ in range(5):
            mp = _bmm(mp, stack_mask(mp.astype(BF16)))
            tinv = tinv + _bmm(tinv, stack_mask(mp.astype(BF16)))
        tinv_b = tinv.astype(BF16)
        vm = stack_mask(v_p)
        mv = _bmm(pw_b[:, :CHUNK, pw_:], vm)

        st = st_ref[...]
        for q in (range(grp - 1, -1, -1) if reverse else range(grp)):
            hs = slice(q * n_pair, (q + 1) * n_pair)
            ar_st = _bmm(ar[hs], st)
            u = _bmm(tinv_b[hs], stack_mask((ar_st[:, :CHUNK] + mv[hs]).astype(BF16)))
            u_b = u.astype(BF16)
            y = ar_st[:, CHUNK:] + _bmm(pw_b[hs][:, CHUNK:, :], jnp.concatenate([stack_mask(u_b), vm[hs]], axis=1))
            upd = jnp.einsum('bjc,bjv->bcv', bk[hs], jnp.concatenate([u_b, v_p[hs]], axis=1),
                             preferred_element_type=F32)
            g_col = g_all[:, :, q * pw_:(q + 1) * pw_]
            st = g_col * (st + jnp.where(bdmask, upd, 0.0))
            for p in range(n_pair):
                y_scr[pl.ds(base + q * CHUNK, CHUNK), p * pw_:(p + 1) * pw_] = y[p]
        st_ref[...] = st
        return carry

    lax.fori_loop(0, nc, chunk, 0)

    if not final:
        out_ref[0] = y_scr[...]
    else:
        yy = yf_ref[0] + y_scr[...]
        mu_y = head_sum(yy) * (1.0 / HEAD_DIM)
        yc = yy - mu_y
        var_y = head_sum(yc * yc) * (1.0 / HEAD_DIM)
        yn = yc * lax.rsqrt(var_y + GN_EPS) * gng_ref[...] + gnb_ref[...]
        out_ref[0] = ((yn + bonus_s[...]) * gate_s[...]).astype(BF16)


def _rwkv_dir(zr, y_fwd, p, d, lt):
    b, t, _ = zr.shape
    nt = t // lt
    reverse = d == 1
    final = y_fwd is not None
    tmap = (lambda j: nt - 1 - j) if reverse else (lambda j: j)
    r8 = lt // 8
    vec = _const_spec((1, D_RWKV))
    in_specs = [pl.BlockSpec((1, lt, D_RWKV_IN), lambda i, j: (i, tmap(j), 0)),
                pl.BlockSpec((1, 8, D_RWKV_IN), lambda i, j: (i, jnp.maximum(tmap(j) * r8 - 1, 0), 0)),
                pl.BlockSpec((1, 8, D_RWKV_IN), lambda i, j: (i, jnp.minimum((tmap(j) + 1) * r8, t // 8 - 1), 0)),
                _const_spec((2, D_RWKV_IN)), vec, _const_spec((W_LORA + A_LORA, D_RWKV)), vec,
                _const_spec((W_LORA + A_LORA, D_RWKV))]
    args = [zr, zr, zr, p['mu'], p['w0'][d], p['rw_w_up'][d], p['a0'][d], p['rw_a_up'][d]]
    n_tok_scratch = 6
    if final:
        in_specs += [_const_spec((G_LORA, D_RWKV)), vec, vec, vec, vec, vec, _const_spec((LANES, LANES)),
                     pl.BlockSpec((1, lt, D_RWKV), lambda i, j: (i, tmap(j), 0))]
        args += [p['g_up'], p['k_k'], p['k_a'], p['r_k'], p['gn_g'], p['gn_b'], p['bd'], y_fwd]
        n_tok_scratch = 8
    else:
        in_specs += [vec, vec, _const_spec((LANES, LANES))]
        args += [p['k_k'], p['k_a'], p['bd']]
    return pl.pallas_call(
        functools.partial(_rwkv_kernel, reverse=reverse, lt=lt, nt=nt, final=final, grp=min(8, lt // CHUNK)),
        grid=(b, nt),
        in_specs=in_specs,
        out_specs=pl.BlockSpec((1, lt, D_RWKV), lambda i, j: (i, tmap(j), 0)),
        out_shape=jax.ShapeDtypeStruct((b, t, D_RWKV), BF16 if final else F32),
        scratch_shapes=[pltpu.VMEM((RWKV_HEADS // 2, 2 * HEAD_DIM, 2 * HEAD_DIM), F32)]
                       + [pltpu.VMEM((lt, D_RWKV), F32)] * (1 + n_tok_scratch),
        compiler_params=_cparams("arbitrary", "arbitrary"),
        name="rwkv_bwd_final" if final else "rwkv_fwd",
    )(*args)


def _memkv_kernel(m_ref, g_ref, b_ref, w_ref, kv_ref):
    m = _ln(m_ref[...], g_ref[...], b_ref[...])
    kv_ref[...] = jnp.dot(m.astype(BF16), w_ref[...], preferred_element_type=F32).astype(BF16)


def _memkv(mem2d, g, b, w_ckv):
    n = mem2d.shape[0]
    return pl.pallas_call(
        _memkv_kernel,
        grid=(n // N_MEM,),
        in_specs=[pl.BlockSpec((N_MEM, D_MODEL), lambda i: (i, 0)),
                  _const_spec((1, D_MODEL)), _const_spec((1, D_MODEL)),
                  _const_spec((D_MODEL, 2 * D_CROSS))],
        out_specs=pl.BlockSpec((N_MEM, 2 * D_CROSS), lambda i: (i, 0)),
        out_shape=jax.ShapeDtypeStruct((n, 2 * D_CROSS), BF16),
        compiler_params=_cparams("arbitrary"),
        name="mem_kv",
    )(mem2d, g, b, w_ckv)


def _mid_kernel(x_ref, att_ref, rw_ref, kv_ref, wo_ref, l1g_ref, l1b_ref, wq_ref, wco_ref, l2g_ref, l2b_ref,
                wr_ref, br_ref, before_ref, x2_ref, route_ref, routet_ref, meta_ref, *, sub, win):
    mix = (jnp.dot(att_ref[...], wo_ref[:D_ATT, :], preferred_element_type=F32)
           + jnp.dot(rw_ref[...], wo_ref[D_ATT:, :], preferred_element_type=F32))
    x1 = _ln(DEEPNORM_ALPHA * x_ref[...] + mix, l1g_ref[...], l1b_ref[...])

    q = jnp.dot(x1.astype(BF16), wq_ref[...], preferred_element_type=F32)
    q = (q * (CROSS_HEAD_DIM ** -0.5)).astype(BF16)
    kv = kv_ref[...]
    cr = None
    for h in range(CROSS_HEADS):
        hs = slice(h * CROSS_HEAD_DIM, (h + 1) * CROSS_HEAD_DIM)
        kh = kv[:, h * CROSS_HEAD_DIM:(h + 1) * CROSS_HEAD_DIM]
        vh = kv[:, D_CROSS + h * CROSS_HEAD_DIM:D_CROSS + (h + 1) * CROSS_HEAD_DIM]
        s = lax.dot_general(q[:, hs], kh, (((1,), (1,)), ((), ())), preferred_element_type=F32)
        m = jnp.max(s, -1, keepdims=True)
        p = jnp.exp(s - m)
        den = jnp.sum(p, -1, keepdims=True)
        o = jnp.dot(p.astype(BF16), vh, preferred_element_type=F32) / den
        part = jnp.dot(o.astype(BF16), wco_ref[hs, :], preferred_element_type=F32)
        cr = part if cr is None else cr + part
    x2 = _ln(DEEPNORM_ALPHA * x1 + cr, l2g_ref[...], l2b_ref[...])
    x2_ref[...] = x2

    x2_hi = x2.astype(BF16)
    x2_lo = (x2 - x2_hi.astype(F32)).astype(BF16)
    both = jnp.dot(x2_hi, wr_ref[...], preferred_element_type=F32)
    logits = (both[:, :ROUTE_LANES] + both[:, ROUTE_LANES:]
              + jnp.dot(x2_lo, wr_ref[:, :ROUTE_LANES], preferred_element_type=F32)) + br_ref[...]
    lane = lax.broadcasted_iota(jnp.int32, logits.shape, 1)
    is_g = (lane >= N_EXPERTS) & (lane < N_EXPERTS + N_EXPERT_GROUPS)
    lg = jnp.where(is_g, logits, NEG_INF)
    eg = jnp.where(is_g, jnp.exp(lg - jnp.max(lg, -1, keepdims=True)), 0.0)
    pg = eg / jnp.sum(eg, -1, keepdims=True)
    pg_top = jnp.max(pg, -1, keepdims=True)
    gi = jnp.min(jnp.where(is_g & (pg == pg_top), lane, 4 * ROUTE_LANES), -1, keepdims=True) - N_EXPERTS
    in_grp = (lane >= gi * EXPERTS_PER_GROUP) & (lane < (gi + 1) * EXPERTS_PER_GROUP)
    le = jnp.where(in_grp, logits, NEG_INF)
    ee = jnp.where(in_grp, jnp.exp(le - jnp.max(le, -1, keepdims=True)), 0.0)
    pe = ee / jnp.sum(ee, -1, keepdims=True)
    p1 = jnp.max(jnp.where(in_grp, pe, -1.0), -1, keepdims=True)
    i1 = jnp.min(jnp.where(in_grp & (pe == p1), lane, 4 * ROUTE_LANES), -1, keepdims=True)
    rest = in_grp & (lane != i1)
    p2 = jnp.max(jnp.where(rest, pe, -1.0), -1, keepdims=True)
    i2 = jnp.min(jnp.where(rest & (pe == p2), lane, 4 * ROUTE_LANES), -1, keepdims=True)
    tot = p1 + p2
    w1 = p1 / tot * pg_top
    w2 = p2 / tot * pg_top

    tm = logits.shape[0]
    n_sub = tm // sub
    e_lane = lane & (N_EXPERTS - 1)
    sel = ((e_lane == i1) | (e_lane == i2)) & (lane < 2 * N_EXPERTS)
    sel_b = sel.astype(BF16)
    rank = jnp.dot(before_ref[...], sel_b, preferred_element_type=F32)
    si = lax.broadcasted_iota(jnp.int32, (8, tm), 0)
    sj = lax.broadcasted_iota(jnp.int32, (8, tm), 1)
    cnt = jnp.dot(((sj // sub) == si).astype(BF16), sel_b, preferred_element_type=F32)
    padded = jnp.floor((cnt + (SEG_ALIGN - 1)) * (1.0 / SEG_ALIGN)) * SEG_ALIGN
    ui = lax.broadcasted_iota(jnp.int32, (ROUTE_LANES, ROUTE_LANES), 0)
    uj = lax.broadcasted_iota(jnp.int32, (ROUTE_LANES, ROUTE_LANES), 1)
    excl = ((ui < uj) & (uj < N_EXPERTS)).astype(F32)
    lane8 = lax.broadcasted_iota(jnp.int32, (8, ROUTE_LANES), 1)
    off = jnp.dot(jnp.where(lane8 < N_EXPERTS, padded, 0.0), excl, precision=HI, preferred_element_type=F32)
    nblk = jnp.floor((cnt + (win - 0.5)) * (1.0 / win))
    meta_ref[...] = jnp.where(lane8 < N_EXPERTS, off, nblk).astype(jnp.int32)
    sub_id = lax.broadcasted_iota(jnp.int32, (tm, 1), 0) // sub
    off_tok = jnp.zeros_like(rank)
    for s in range(n_sub):
        off_tok = jnp.where(sub_id == s, off[s:s + 1, :], off_tok)
    dest = off_tok + rank
    d1 = jnp.sum(jnp.where(lane == i1, dest, 0.0), -1, keepdims=True)
    d2 = jnp.sum(jnp.where(lane == i2, dest, 0.0), -1, keepdims=True)
    route = (jnp.where(lane == 0, w1, 0.0) + jnp.where(lane == 1, w2, 0.0)
             + jnp.where(lane == 2, d1, 0.0) + jnp.where(lane == 3, d2, 0.0))
    route_ref[...] = route
    pick = (lane8 == lax.broadcasted_iota(jnp.int32, (8, ROUTE_LANES), 0)).astype(F32)
    routet_ref[...] = lax.dot_general(pick, route, (((1,), (1,)), ((), ())), precision=HI,
                                      preferred_element_type=F32)


def _mid(x2d, att2d, rw2d, kv, p, tm, t, sub, win):
    n = x2d.shape[0]
    per_b = t // tm
    nt = n // tm
    vec = _const_spec((1, D_MODEL))
    tok = jnp.arange(tm)
    before = ((tok[None, :] < tok[:, None]) & ((tok[None, :] // sub) == (tok[:, None] // sub))).astype(BF16)
    return pl.pallas_call(
        functools.partial(_mid_kernel, sub=sub, win=win),
        grid=(nt,),
        in_specs=[pl.BlockSpec((tm, D_MODEL), lambda i: (i, 0)),
                  pl.BlockSpec((tm, D_ATT), lambda i: (i, 0)),
                  pl.BlockSpec((tm, D_RWKV), lambda i: (i, 0)),
                  pl.BlockSpec((N_MEM, 2 * D_CROSS), lambda i: (i // per_b, 0)),
                  _const_spec((D_MODEL, D_MODEL)), vec, vec,
                  _const_spec((D_MODEL, D_CROSS)), _const_spec((D_CROSS, D_MODEL)), vec, vec,
                  _const_spec((D_MODEL, 2 * ROUTE_LANES)), _const_spec((1, ROUTE_LANES)), _const_spec((tm, tm))],
        out_specs=[pl.BlockSpec((tm, D_MODEL), lambda i: (i, 0)),
                   pl.BlockSpec((tm, ROUTE_LANES), lambda i: (i, 0)),
                   pl.BlockSpec((8, tm), lambda i: (i, 0)),
                   pl.BlockSpec((8, ROUTE_LANES), lambda i: (i, 0))],
        out_shape=[jax.ShapeDtypeStruct((n, D_MODEL), F32),
                   jax.ShapeDtypeStruct((n, ROUTE_LANES), F32),
                   jax.ShapeDtypeStruct((nt * 8, tm), F32),
                   jax.ShapeDtypeStruct((nt * 8, ROUTE_LANES), jnp.int32)],
        compiler_params=_cparams("arbitrary"),
        name="mid",
    )(x2d, att2d, rw2d, kv, p['w_out'], p['ln1_g'], p['ln1_b'], p['w_cq'], p['w_co'], p['ln2_g'], p['ln2_b'],
      p['w_route'], p['b_route'], before)


def _moe_kernel(meta_ref, x_ref, route_ref, routet_ref, wg_ref, wu_ref, wd_ref, l3g_ref, l3b_ref, o_ref,
                xs_ref, ys_ref, *, sub, win, rows):
    i = pl.program_id(0)
    step = pl.program_id(1)
    n_sub = xs_ref.shape[0]
    dummy = rows - win

    @pl.when(step == 0)
    def _():
        ys_ref[...] = jnp.zeros_like(ys_ref)
        r_id = lax.broadcasted_iota(jnp.int32, (rows, sub), 0).astype(F32)
        for s in range(n_sub):
            d1 = routet_ref[2:3, s * sub:(s + 1) * sub]
            d2 = routet_ref[3:4, s * sub:(s + 1) * sub]
            perm = ((r_id == d1) | (r_id == d2)).astype(BF16)
            xb = x_ref[s * sub:(s + 1) * sub, :].astype(BF16)
            xs_ref[s] = jnp.dot(perm, xb, preferred_element_type=F32).astype(BF16)

    base = i * (n_sub * 32)
    for ee in range(EXPERTS_PER_STEP):
        e = step * EXPERTS_PER_STEP + ee
        offs = [meta_ref[base + s * 32 + e] for s in range(n_sub)]
        nbs = [meta_ref[base + s * 32 + N_EXPERTS + e] for s in range(n_sub)]
        nb_max = functools.reduce(jnp.maximum, nbs)

        def block(j, carry, ee=ee, offs=offs, nbs=nbs):
            starts = [pl.multiple_of(jnp.where(j < nbs[s], offs[s] + j * win, dummy), SEG_ALIGN)
                      for s in range(n_sub)]
            xw = jnp.concatenate([xs_ref[s, pl.ds(starts[s], win), :] for s in range(n_sub)], axis=0)
            gate = jnp.dot(xw, wg_ref[ee], preferred_element_type=F32)
            up = jnp.dot(xw, wu_ref[ee], preferred_element_type=F32)
            hdn = (gate * jax.nn.sigmoid(gate)) * up
            ye = jnp.dot(hdn.astype(BF16), wd_ref[ee], preferred_element_type=F32).astype(BF16)
            for s in range(n_sub):
                ys_ref[s, pl.ds(starts[s], win), :] = ye[s * win:(s + 1) * win]
            return carry

        lax.fori_loop(0, nb_max, block, 0)

    @pl.when(step == N_EXPERTS // EXPERTS_PER_STEP - 1)
    def _():
        c_id = lax.broadcasted_iota(jnp.int32, (sub, rows), 1).astype(F32)
        for s in range(n_sub):
            rt = route_ref[s * sub:(s + 1) * sub, :]
            w1, w2, d1, d2 = rt[:, 0:1], rt[:, 1:2], rt[:, 2:3], rt[:, 3:4]
            comb = jnp.where(c_id == d1, w1, 0.0) + jnp.where(c_id == d2, w2, 0.0)
            hi = comb.astype(BF16)
            lo = (comb - hi.astype(F32)).astype(BF16)
            ysv = ys_ref[s]
            ff = (jnp.dot(hi, ysv, preferred_element_type=F32) + jnp.dot(lo, ysv, preferred_element_type=F32))
            xr = x_ref[s * sub:(s + 1) * sub, :]
            o_ref[s * sub:(s + 1) * sub, :] = _ln(DEEPNORM_ALPHA * xr + ff, l3g_ref[...], l3b_ref[...])


def _moe(x2, route, routet, meta, p, tm, sub, win):
    n = x2.shape[0]
    nt = n // tm
    n_sub = tm // sub
    rows = 2 * sub + N_EXPERTS * SEG_ALIGN + 2 * win
    meta_flat = meta.reshape(nt, 8, ROUTE_LANES)[:, :n_sub, :32].reshape(-1)
    vec = pl.BlockSpec((1, D_MODEL), lambda i, e, m: (0, 0))
    grid_spec = pltpu.PrefetchScalarGridSpec(
        num_scalar_prefetch=1,
        grid=(nt, N_EXPERTS // EXPERTS_PER_STEP),
        in_specs=[pl.BlockSpec((tm, D_MODEL), lambda i, e, m: (i, 0)),
                  pl.BlockSpec((tm, ROUTE_LANES), lambda i, e, m: (i, 0)),
                  pl.BlockSpec((8, tm), lambda i, e, m: (i, 0)),
                  pl.BlockSpec((EXPERTS_PER_STEP, D_MODEL, D_EXPERT), lambda i, e, m: (e, 0, 0)),
                  pl.BlockSpec((EXPERTS_PER_STEP, D_MODEL, D_EXPERT), lambda i, e, m: (e, 0, 0)),
                  pl.BlockSpec((EXPERTS_PER_STEP, D_EXPERT, D_MODEL), lambda i, e, m: (e, 0, 0)),
                  vec, vec],
        out_specs=pl.BlockSpec((tm, D_MODEL), lambda i, e, m: (i, 0)),
        scratch_shapes=[pltpu.VMEM((n_sub, rows, D_MODEL), BF16),
                        pltpu.VMEM((n_sub, rows, D_MODEL), BF16)])
    return pl.pallas_call(
        functools.partial(_moe_kernel, sub=sub, win=win, rows=rows),
        grid_spec=grid_spec,
        out_shape=jax.ShapeDtypeStruct((n, D_MODEL), F32),
        compiler_params=_cparams("arbitrary", "arbitrary"),
        name="moe",
    )(meta_flat, x2, route, routet, p['w_gate'], p['w_up'], p['w_down'], p['ln3_g'], p['ln3_b'])


def _prep_params(w_in, tshift_mu, attn_sink, rwkv_w0, rwkv_w_up, rwkv_a0, rwkv_a_up, rwkv_g_up, rwkv_k_k,
                 rwkv_k_a, rwkv_r_k, rwkv_gn_g, rwkv_gn_b, w_out, ln1_g, ln1_b, mem_ln_g, mem_ln_b, w_cq, w_ckv,
                 w_co, ln2_g, ln2_b, w_route_group, b_route_group, w_route_expert, b_route_expert,
                 w_exp_gate, w_exp_up, w_exp_down, ln3_g, ln3_b):
    row = lambda a: a.reshape(1, -1).astype(F32)
    zeros_lora = jnp.zeros((2, W_LORA, D_RWKV), F32)
    hid = jnp.arange(LANES) // HEAD_DIM
    pad = ROUTE_LANES - N_EXPERTS - N_EXPERT_GROUPS
    return {
        'w_in_att': w_in[:, :D_QKV].astype(BF16),
        'w_in_rw': w_in[:, D_QKV:].astype(BF16),
        'sink': attn_sink.astype(F32),
        'mu': tshift_mu.astype(F32),
        'w0': rwkv_w0.reshape(2, 1, D_RWKV).astype(F32),
        'rw_w_up': jnp.concatenate([rwkv_w_up, zeros_lora], axis=1).astype(BF16),
        'a0': rwkv_a0.reshape(2, 1, D_RWKV).astype(F32),
        'rw_a_up': jnp.concatenate([zeros_lora, rwkv_a_up], axis=1).astype(BF16),
        'g_up': rwkv_g_up.astype(BF16),
        'k_k': row(rwkv_k_k), 'k_a': row(rwkv_k_a), 'r_k': row(rwkv_r_k),
        'gn_g': row(rwkv_gn_g), 'gn_b': row(rwkv_gn_b),
        'bd': (hid[:, None] == hid[None, :]).astype(BF16),
        'w_out': w_out.astype(BF16),
        'ln1_g': row(ln1_g), 'ln1_b': row(ln1_b),
        'mem_ln_g': row(mem_ln_g), 'mem_ln_b': row(mem_ln_b),
        'w_cq': w_cq.astype(BF16), 'w_ckv': w_ckv.astype(BF16), 'w_co': w_co.astype(BF16),
        'ln2_g': row(ln2_g), 'ln2_b': row(ln2_b),
        'w_route': _hi_lo(jnp.pad(jnp.concatenate([w_route_expert, w_route_group], axis=1), ((0, 0), (0, pad)))),
        'b_route': jnp.pad(jnp.concatenate([b_route_expert, b_route_group]), (0, pad)).reshape(1, -1).astype(F32),
        'w_gate': w_exp_gate.astype(BF16), 'w_up': w_exp_up.astype(BF16), 'w_down': w_exp_down.astype(BF16),
        'ln3_g': row(ln3_g), 'ln3_b': row(ln3_b),
    }


def _hi_lo(w):
    w = w.astype(F32)
    hi = w.astype(BF16)
    return jnp.concatenate([hi, (w - hi.astype(F32)).astype(BF16)], axis=-1)


def _tile(n, pref):
    t = pref
    while n % t:
        t //= 2
    return t


def _layer(x, mem, p):
    b, t, _ = x.shape
    n = b * t
    x2d = x.reshape(n, D_MODEL)
    tm = _tile(t, 512)
    qkv, zr = _inproj(x2d, p['w_in_att'], p['w_in_rw'], tm)
    att = _attention(qkv.reshape(b, t, D_QKV), p['sink'])
    zr3 = zr.reshape(b, t, D_RWKV_IN)
    lt = _tile(t, 512)
    y_fwd = _rwkv_dir(zr3, None, p, 0, lt)
    rw = _rwkv_dir(zr3, y_fwd, p, 1, lt)
    kv = _memkv(mem.reshape(b * N_MEM, D_MODEL), p['mem_ln_g'], p['mem_ln_b'], p['w_ckv'])
    tme = _tile(t, 1024)
    sub = _tile(tme, 256)
    win = max(SEG_ALIGN, (3 * sub // 16) // SEG_ALIGN * SEG_ALIGN)
    x2, route, routet, meta = _mid(x2d, att.reshape(n, D_ATT), rw.reshape(n, D_RWKV), kv, p, tme, t, sub, win)
    y = _moe(x2, route, routet, meta, p, tme, sub, win)
    return y.reshape(b, t, D_MODEL)


def kernel(x_prompt, x_sample, mem_prompt, mem_sample, w_in, tshift_mu, attn_sink, rwkv_w0, rwkv_w_up, rwkv_a0, rwkv_a_up, rwkv_g_up, rwkv_k_k, rwkv_k_a, rwkv_r_k, rwkv_gn_g, rwkv_gn_b, w_out, ln1_g, ln1_b, mem_ln_g, mem_ln_b, w_cq, w_ckv, w_co, ln2_g, ln2_b, w_route_group, b_route_group, w_route_expert, b_route_expert, w_exp_gate, w_exp_up, w_exp_down, ln3_g, ln3_b):
    weights = (w_in, tshift_mu, attn_sink, rwkv_w0, rwkv_w_up, rwkv_a0, rwkv_a_up, rwkv_g_up, rwkv_k_k, rwkv_k_a,
               rwkv_r_k, rwkv_gn_g, rwkv_gn_b, w_out, ln1_g, ln1_b, mem_ln_g, mem_ln_b, w_cq, w_ckv, w_co,
               ln2_g, ln2_b, w_route_group, b_route_group, w_route_expert, b_route_expert,
               w_exp_gate, w_exp_up, w_exp_down, ln3_g, ln3_b)
    p = _prep_params(*[w[0] for w in weights])
    return (_layer(x_prompt, mem_prompt, p), _layer(x_sample, mem_sample, p))
```

```python
import functools
import math

import jax
import jax.numpy as jnp
from jax import lax
from jax.experimental import pallas as pl
from jax.experimental.pallas import tpu as pltpu

F32 = jnp.float32
BF16 = jnp.bfloat16
HI = lax.Precision.HIGHEST

D_MODEL = 1024
HEAD_DIM = 64
ATT_Q_HEADS = 8
ATT_KV_HEADS = 2
ATT_GROUP = ATT_Q_HEADS // ATT_KV_HEADS
WINDOW = 128
BLOCK = 128
RWKV_HEADS = 8
D_ATT = ATT_Q_HEADS * HEAD_DIM
D_KV = ATT_KV_HEADS * HEAD_DIM
D_QKV = D_ATT + 2 * D_KV
D_RWKV = RWKV_HEADS * HEAD_DIM
W_LORA = 64
A_LORA = 64
G_LORA = 128
D_RWKV_IN = 3 * D_RWKV + W_LORA + A_LORA + G_LORA
N_MEM = 256
CROSS_HEADS = 4
CROSS_HEAD_DIM = 128
D_CROSS = CROSS_HEADS * CROSS_HEAD_DIM
N_EXPERT_GROUPS = 4
EXPERTS_PER_GROUP = 4
N_EXPERTS = N_EXPERT_GROUPS * EXPERTS_PER_GROUP
D_EXPERT = 512
LN_EPS = 1e-5
GN_EPS = 64e-5
DEEPNORM_ALPHA = 2.0 ** 0.25
NEG_INF = -1e30
LANES = 128
ROUTE_LANES = LANES
DECAY_SCALE = math.exp(-0.5)
CHUNK = 64
SEG_ALIGN = 16
EXPERTS_PER_STEP = 2
VMEM_LIMIT = 56 * 1024 * 1024


def _cparams(*sem):
    return pltpu.CompilerParams(dimension_semantics=sem, vmem_limit_bytes=VMEM_LIMIT)


def _ln(x, g, b):
    mu = jnp.mean(x, -1, keepdims=True)
    xc = x - mu
    var = jnp.mean(xc * xc, -1, keepdims=True)
    return xc * lax.rsqrt(var + LN_EPS) * g + b


def _const_spec(shape):
    nd = len(shape)
    return pl.BlockSpec(shape, lambda *_: (0,) * nd)


def _inproj_kernel(x_ref, wa_ref, wr_ref, qkv_ref, zr_ref):
    xb = x_ref[...].astype(BF16)
    qkv_ref[...] = jnp.dot(xb, wa_ref[...], preferred_element_type=F32).astype(BF16)
    zr_ref[...] = jnp.dot(xb, wr_ref[...], preferred_element_type=F32)


def _inproj(x2d, w_att, w_rw, tm):
    n = x2d.shape[0]
    return pl.pallas_call(
        _inproj_kernel,
        grid=(n // tm,),
        in_specs=[pl.BlockSpec((tm, D_MODEL), lambda i: (i, 0)),
                  _const_spec((D_MODEL, D_QKV)),
                  _const_spec((D_MODEL, D_RWKV_IN))],
        out_specs=[pl.BlockSpec((tm, D_QKV), lambda i: (i, 0)),
                   pl.BlockSpec((tm, D_RWKV_IN), lambda i: (i, 0))],
        out_shape=[jax.ShapeDtypeStruct((n, D_QKV), BF16),
                   jax.ShapeDtypeStruct((n, D_RWKV_IN), F32)],
        compiler_params=_cparams("arbitrary"),
        name="inproj",
    )(x2d, w_att, w_rw)


def _attn_kernel(sink_ref, cur_ref, prv_ref, nxt_ref, o_ref, *, ns, qb):
    n = pl.program_id(1)
    blocks = [prv_ref[0]] + [cur_ref[0, a * BLOCK:(a + 1) * BLOCK, :] for a in range(qb)] + [nxt_ref[0]]
    rows = ATT_GROUP * BLOCK
    ri = lax.broadcasted_iota(jnp.int32, (rows, 3 * BLOCK), 0)
    ki = lax.broadcasted_iota(jnp.int32, (rows, 3 * BLOCK), 1)
    adist = jnp.abs((ri & (BLOCK - 1)) - ki + BLOCK)
    band = adist <= WINDOW
    adist_f = adist.astype(F32)
    grp_id = lax.broadcasted_iota(jnp.int32, (rows, 1), 0) // BLOCK
    biases, sinks = [], []
    for h in range(ATT_KV_HEADS):
        slope = jnp.zeros((rows, 1), F32)
        sink = jnp.zeros((rows, 1), F32)
        for g in range(ATT_GROUP):
            hq = h * ATT_GROUP + g
            slope = jnp.where(grp_id == g, 2.0 ** (-8.0 / ATT_Q_HEADS * (hq + 1)), slope)
            sink = jnp.where(grp_id == g, sink_ref[hq], sink)
        biases.append(slope * adist_f)
        sinks.append(sink)
    for a in range(qb):
        valid = band
        if a == 0:
            valid = valid & ((ki >= BLOCK) | (n > 0))
        if a == qb - 1:
            valid = valid & ((ki < 2 * BLOCK) | (n < ns - 1))
        cur = blocks[a + 1]
        for h in range(ATT_KV_HEADS):
            k0 = D_ATT + h * HEAD_DIM
            v0 = D_ATT + D_KV + h * HEAD_DIM
            kcat = jnp.concatenate([blk[:, k0:k0 + HEAD_DIM] for blk in blocks[a:a + 3]], axis=0)
            vcat = jnp.concatenate([blk[:, v0:v0 + HEAD_DIM] for blk in blocks[a:a + 3]], axis=0)
            q0 = h * ATT_GROUP * HEAD_DIM
            q = jnp.concatenate([cur[:, q0 + g * HEAD_DIM:q0 + (g + 1) * HEAD_DIM] for g in range(ATT_GROUP)], axis=0)
            q = q * (HEAD_DIM ** -0.5)
            s = lax.dot_general(q, kcat, (((1,), (1,)), ((), ())), preferred_element_type=F32)
            s = jnp.where(valid, s - biases[h], NEG_INF)
            sk = sinks[h]
            m = jnp.maximum(jnp.max(s, -1, keepdims=True), sk)
            p = jnp.exp(s - m)
            den = jnp.sum(p, -1, keepdims=True) + jnp.exp(sk - m)
            o = (jnp.dot(p.astype(BF16), vcat, preferred_element_type=F32) / den).astype(BF16)
            for g in range(ATT_GROUP):
                o_ref[0, a * BLOCK:(a + 1) * BLOCK, q0 + g * HEAD_DIM:q0 + (g + 1) * HEAD_DIM] = (
                    o[g * BLOCK:(g + 1) * BLOCK])


def _attention(qkv, sink):
    b, t, _ = qkv.shape
    nb = t // BLOCK
    qb = _tile(nb, 8)
    ns = nb // qb
    return pl.pallas_call(
        functools.partial(_attn_kernel, ns=ns, qb=qb),
        grid=(b, ns),
        in_specs=[pl.BlockSpec(memory_space=pltpu.SMEM),
                  pl.BlockSpec((1, qb * BLOCK, D_QKV), lambda i, j: (i, j, 0)),
                  pl.BlockSpec((1, BLOCK, D_QKV), lambda i, j: (i, jnp.maximum(j * qb - 1, 0), 0)),
                  pl.BlockSpec((1, BLOCK, D_QKV), lambda i, j: (i, jnp.minimum((j + 1) * qb, nb - 1), 0))],
        out_specs=pl.BlockSpec((1, qb * BLOCK, D_ATT), lambda i, j: (i, j, 0)),
        out_shape=jax.ShapeDtypeStruct((b, t, D_ATT), BF16),
        compiler_params=_cparams("arbitrary", "arbitrary"),
        name="win_attn",
    )(sink, qkv, qkv, qkv)


def _bmm(a, b):
    return jnp.einsum('hlj,hjm->hlm', a.astype(BF16), b.astype(BF16), preferred_element_type=F32)


def _split3(x):
    hi = x.astype(BF16)
    r1 = x - hi.astype(F32)
    mid = r1.astype(BF16)
    lo = (r1 - mid.astype(F32)).astype(BF16)
    return hi, mid, lo


def _rwkv_kernel(*refs, reverse, lt, nt, final, grp):
    if final:
        (zc_ref, zp_ref, zn_ref, mu_ref, w0_ref, wup_ref, a0_ref, aup_ref, gup_ref, kk_ref, ka_ref, rk_ref,
         gng_ref, gnb_ref, bd_ref, yf_ref, out_ref,
         st_ref, y_scr, r_s, kd_s, v_s, a_s, b_s, lw_s, bonus_s, gate_s) = refs
    else:
        (zc_ref, zp_ref, zn_ref, mu_ref, w0_ref, wup_ref, a0_ref, aup_ref, kk_ref, ka_ref, bd_ref,
         out_ref, st_ref, y_scr, r_s, kd_s, v_s, a_s, b_s, lw_s) = refs
    i = pl.program_id(1)
    tt = (nt - 1 - i) if reverse else i

    @pl.when(i == 0)
    def _():
        st_ref[...] = jnp.zeros_like(st_ref)

    bd = bd_ref[...]

    def head_sum(x):
        hi = x.astype(BF16)
        lo = (x - hi.astype(F32)).astype(BF16)
        tiles = []
        for t in range(D_RWKV // LANES):
            sl = slice(t * LANES, (t + 1) * LANES)
            tiles.append(jnp.dot(hi[:, sl], bd, preferred_element_type=F32)
                         + jnp.dot(lo[:, sl], bd, preferred_element_type=F32))
        return jnp.concatenate(tiles, axis=-1)

    z = zc_ref[0]
    row = lax.broadcasted_iota(jnp.int32, (lt, 1), 0)
    zprev_edge = jnp.where(tt > 0, zp_ref[0, 7:8, :], 0.0)
    znext_edge = jnp.where(tt < nt - 1, zn_ref[0, 0:1, :], 0.0)
    prev = jnp.where(row == 0, zprev_edge, pltpu.roll(z, 1, 0))
    nxt = jnp.where(row == lt - 1, znext_edge, pltpu.roll(z, lt - 1, 0))
    zs = z + mu_ref[0:1, :] * (prev - z) + mu_ref[1:2, :] * (nxt - z)

    o1, o2, o3 = D_RWKV, 2 * D_RWKV, 3 * D_RWKV
    r = zs[:, :o1]
    k = zs[:, o1:o2]
    v = zs[:, o2:o3]
    wad = zs[:, o3:o3 + W_LORA + A_LORA]
    kk = k * kk_ref[...]
    kk = kk * lax.rsqrt(head_sum(kk * kk) + 1e-12)
    w_arg = w0_ref[...] + jnp.dot(jnp.tanh(wad).astype(BF16), wup_ref[...], preferred_element_type=F32)
    a_sig = jax.nn.sigmoid(a0_ref[...] + jnp.dot(wad.astype(BF16), aup_ref[...], preferred_element_type=F32))
    r_s[...] = r
    v_s[...] = v
    kd_s[...] = k * (1.0 + (a_sig - 1.0) * ka_ref[...])
    a_s[...] = -kk
    b_s[...] = kk * a_sig
    lw_s[...] = (-DECAY_SCALE) * jax.nn.sigmoid(w_arg)
    if final:
        bonus_s[...] = head_sum(r * k * rk_ref[...]) * v
        gd = zs[:, o3 + W_LORA + A_LORA:]
        gate_s[...] = jnp.dot(jax.nn.sigmoid(gd).astype(BF16), gup_ref[...], preferred_element_type=F32)

    span = grp * CHUNK
    pw_ = 2 * HEAD_DIM
    n_pair = RWKV_HEADS // 2
    ri = lax.broadcasted_iota(jnp.int32, (2 * CHUNK, 2 * pw_), 0)
    rj = lax.broadcasted_iota(jnp.int32, (2 * CHUNK, 2 * pw_), 1)
    rt, jt = ri & (CHUNK - 1), rj & (CHUNK - 1)
    if reverse:
        strict, incl = rt < jt, rt <= jt
    else:
        strict, incl = rt > jt, rt >= jt
    pmask = strict | (incl & (ri >= CHUNK))
    oi = lax.broadcasted_iota(jnp.int32, (16, grp * pw_), 0)
    oj = lax.broadcasted_iota(jnp.int32, (16, grp * pw_), 1)
    spread = (oi == (oj // pw_)).astype(BF16)
    ci = lax.broadcasted_iota(jnp.int32, (CHUNK, CHUNK), 0)
    cj = lax.broadcasted_iota(jnp.int32, (CHUNK, CHUNK), 1)
    tri = ((ci <= cj) if reverse else (ci >= cj)).astype(BF16)
    hi_ = lax.broadcasted_iota(jnp.int32, (CHUNK, pw_), 0)
    hj_ = lax.broadcasted_iota(jnp.int32, (CHUNK, pw_), 1)
    left = hj_ < HEAD_DIM
    eye_h = (hi_ == (hj_ & (HEAD_DIM - 1))).astype(F32)
    bi = lax.broadcasted_iota(jnp.int32, (pw_, pw_), 0)
    bj = lax.broadcasted_iota(jnp.int32, (pw_, pw_), 1)
    bdmask = (bi // HEAD_DIM) == (bj // HEAD_DIM)
    nc = lt // span

    def pairs(x):
        return jnp.stack([x[g * CHUNK:(g + 1) * CHUNK, p * pw_:(p + 1) * pw_]
                          for g in range(grp) for p in range(n_pair)], axis=0)

    def stack_mask(x):
        zero = jnp.zeros_like(x)
        return jnp.concatenate([jnp.where(left, x, zero), jnp.where(left, zero, x)], axis=1)

    def precompute(c):
        base = c * span
        rows = pl.ds(base, span)
        lw_c = lw_s[rows, :]
        parts = _split3(lw_c)
        cum = jnp.concatenate(
            [sum(jnp.dot(tri, q[g * CHUNK:(g + 1) * CHUNK], preferred_element_type=F32) for q in parts)
             for g in range(grp)], axis=0)
        g_rows = jnp.exp(cum.reshape(grp, CHUNK, D_RWKV)[:, (0 if reverse else CHUNK - 1), :])
        if spread.shape[0] > grp:
            g_rows = jnp.concatenate([g_rows, jnp.zeros((spread.shape[0] - grp, D_RWKV), F32)], axis=0)
        g_all = sum(lax.dot_general(q, spread, (((0,), (0,)), ((), ())), preferred_element_type=F32)
                    for q in _split3(g_rows)[:2]).reshape(n_pair, pw_, grp * pw_)
        e_pos = jnp.exp(cum)
        e_neg = jnp.exp(-cum)
        e_prev = jnp.exp(cum - lw_c)
        a_p = pairs((a_s[rows, :] * e_prev).astype(BF16))
        r_p = pairs((r_s[rows, :] * e_pos).astype(BF16))
        b_p = pairs((b_s[rows, :] * e_neg).astype(BF16))
        k_p = pairs((kd_s[rows, :] * e_neg).astype(BF16))
        v_p = pairs(v_s[rows, :].astype(BF16))
        ar = jnp.concatenate([a_p, r_p], axis=1)
        bk = jnp.concatenate([b_p, k_p], axis=1)
        bkm = jnp.concatenate([stack_mask(b_p), stack_mask(k_p)], axis=1)
        pw = jnp.einsum('blc,bjc->blj', ar, bkm, preferred_element_type=F32)
        pw = jnp.where(pmask, pw, 0.0)
        pw_b = pw.astype(BF16)
        yield None
        m_ab = pw[:, :CHUNK, :pw_]
        tinv = eye_h + m_ab
        mp = m_ab
        for _ in range(5):
            mp = _bmm(mp, stack_mask(mp.astype(BF16)))
            tinv = tinv + _bmm(tinv, stack_mask(mp.astype(BF16)))
            yield None
        tinv_b = tinv.astype(BF16)
        vm = stack_mask(v_p)
        mv = _bmm(pw_b[:, :CHUNK, pw_:], vm)

        yield base, ar, bk, pw_b, tinv_b, vm, v_p, mv, g_all

    def chain(st, pre):
        base, ar, bk, pw_b, tinv_b, vm, v_p, mv, g_all = pre
        yield None
        for q in (range(grp - 1, -1, -1) if reverse else range(grp)):
            hs = slice(q * n_pair, (q + 1) * n_pair)
            ar_st = _bmm(ar[hs], st)
            u = _bmm(tinv_b[hs], stack_mask((ar_st[:, :CHUNK] + mv[hs]).astype(BF16)))
            u_b = u.astype(BF16)
            y = ar_st[:, CHUNK:] + _bmm(pw_b[hs][:, CHUNK:, :], jnp.concatenate([stack_mask(u_b), vm[hs]], axis=1))
            upd = jnp.einsum('bjc,bjv->bcv', bk[hs], jnp.concatenate([u_b, v_p[hs]], axis=1),
                             preferred_element_type=F32)
            g_col = g_all[:, :, q * pw_:(q + 1) * pw_]
            st = g_col * (st + jnp.where(bdmask, upd, 0.0))
            for p in range(n_pair):
                y_scr[pl.ds(base + q * CHUNK, CHUNK), p * pw_:(p + 1) * pw_] = y[p]
            yield None
        yield st

    def drain(*gens):
        last = [None] * len(gens)
        live = list(range(len(gens)))
        while live:
            for k in list(live):
                try:
                    out = next(gens[k])
                    if out is not None:
                        last[k] = out
                except StopIteration:
                    live.remove(k)
        return last

    order = list(range(nc - 1, -1, -1) if reverse else range(nc))
    pre, = drain(precompute(order[0]))
    st = st_ref[...]
    for c in order[1:]:
        st, pre = drain(chain(st, pre), precompute(c))
    st, = drain(chain(st, pre))
    st_ref[...] = st

    if not final:
        out_ref[0] = y_scr[...]
    else:
        yy = yf_ref[0] + y_scr[...]
        mu_y = head_sum(yy) * (1.0 / HEAD_DIM)
        yc = yy - mu_y
        var_y = head_sum(yc * yc) * (1.0 / HEAD_DIM)
        yn = yc * lax.rsqrt(var_y + GN_EPS) * gng_ref[...] + gnb_ref[...]
        out_ref[0] = ((yn + bonus_s[...]) * gate_s[...]).astype(BF16)


def _rwkv_dir(zr, y_fwd, p, d, lt):
    b, t, _ = zr.shape
    nt = t // lt
    reverse = d == 1
    final = y_fwd is not None
    tmap = (lambda j: nt - 1 - j) if reverse else (lambda j: j)
    r8 = lt // 8
    vec = _const_spec((1, D_RWKV))
    in_specs = [pl.BlockSpec((1, lt, D_RWKV_IN), lambda i, j: (i, tmap(j), 0)),
                pl.BlockSpec((1, 8, D_RWKV_IN), lambda i, j: (i, jnp.maximum(tmap(j) * r8 - 1, 0), 0)),
                pl.BlockSpec((1, 8, D_RWKV_IN), lambda i, j: (i, jnp.minimum((tmap(j) + 1) * r8, t // 8 - 1), 0)),
                _const_spec((2, D_RWKV_IN)), vec, _const_spec((W_LORA + A_LORA, D_RWKV)), vec,
                _const_spec((W_LORA + A_LORA, D_RWKV))]
    args = [zr, zr, zr, p['mu'], p['w0'][d], p['rw_w_up'][d], p['a0'][d], p['rw_a_up'][d]]
    n_tok_scratch = 6
    if final:
        in_specs += [_const_spec((G_LORA, D_RWKV)), vec, vec, vec, vec, vec, _const_spec((LANES, LANES)),
                     pl.BlockSpec((1, lt, D_RWKV), lambda i, j: (i, tmap(j), 0))]
        args += [p['g_up'], p['k_k'], p['k_a'], p['r_k'], p['gn_g'], p['gn_b'], p['bd'], y_fwd]
        n_tok_scratch = 8
    else:
        in_specs += [vec, vec, _const_spec((LANES, LANES))]
        args += [p['k_k'], p['k_a'], p['bd']]
    return pl.pallas_call(
        functools.partial(_rwkv_kernel, reverse=reverse, lt=lt, nt=nt, final=final, grp=min(4, lt // CHUNK)),
        grid=(b, nt),
        in_specs=in_specs,
        out_specs=pl.BlockSpec((1, lt, D_RWKV), lambda i, j: (i, tmap(j), 0)),
        out_shape=jax.ShapeDtypeStruct((b, t, D_RWKV), BF16 if final else F32),
        scratch_shapes=[pltpu.VMEM((RWKV_HEADS // 2, 2 * HEAD_DIM, 2 * HEAD_DIM), F32)]
                       + [pltpu.VMEM((lt, D_RWKV), F32)] * (1 + n_tok_scratch),
        compiler_params=_cparams("arbitrary", "arbitrary"),
        name="rwkv_bwd_final" if final else "rwkv_fwd",
    )(*args)


def _memkv_kernel(m_ref, g_ref, b_ref, w_ref, kv_ref):
    m = _ln(m_ref[...], g_ref[...], b_ref[...])
    kv_ref[...] = jnp.dot(m.astype(BF16), w_ref[...], preferred_element_type=F32).astype(BF16)


def _memkv(mem2d, g, b, w_ckv):
    n = mem2d.shape[0]
    return pl.pallas_call(
        _memkv_kernel,
        grid=(n // N_MEM,),
        in_specs=[pl.BlockSpec((N_MEM, D_MODEL), lambda i: (i, 0)),
                  _const_spec((1, D_MODEL)), _const_spec((1, D_MODEL)),
                  _const_spec((D_MODEL, 2 * D_CROSS))],
        out_specs=pl.BlockSpec((N_MEM, 2 * D_CROSS), lambda i: (i, 0)),
        out_shape=jax.ShapeDtypeStruct((n, 2 * D_CROSS), BF16),
        compiler_params=_cparams("arbitrary"),
        name="mem_kv",
    )(mem2d, g, b, w_ckv)


def _mid_kernel(x_ref, att_ref, rw_ref, kv_ref, wo_ref, l1g_ref, l1b_ref, wq_ref, wco_ref, l2g_ref, l2b_ref,
                wr_ref, br_ref, before_ref, x2_ref, route_ref, routet_ref, meta_ref, *, sub, win):
    mix = (jnp.dot(att_ref[...], wo_ref[:D_ATT, :], preferred_element_type=F32)
           + jnp.dot(rw_ref[...], wo_ref[D_ATT:, :], preferred_element_type=F32))
    x1 = _ln(DEEPNORM_ALPHA * x_ref[...] + mix, l1g_ref[...], l1b_ref[...])

    q = jnp.dot(x1.astype(BF16), wq_ref[...], preferred_element_type=F32)
    q = (q * (CROSS_HEAD_DIM ** -0.5)).astype(BF16)
    kv = kv_ref[...]
    cr = None
    for h in range(CROSS_HEADS):
        hs = slice(h * CROSS_HEAD_DIM, (h + 1) * CROSS_HEAD_DIM)
        kh = kv[:, h * CROSS_HEAD_DIM:(h + 1) * CROSS_HEAD_DIM]
        vh = kv[:, D_CROSS + h * CROSS_HEAD_DIM:D_CROSS + (h + 1) * CROSS_HEAD_DIM]
        s = lax.dot_general(q[:, hs], kh, (((1,), (1,)), ((), ())), preferred_element_type=F32)
        m = jnp.max(s, -1, keepdims=True)
        p = jnp.exp(s - m)
        den = jnp.sum(p, -1, keepdims=True)
        o = jnp.dot(p.astype(BF16), vh, preferred_element_type=F32) / den
        part = jnp.dot(o.astype(BF16), wco_ref[hs, :], preferred_element_type=F32)
        cr = part if cr is None else cr + part
    x2 = _ln(DEEPNORM_ALPHA * x1 + cr, l2g_ref[...], l2b_ref[...])
    x2_ref[...] = x2

    x2_hi = x2.astype(BF16)
    x2_lo = (x2 - x2_hi.astype(F32)).astype(BF16)
    both = jnp.dot(x2_hi, wr_ref[...], preferred_element_type=F32)
    logits = (both[:, :ROUTE_LANES] + both[:, ROUTE_LANES:]
              + jnp.dot(x2_lo, wr_ref[:, :ROUTE_LANES], preferred_element_type=F32)) + br_ref[...]
    lane = lax.broadcasted_iota(jnp.int32, logits.shape, 1)
    is_g = (lane >= N_EXPERTS) & (lane < N_EXPERTS + N_EXPERT_GROUPS)
    lg = jnp.where(is_g, logits, NEG_INF)
    eg = jnp.where(is_g, jnp.exp(lg - jnp.max(lg, -1, keepdims=True)), 0.0)
    pg = eg / jnp.sum(eg, -1, keepdims=True)
    pg_top = jnp.max(pg, -1, keepdims=True)
    gi = jnp.min(jnp.where(is_g & (pg == pg_top), lane, 4 * ROUTE_LANES), -1, keepdims=True) - N_EXPERTS
    in_grp = (lane >= gi * EXPERTS_PER_GROUP) & (lane < (gi + 1) * EXPERTS_PER_GROUP)
    le = jnp.where(in_grp, logits, NEG_INF)
    ee = jnp.where(in_grp, jnp.exp(le - jnp.max(le, -1, keepdims=True)), 0.0)
    pe = ee / jnp.sum(ee, -1, keepdims=True)
    p1 = jnp.max(jnp.where(in_grp, pe, -1.0), -1, keepdims=True)
    i1 = jnp.min(jnp.where(in_grp & (pe == p1), lane, 4 * ROUTE_LANES), -1, keepdims=True)
    rest = in_grp & (lane != i1)
    p2 = jnp.max(jnp.where(rest, pe, -1.0), -1, keepdims=True)
    i2 = jnp.min(jnp.where(rest & (pe == p2), lane, 4 * ROUTE_LANES), -1, keepdims=True)
    tot = p1 + p2
    w1 = p1 / tot * pg_top
    w2 = p2 / tot * pg_top

    tm = logits.shape[0]
    n_sub = tm // sub
    e_lane = lane & (N_EXPERTS - 1)
    sel = ((e_lane == i1) | (e_lane == i2)) & (lane < 2 * N_EXPERTS)
    sel_b = sel.astype(BF16)
    rank = jnp.dot(before_ref[...], sel_b, preferred_element_type=F32)
    si = lax.broadcasted_iota(jnp.int32, (8, tm), 0)
    sj = lax.broadcasted_iota(jnp.int32, (8, tm), 1)
    cnt = jnp.dot(((sj // sub) == si).astype(BF16), sel_b, preferred_element_type=F32)
    padded = jnp.floor((cnt + (SEG_ALIGN - 1)) * (1.0 / SEG_ALIGN)) * SEG_ALIGN
    ui = lax.broadcasted_iota(jnp.int32, (ROUTE_LANES, ROUTE_LANES), 0)
    uj = lax.broadcasted_iota(jnp.int32, (ROUTE_LANES, ROUTE_LANES), 1)
    excl = ((ui < uj) & (uj < N_EXPERTS)).astype(F32)
    lane8 = lax.broadcasted_iota(jnp.int32, (8, ROUTE_LANES), 1)
    off = jnp.dot(jnp.where(lane8 < N_EXPERTS, padded, 0.0), excl, precision=HI, preferred_element_type=F32)
    nblk = jnp.floor((cnt + (win - 0.5)) * (1.0 / win))
    meta_ref[...] = jnp.where(lane8 < N_EXPERTS, off, nblk).astype(jnp.int32)
    sub_id = lax.broadcasted_iota(jnp.int32, (tm, 1), 0) // sub
    off_tok = jnp.zeros_like(rank)
    for s in range(n_sub):
        off_tok = jnp.where(sub_id == s, off[s:s + 1, :], off_tok)
    dest = off_tok + rank
    d1 = jnp.sum(jnp.where(lane == i1, dest, 0.0), -1, keepdims=True)
    d2 = jnp.sum(jnp.where(lane == i2, dest, 0.0), -1, keepdims=True)
    route = (jnp.where(lane == 0, w1, 0.0) + jnp.where(lane == 1, w2, 0.0)
             + jnp.where(lane == 2, d1, 0.0) + jnp.where(lane == 3, d2, 0.0))
    route_ref[...] = route
    pick = (lane8 == lax.broadcasted_iota(jnp.int32, (8, ROUTE_LANES), 0)).astype(F32)
    routet_ref[...] = lax.dot_general(pick, route, (((1,), (1,)), ((), ())), precision=HI,
                                      preferred_element_type=F32)


def _mid(x2d, att2d, rw2d, kv, p, tm, t, sub, win):
    n = x2d.shape[0]
    per_b = t // tm
    nt = n // tm
    vec = _const_spec((1, D_MODEL))
    tok = jnp.arange(tm)
    before = ((tok[None, :] < tok[:, None]) & ((tok[None, :] // sub) == (tok[:, None] // sub))).astype(BF16)
    return pl.pallas_call(
        functools.partial(_mid_kernel, sub=sub, win=win),
        grid=(nt,),
        in_specs=[pl.BlockSpec((tm, D_MODEL), lambda i: (i, 0)),
                  pl.BlockSpec((tm, D_ATT), lambda i: (i, 0)),
                  pl.BlockSpec((tm, D_RWKV), lambda i: (i, 0)),
                  pl.BlockSpec((N_MEM, 2 * D_CROSS), lambda i: (i // per_b, 0)),
                  _const_spec((D_MODEL, D_MODEL)), vec, vec,
                  _const_spec((D_MODEL, D_CROSS)), _const_spec((D_CROSS, D_MODEL)), vec, vec,
                  _const_spec((D_MODEL, 2 * ROUTE_LANES)), _const_spec((1, ROUTE_LANES)), _const_spec((tm, tm))],
        out_specs=[pl.BlockSpec((tm, D_MODEL), lambda i: (i, 0)),
                   pl.BlockSpec((tm, ROUTE_LANES), lambda i: (i, 0)),
                   pl.BlockSpec((8, tm), lambda i: (i, 0)),
                   pl.BlockSpec((8, ROUTE_LANES), lambda i: (i, 0))],
        out_shape=[jax.ShapeDtypeStruct((n, D_MODEL), F32),
                   jax.ShapeDtypeStruct((n, ROUTE_LANES), F32),
                   jax.ShapeDtypeStruct((nt * 8, tm), F32),
                   jax.ShapeDtypeStruct((nt * 8, ROUTE_LANES), jnp.int32)],
        compiler_params=_cparams("arbitrary"),
        name="mid",
    )(x2d, att2d, rw2d, kv, p['w_out'], p['ln1_g'], p['ln1_b'], p['w_cq'], p['w_co'], p['ln2_g'], p['ln2_b'],
      p['w_route'], p['b_route'], before)


def _moe_kernel(meta_ref, x_ref, route_ref, routet_ref, wg_ref, wu_ref, wd_ref, l3g_ref, l3b_ref, o_ref,
                xs_ref, ys_ref, *, sub, win, rows):
    i = pl.program_id(0)
    step = pl.program_id(1)
    n_sub = xs_ref.shape[0]
    dummy = rows - win

    @pl.when(step == 0)
    def _():
        ys_ref[...] = jnp.zeros_like(ys_ref)
        r_id = lax.broadcasted_iota(jnp.int32, (rows, sub), 0).astype(F32)
        for s in range(n_sub):
            d1 = routet_ref[2:3, s * sub:(s + 1) * sub]
            d2 = routet_ref[3:4, s * sub:(s + 1) * sub]
            perm = ((r_id == d1) | (r_id == d2)).astype(BF16)
            xb = x_ref[s * sub:(s + 1) * sub, :].astype(BF16)
            xs_ref[s] = jnp.dot(perm, xb, preferred_element_type=F32).astype(BF16)

    base = i * (n_sub * 32)
    for ee in range(EXPERTS_PER_STEP):
        e = step * EXPERTS_PER_STEP + ee
        offs = [meta_ref[base + s * 32 + e] for s in range(n_sub)]
        nbs = [meta_ref[base + s * 32 + N_EXPERTS + e] for s in range(n_sub)]
        nb_max = functools.reduce(jnp.maximum, nbs)

        def block(j, carry, ee=ee, offs=offs, nbs=nbs):
            starts = [pl.multiple_of(jnp.where(j < nbs[s], offs[s] + j * win, dummy), SEG_ALIGN)
                      for s in range(n_sub)]
            xw = jnp.concatenate([xs_ref[s, pl.ds(starts[s], win), :] for s in range(n_sub)], axis=0)
            gate = jnp.dot(xw, wg_ref[ee], preferred_element_type=F32)
            up = jnp.dot(xw, wu_ref[ee], preferred_element_type=F32)
            hdn = (gate * jax.nn.sigmoid(gate)) * up
            ye = jnp.dot(hdn.astype(BF16), wd_ref[ee], preferred_element_type=F32).astype(BF16)
            for s in range(n_sub):
                ys_ref[s, pl.ds(starts[s], win), :] = ye[s * win:(s + 1) * win]
            return carry

        lax.fori_loop(0, nb_max, block, 0)

    @pl.when(step == N_EXPERTS // EXPERTS_PER_STEP - 1)
    def _():
        c_id = lax.broadcasted_iota(jnp.int32, (sub, rows), 1).astype(F32)
        for s in range(n_sub):
            rt = route_ref[s * sub:(s + 1) * sub, :]
            w1, w2, d1, d2 = rt[:, 0:1], rt[:, 1:2], rt[:, 2:3], rt[:, 3:4]
            comb = jnp.where(c_id == d1, w1, 0.0) + jnp.where(c_id == d2, w2, 0.0)
            hi = comb.astype(BF16)
            lo = (comb - hi.astype(F32)).astype(BF16)
            ysv = ys_ref[s]
            ff = (jnp.dot(hi, ysv, preferred_element_type=F32) + jnp.dot(lo, ysv, preferred_element_type=F32))
            xr = x_ref[s * sub:(s + 1) * sub, :]
            o_ref[s * sub:(s + 1) * sub, :] = _ln(DEEPNORM_ALPHA * xr + ff, l3g_ref[...], l3b_ref[...])


def _moe(x2, route, routet, meta, p, tm, sub, win):
    n = x2.shape[0]
    nt = n // tm
    n_sub = tm // sub
    rows = 2 * sub + N_EXPERTS * SEG_ALIGN + 2 * win
    meta_flat = meta.reshape(nt, 8, ROUTE_LANES)[:, :n_sub, :32].reshape(-1)
    vec = pl.BlockSpec((1, D_MODEL), lambda i, e, m: (0, 0))
    grid_spec = pltpu.PrefetchScalarGridSpec(
        num_scalar_prefetch=1,
        grid=(nt, N_EXPERTS // EXPERTS_PER_STEP),
        in_specs=[pl.BlockSpec((tm, D_MODEL), lambda i, e, m: (i, 0)),
                  pl.BlockSpec((tm, ROUTE_LANES), lambda i, e, m: (i, 0)),
                  pl.BlockSpec((8, tm), lambda i, e, m: (i, 0)),
                  pl.BlockSpec((EXPERTS_PER_STEP, D_MODEL, D_EXPERT), lambda i, e, m: (e, 0, 0)),
                  pl.BlockSpec((EXPERTS_PER_STEP, D_MODEL, D_EXPERT), lambda i, e, m: (e, 0, 0)),
                  pl.BlockSpec((EXPERTS_PER_STEP, D_EXPERT, D_MODEL), lambda i, e, m: (e, 0, 0)),
                  vec, vec],
        out_specs=pl.BlockSpec((tm, D_MODEL), lambda i, e, m: (i, 0)),
        scratch_shapes=[pltpu.VMEM((n_sub, rows, D_MODEL), BF16),
                        pltpu.VMEM((n_sub, rows, D_MODEL), BF16)])
    return pl.pallas_call(
        functools.partial(_moe_kernel, sub=sub, win=win, rows=rows),
        grid_spec=grid_spec,
        out_shape=jax.ShapeDtypeStruct((n, D_MODEL), F32),
        compiler_params=_cparams("arbitrary", "arbitrary"),
        name="moe",
    )(meta_flat, x2, route, routet, p['w_gate'], p['w_up'], p['w_down'], p['ln3_g'], p['ln3_b'])


def _prep_params(w_in, tshift_mu, attn_sink, rwkv_w0, rwkv_w_up, rwkv_a0, rwkv_a_up, rwkv_g_up, rwkv_k_k,
                 rwkv_k_a, rwkv_r_k, rwkv_gn_g, rwkv_gn_b, w_out, ln1_g, ln1_b, mem_ln_g, mem_ln_b, w_cq, w_ckv,
                 w_co, ln2_g, ln2_b, w_route_group, b_route_group, w_route_expert, b_route_expert,
                 w_exp_gate, w_exp_up, w_exp_down, ln3_g, ln3_b):
    row = lambda a: a.reshape(1, -1).astype(F32)
    zeros_lora = jnp.zeros((2, W_LORA, D_RWKV), F32)
    hid = jnp.arange(LANES) // HEAD_DIM
    pad = ROUTE_LANES - N_EXPERTS - N_EXPERT_GROUPS
    return {
        'w_in_att': w_in[:, :D_QKV].astype(BF16),
        'w_in_rw': w_in[:, D_QKV:].astype(BF16),
        'sink': attn_sink.astype(F32),
        'mu': tshift_mu.astype(F32),
        'w0': rwkv_w0.reshape(2, 1, D_RWKV).astype(F32),
        'rw_w_up': jnp.concatenate([rwkv_w_up, zeros_lora], axis=1).astype(BF16),
        'a0': rwkv_a0.reshape(2, 1, D_RWKV).astype(F32),
        'rw_a_up': jnp.concatenate([zeros_lora, rwkv_a_up], axis=1).astype(BF16),
        'g_up': rwkv_g_up.astype(BF16),
        'k_k': row(rwkv_k_k), 'k_a': row(rwkv_k_a), 'r_k': row(rwkv_r_k),
        'gn_g': row(rwkv_gn_g), 'gn_b': row(rwkv_gn_b),
        'bd': (hid[:, None] == hid[None, :]).astype(BF16),
        'w_out': w_out.astype(BF16),
        'ln1_g': row(ln1_g), 'ln1_b': row(ln1_b),
        'mem_ln_g': row(mem_ln_g), 'mem_ln_b': row(mem_ln_b),
        'w_cq': w_cq.astype(BF16), 'w_ckv': w_ckv.astype(BF16), 'w_co': w_co.astype(BF16),
        'ln2_g': row(ln2_g), 'ln2_b': row(ln2_b),
        'w_route': _hi_lo(jnp.pad(jnp.concatenate([w_route_expert, w_route_group], axis=1), ((0, 0), (0, pad)))),
        'b_route': jnp.pad(jnp.concatenate([b_route_expert, b_route_group]), (0, pad)).reshape(1, -1).astype(F32),
        'w_gate': w_exp_gate.astype(BF16), 'w_up': w_exp_up.astype(BF16), 'w_down': w_exp_down.astype(BF16),
        'ln3_g': row(ln3_g), 'ln3_b': row(ln3_b),
    }


def _hi_lo(w):
    w = w.astype(F32)
    hi = w.astype(BF16)
    return jnp.concatenate([hi, (w - hi.astype(F32)).astype(BF16)], axis=-1)


def _tile(n, pref):
    t = pref
    while n % t:
        t //= 2
    return t


def _layer(x, mem, p):
    b, t, _ = x.shape
    n = b * t
    x2d = x.reshape(n, D_MODEL)
    tm = _tile(t, 512)
    qkv, zr = _inproj(x2d, p['w_in_att'], p['w_in_rw'], tm)
    att = _attention(qkv.reshape(b, t, D_QKV), p['sink'])
    zr3 = zr.reshape(b, t, D_RWKV_IN)
    lt = _tile(t, 512)
    y_fwd = _rwkv_dir(zr3, None, p, 0, lt)
    rw = _rwkv_dir(zr3, y_fwd, p, 1, lt)
    kv = _memkv(mem.reshape(b * N_MEM, D_MODEL), p['mem_ln_g'], p['mem_ln_b'], p['w_ckv'])
    tme = _tile(t, 1024)
    sub = _tile(tme, 256)
    win = max(SEG_ALIGN, (3 * sub // 16) // SEG_ALIGN * SEG_ALIGN)
    x2, route, routet, meta = _mid(x2d, att.reshape(n, D_ATT), rw.reshape(n, D_RWKV), kv, p, tme, t, sub, win)
    y = _moe(x2, route, routet, meta, p, tme, sub, win)
    return y.reshape(b, t, D_MODEL)


def kernel(x_prompt, x_sample, mem_prompt, mem_sample, w_in, tshift_mu, attn_sink, rwkv_w0, rwkv_w_up, rwkv_a0, rwkv_a_up, rwkv_g_up, rwkv_k_k, rwkv_k_a, rwkv_r_k, rwkv_gn_g, rwkv_gn_b, w_out, ln1_g, ln1_b, mem_ln_g, mem_ln_b, w_cq, w_ckv, w_co, ln2_g, ln2_b, w_route_group, b_route_group, w_route_expert, b_route_expert, w_exp_gate, w_exp_up, w_exp_down, ln3_g, ln3_b):
    weights = (w_in, tshift_mu, attn_sink, rwkv_w0, rwkv_w_up, rwkv_a0, rwkv_a_up, rwkv_g_up, rwkv_k_k, rwkv_k_a,
               rwkv_r_k, rwkv_gn_g, rwkv_gn_b, w_out, ln1_g, ln1_b, mem_ln_g, mem_ln_b, w_cq, w_ckv, w_co,
               ln2_g, ln2_b, w_route_group, b_route_group, w_route_expert, b_route_expert,
               w_exp_gate, w_exp_up, w_exp_down, ln3_g, ln3_b)
    p = _prep_params(*[w[0] for w in weights])
    return (_layer(x_prompt, mem_prompt, p), _layer(x_sample, mem_sample, p))
```

```python
import functools
import math

import jax
import jax.numpy as jnp
from jax import lax
from jax.experimental import pallas as pl
from jax.experimental.pallas import tpu as pltpu

F32 = jnp.float32
BF16 = jnp.bfloat16
HI = lax.Precision.HIGHEST

D_MODEL = 1024
HEAD_DIM = 64
ATT_Q_HEADS = 8
ATT_KV_HEADS = 2
ATT_GROUP = ATT_Q_HEADS // ATT_KV_HEADS
WINDOW = 128
BLOCK = 128
RWKV_HEADS = 8
D_ATT = ATT_Q_HEADS * HEAD_DIM
D_KV = ATT_KV_HEADS * HEAD_DIM
D_QKV = D_ATT + 2 * D_KV
D_RWKV = RWKV_HEADS * HEAD_DIM
W_LORA = 64
A_LORA = 64
G_LORA = 128
D_RWKV_IN = 3 * D_RWKV + W_LORA + A_LORA + G_LORA
N_MEM = 256
CROSS_HEADS = 4
CROSS_HEAD_DIM = 128
D_CROSS = CROSS_HEADS * CROSS_HEAD_DIM
N_EXPERT_GROUPS = 4
EXPERTS_PER_GROUP = 4
N_EXPERTS = N_EXPERT_GROUPS * EXPERTS_PER_GROUP
D_EXPERT = 512
LN_EPS = 1e-5
GN_EPS = 64e-5
DEEPNORM_ALPHA = 2.0 ** 0.25
NEG_INF = -1e30
LANES = 128
ROUTE_LANES = LANES
DECAY_SCALE = math.exp(-0.5)
CHUNK = 64
SEG_ALIGN = 16
EXPERTS_PER_STEP = 4
VMEM_LIMIT = 60 * 1024 * 1024


def _cparams(*sem):
    return pltpu.CompilerParams(dimension_semantics=sem, vmem_limit_bytes=VMEM_LIMIT)


def _ln(x, g, b):
    mu = jnp.mean(x, -1, keepdims=True)
    xc = x - mu
    var = jnp.mean(xc * xc, -1, keepdims=True)
    return xc * lax.rsqrt(var + LN_EPS) * g + b


def _const_spec(shape):
    nd = len(shape)
    return pl.BlockSpec(shape, lambda *_: (0,) * nd)


def _inproj_kernel(x_ref, wa_ref, wr_ref, qkv_ref, zr_ref):
    xb = x_ref[...].astype(BF16)
    qkv_ref[...] = jnp.dot(xb, wa_ref[...], preferred_element_type=F32).astype(BF16)
    zr_ref[...] = jnp.dot(xb, wr_ref[...], preferred_element_type=F32)


def _inproj(x2d, w_att, w_rw, tm):
    n = x2d.shape[0]
    return pl.pallas_call(
        _inproj_kernel,
        grid=(n // tm,),
        in_specs=[pl.BlockSpec((tm, D_MODEL), lambda i: (i, 0)),
                  _const_spec((D_MODEL, D_QKV)),
                  _const_spec((D_MODEL, D_RWKV_IN))],
        out_specs=[pl.BlockSpec((tm, D_QKV), lambda i: (i, 0)),
                   pl.BlockSpec((tm, D_RWKV_IN), lambda i: (i, 0))],
        out_shape=[jax.ShapeDtypeStruct((n, D_QKV), BF16),
                   jax.ShapeDtypeStruct((n, D_RWKV_IN), F32)],
        compiler_params=_cparams("arbitrary"),
        name="inproj",
    )(x2d, w_att, w_rw)


def _attn_kernel(sink_ref, cur_ref, prv_ref, nxt_ref, o_ref, *, ns, qb):
    n = pl.program_id(1)
    blocks = [prv_ref[0]] + [cur_ref[0, a * BLOCK:(a + 1) * BLOCK, :] for a in range(qb)] + [nxt_ref[0]]
    rows = ATT_GROUP * BLOCK
    ri = lax.broadcasted_iota(jnp.int32, (rows, 3 * BLOCK), 0)
    ki = lax.broadcasted_iota(jnp.int32, (rows, 3 * BLOCK), 1)
    adist = jnp.abs((ri & (BLOCK - 1)) - ki + BLOCK)
    band = adist <= WINDOW
    adist_f = adist.astype(F32)
    grp_id = lax.broadcasted_iota(jnp.int32, (rows, 1), 0) // BLOCK
    biases, sinks = [], []
    for h in range(ATT_KV_HEADS):
        slope = jnp.zeros((rows, 1), F32)
        sink = jnp.zeros((rows, 1), F32)
        for g in range(ATT_GROUP):
            hq = h * ATT_GROUP + g
            slope = jnp.where(grp_id == g, 2.0 ** (-8.0 / ATT_Q_HEADS * (hq + 1)), slope)
            sink = jnp.where(grp_id == g, sink_ref[hq], sink)
        biases.append(slope * adist_f)
        sinks.append(sink)
    for a in range(qb):
        valid = band
        if a == 0:
            valid = valid & ((ki >= BLOCK) | (n > 0))
        if a == qb - 1:
            valid = valid & ((ki < 2 * BLOCK) | (n < ns - 1))
        cur = blocks[a + 1]
        for h in range(ATT_KV_HEADS):
            k0 = D_ATT + h * HEAD_DIM
            v0 = D_ATT + D_KV + h * HEAD_DIM
            kcat = jnp.concatenate([blk[:, k0:k0 + HEAD_DIM] for blk in blocks[a:a + 3]], axis=0)
            vcat = jnp.concatenate([blk[:, v0:v0 + HEAD_DIM] for blk in blocks[a:a + 3]], axis=0)
            q0 = h * ATT_GROUP * HEAD_DIM
            q = jnp.concatenate([cur[:, q0 + g * HEAD_DIM:q0 + (g + 1) * HEAD_DIM] for g in range(ATT_GROUP)], axis=0)
            q = q * (HEAD_DIM ** -0.5)
            s = lax.dot_general(q, kcat, (((1,), (1,)), ((), ())), preferred_element_type=F32)
            s = jnp.where(valid, s - biases[h], NEG_INF)
            sk = sinks[h]
            m = jnp.maximum(jnp.max(s, -1, keepdims=True), sk)
            p = jnp.exp(s - m)
            den = jnp.sum(p, -1, keepdims=True) + jnp.exp(sk - m)
            o = (jnp.dot(p.astype(BF16), vcat, preferred_element_type=F32) / den).astype(BF16)
            for g in range(ATT_GROUP):
                o_ref[0, a * BLOCK:(a + 1) * BLOCK, q0 + g * HEAD_DIM:q0 + (g + 1) * HEAD_DIM] = (
                    o[g * BLOCK:(g + 1) * BLOCK])


def _attention(qkv, sink):
    b, t, _ = qkv.shape
    nb = t // BLOCK
    qb = _tile(nb, 8)
    ns = nb // qb
    return pl.pallas_call(
        functools.partial(_attn_kernel, ns=ns, qb=qb),
        grid=(b, ns),
        in_specs=[pl.BlockSpec(memory_space=pltpu.SMEM),
                  pl.BlockSpec((1, qb * BLOCK, D_QKV), lambda i, j: (i, j, 0)),
                  pl.BlockSpec((1, BLOCK, D_QKV), lambda i, j: (i, jnp.maximum(j * qb - 1, 0), 0)),
                  pl.BlockSpec((1, BLOCK, D_QKV), lambda i, j: (i, jnp.minimum((j + 1) * qb, nb - 1), 0))],
        out_specs=pl.BlockSpec((1, qb * BLOCK, D_ATT), lambda i, j: (i, j, 0)),
        out_shape=jax.ShapeDtypeStruct((b, t, D_ATT), BF16),
        compiler_params=_cparams("arbitrary", "arbitrary"),
        name="win_attn",
    )(sink, qkv, qkv, qkv)


def _bmm(a, b):
    return jnp.einsum('hlj,hjm->hlm', a.astype(BF16), b.astype(BF16), preferred_element_type=F32)


def _split3(x):
    hi = x.astype(BF16)
    r1 = x - hi.astype(F32)
    mid = r1.astype(BF16)
    lo = (r1 - mid.astype(F32)).astype(BF16)
    return hi, mid, lo


def _rwkv_kernel(*refs, reverse, lt, nt, final, grp):
    if final:
        (zc_ref, zp_ref, zn_ref, mu_ref, w0_ref, wup_ref, a0_ref, aup_ref, gup_ref, kk_ref, ka_ref, rk_ref,
         gng_ref, gnb_ref, bd_ref, yf_ref, out_ref,
         st_ref, y_scr, r_s, kd_s, v_s, a_s, b_s, lw_s, bonus_s, gate_s) = refs
    else:
        (zc_ref, zp_ref, zn_ref, mu_ref, w0_ref, wup_ref, a0_ref, aup_ref, kk_ref, ka_ref, bd_ref,
         out_ref, st_ref, y_scr, r_s, kd_s, v_s, a_s, b_s, lw_s) = refs
    i = pl.program_id(1)
    tt = (nt - 1 - i) if reverse else i

    @pl.when(i == 0)
    def _():
        st_ref[...] = jnp.zeros_like(st_ref)

    bd = bd_ref[...]

    def head_sum(x):
        hi = x.astype(BF16)
        lo = (x - hi.astype(F32)).astype(BF16)
        tiles = []
        for t in range(D_RWKV // LANES):
            sl = slice(t * LANES, (t + 1) * LANES)
            tiles.append(jnp.dot(hi[:, sl], bd, preferred_element_type=F32)
                         + jnp.dot(lo[:, sl], bd, preferred_element_type=F32))
        return jnp.concatenate(tiles, axis=-1)

    z = zc_ref[0]
    row = lax.broadcasted_iota(jnp.int32, (lt, 1), 0)
    zprev_edge = jnp.where(tt > 0, zp_ref[0, 7:8, :], 0.0)
    znext_edge = jnp.where(tt < nt - 1, zn_ref[0, 0:1, :], 0.0)
    prev = jnp.where(row == 0, zprev_edge, pltpu.roll(z, 1, 0))
    nxt = jnp.where(row == lt - 1, znext_edge, pltpu.roll(z, lt - 1, 0))
    zs = z + mu_ref[0:1, :] * (prev - z) + mu_ref[1:2, :] * (nxt - z)

    o1, o2, o3 = D_RWKV, 2 * D_RWKV, 3 * D_RWKV
    r = zs[:, :o1]
    k = zs[:, o1:o2]
    v = zs[:, o2:o3]
    wad = zs[:, o3:o3 + W_LORA + A_LORA]
    kk = k * kk_ref[...]
    kk = kk * lax.rsqrt(head_sum(kk * kk) + 1e-12)
    w_arg = w0_ref[...] + jnp.dot(jnp.tanh(wad).astype(BF16), wup_ref[...], preferred_element_type=F32)
    a_sig = jax.nn.sigmoid(a0_ref[...] + jnp.dot(wad.astype(BF16), aup_ref[...], preferred_element_type=F32))
    r_s[...] = r
    v_s[...] = v
    kd_s[...] = k * (1.0 + (a_sig - 1.0) * ka_ref[...])
    a_s[...] = -kk
    b_s[...] = kk * a_sig
    lw_s[...] = (-DECAY_SCALE) * jax.nn.sigmoid(w_arg)
    if final:
        bonus_s[...] = head_sum(r * k * rk_ref[...]) * v
        gd = zs[:, o3 + W_LORA + A_LORA:]
        gate_s[...] = jnp.dot(jax.nn.sigmoid(gd).astype(BF16), gup_ref[...], preferred_element_type=F32)

    span = grp * CHUNK
    pw_ = 2 * HEAD_DIM
    n_pair = RWKV_HEADS // 2
    ri = lax.broadcasted_iota(jnp.int32, (2 * CHUNK, 2 * pw_), 0)
    rj = lax.broadcasted_iota(jnp.int32, (2 * CHUNK, 2 * pw_), 1)
    rt, jt = ri & (CHUNK - 1), rj & (CHUNK - 1)
    if reverse:
        strict, incl = rt < jt, rt <= jt
    else:
        strict, incl = rt > jt, rt >= jt
    pmask = strict | (incl & (ri >= CHUNK))
    oi = lax.broadcasted_iota(jnp.int32, (16, grp * pw_), 0)
    oj = lax.broadcasted_iota(jnp.int32, (16, grp * pw_), 1)
    spread = (oi == (oj // pw_)).astype(BF16)
    ci = lax.broadcasted_iota(jnp.int32, (CHUNK, CHUNK), 0)
    cj = lax.broadcasted_iota(jnp.int32, (CHUNK, CHUNK), 1)
    tri = ((ci <= cj) if reverse else (ci >= cj)).astype(BF16)
    hi_ = lax.broadcasted_iota(jnp.int32, (CHUNK, pw_), 0)
    hj_ = lax.broadcasted_iota(jnp.int32, (CHUNK, pw_), 1)
    left = hj_ < HEAD_DIM
    eye_h = (hi_ == (hj_ & (HEAD_DIM - 1))).astype(F32)
    bi = lax.broadcasted_iota(jnp.int32, (pw_, pw_), 0)
    bj = lax.broadcasted_iota(jnp.int32, (pw_, pw_), 1)
    bdmask = (bi // HEAD_DIM) == (bj // HEAD_DIM)
    nc = lt // span

    def pairs(x):
        return jnp.stack([x[g * CHUNK:(g + 1) * CHUNK, p * pw_:(p + 1) * pw_]
                          for g in range(grp) for p in range(n_pair)], axis=0)

    def stack_mask(x):
        zero = jnp.zeros_like(x)
        return jnp.concatenate([jnp.where(left, x, zero), jnp.where(left, zero, x)], axis=1)

    def precompute(c):
        base = c * span
        rows = pl.ds(base, span)
        lw_c = lw_s[rows, :]
        parts = _split3(lw_c)
        cum = jnp.concatenate(
            [sum(jnp.dot(tri, q[g * CHUNK:(g + 1) * CHUNK], preferred_element_type=F32) for q in parts)
             for g in range(grp)], axis=0)
        g_rows = jnp.exp(cum.reshape(grp, CHUNK, D_RWKV)[:, (0 if reverse else CHUNK - 1), :])
        if spread.shape[0] > grp:
            g_rows = jnp.concatenate([g_rows, jnp.zeros((spread.shape[0] - grp, D_RWKV), F32)], axis=0)
        g_all = sum(lax.dot_general(q, spread, (((0,), (0,)), ((), ())), preferred_element_type=F32)
                    for q in _split3(g_rows)[:2]).reshape(n_pair, pw_, grp * pw_)
        e_pos = jnp.exp(cum)
        e_neg = jnp.exp(-cum)
        e_prev = jnp.exp(cum - lw_c)
        a_p = pairs((a_s[rows, :] * e_prev).astype(BF16))
        r_p = pairs((r_s[rows, :] * e_pos).astype(BF16))
        b_p = pairs((b_s[rows, :] * e_neg).astype(BF16))
        k_p = pairs((kd_s[rows, :] * e_neg).astype(BF16))
        v_p = pairs(v_s[rows, :].astype(BF16))
        ar = jnp.concatenate([a_p, r_p], axis=1)
        bk = jnp.concatenate([b_p, k_p], axis=1)
        bkm = jnp.concatenate([stack_mask(b_p), stack_mask(k_p)], axis=1)
        pw = jnp.einsum('blc,bjc->blj', ar, bkm, preferred_element_type=F32)
        pw = jnp.where(pmask, pw, 0.0)
        pw_b = pw.astype(BF16)
        yield None
        m_ab = pw[:, :CHUNK, :pw_]
        tinv = eye_h + m_ab
        mp = m_ab
        for _ in range(5):
            mp = _bmm(mp, stack_mask(mp.astype(BF16)))
            tinv = tinv + _bmm(tinv, stack_mask(mp.astype(BF16)))
            yield None
        tinv_b = tinv.astype(BF16)
        vm = stack_mask(v_p)
        mv = _bmm(pw_b[:, :CHUNK, pw_:], vm)

        yield base, ar, bk, pw_b, tinv_b, vm, v_p, mv, g_all

    def chain(st, pre):
        base, ar, bk, pw_b, tinv_b, vm, v_p, mv, g_all = pre
        yield None
        for q in (range(grp - 1, -1, -1) if reverse else range(grp)):
            hs = slice(q * n_pair, (q + 1) * n_pair)
            ar_st = _bmm(ar[hs], st)
            u = _bmm(tinv_b[hs], stack_mask((ar_st[:, :CHUNK] + mv[hs]).astype(BF16)))
            u_b = u.astype(BF16)
            y = ar_st[:, CHUNK:] + _bmm(pw_b[hs][:, CHUNK:, :], jnp.concatenate([stack_mask(u_b), vm[hs]], axis=1))
            upd = jnp.einsum('bjc,bjv->bcv', bk[hs], jnp.concatenate([u_b, v_p[hs]], axis=1),
                             preferred_element_type=F32)
            g_col = g_all[:, :, q * pw_:(q + 1) * pw_]
            st = g_col * (st + jnp.where(bdmask, upd, 0.0))
            for p in range(n_pair):
                y_scr[pl.ds(base + q * CHUNK, CHUNK), p * pw_:(p + 1) * pw_] = y[p]
            yield None
        yield st

    def drain(*gens):
        last = [None] * len(gens)
        live = list(range(len(gens)))
        while live:
            for k in list(live):
                try:
                    out = next(gens[k])
                    if out is not None:
                        last[k] = out
                except StopIteration:
                    live.remove(k)
        return last

    order = list(range(nc - 1, -1, -1) if reverse else range(nc))
    pre, = drain(precompute(order[0]))
    st = st_ref[...]
    for c in order[1:]:
        st, pre = drain(chain(st, pre), precompute(c))
    st, = drain(chain(st, pre))
    st_ref[...] = st

    if not final:
        out_ref[0] = y_scr[...]
    else:
        yy = yf_ref[0] + y_scr[...]
        mu_y = head_sum(yy) * (1.0 / HEAD_DIM)
        yc = yy - mu_y
        var_y = head_sum(yc * yc) * (1.0 / HEAD_DIM)
        yn = yc * lax.rsqrt(var_y + GN_EPS) * gng_ref[...] + gnb_ref[...]
        out_ref[0] = ((yn + bonus_s[...]) * gate_s[...]).astype(BF16)


def _rwkv_dir(zr, y_fwd, p, d, lt):
    b, t, _ = zr.shape
    nt = t // lt
    reverse = d == 1
    final = y_fwd is not None
    tmap = (lambda j: nt - 1 - j) if reverse else (lambda j: j)
    r8 = lt // 8
    vec = _const_spec((1, D_RWKV))
    in_specs = [pl.BlockSpec((1, lt, D_RWKV_IN), lambda i, j: (i, tmap(j), 0)),
                pl.BlockSpec((1, 8, D_RWKV_IN), lambda i, j: (i, jnp.maximum(tmap(j) * r8 - 1, 0), 0)),
                pl.BlockSpec((1, 8, D_RWKV_IN), lambda i, j: (i, jnp.minimum((tmap(j) + 1) * r8, t // 8 - 1), 0)),
                _const_spec((2, D_RWKV_IN)), vec, _const_spec((W_LORA + A_LORA, D_RWKV)), vec,
                _const_spec((W_LORA + A_LORA, D_RWKV))]
    args = [zr, zr, zr, p['mu'], p['w0'][d], p['rw_w_up'][d], p['a0'][d], p['rw_a_up'][d]]
    n_tok_scratch = 6
    if final:
        in_specs += [_const_spec((G_LORA, D_RWKV)), vec, vec, vec, vec, vec, _const_spec((LANES, LANES)),
                     pl.BlockSpec((1, lt, D_RWKV), lambda i, j: (i, tmap(j), 0))]
        args += [p['g_up'], p['k_k'], p['k_a'], p['r_k'], p['gn_g'], p['gn_b'], p['bd'], y_fwd]
        n_tok_scratch = 8
    else:
        in_specs += [vec, vec, _const_spec((LANES, LANES))]
        args += [p['k_k'], p['k_a'], p['bd']]
    return pl.pallas_call(
        functools.partial(_rwkv_kernel, reverse=reverse, lt=lt, nt=nt, final=final, grp=min(4, lt // CHUNK)),
        grid=(b, nt),
        in_specs=in_specs,
        out_specs=pl.BlockSpec((1, lt, D_RWKV), lambda i, j: (i, tmap(j), 0)),
        out_shape=jax.ShapeDtypeStruct((b, t, D_RWKV), BF16 if final else F32),
        scratch_shapes=[pltpu.VMEM((RWKV_HEADS // 2, 2 * HEAD_DIM, 2 * HEAD_DIM), F32)]
                       + [pltpu.VMEM((lt, D_RWKV), F32)] * (1 + n_tok_scratch),
        compiler_params=_cparams("arbitrary", "arbitrary"),
        name="rwkv_bwd_final" if final else "rwkv_fwd",
    )(*args)


def _memkv_kernel(m_ref, g_ref, b_ref, w_ref, kv_ref):
    m = _ln(m_ref[...], g_ref[...], b_ref[...])
    kv_ref[...] = jnp.dot(m.astype(BF16), w_ref[...], preferred_element_type=F32).astype(BF16)


def _memkv(mem2d, g, b, w_ckv):
    n = mem2d.shape[0]
    return pl.pallas_call(
        _memkv_kernel,
        grid=(n // N_MEM,),
        in_specs=[pl.BlockSpec((N_MEM, D_MODEL), lambda i: (i, 0)),
                  _const_spec((1, D_MODEL)), _const_spec((1, D_MODEL)),
                  _const_spec((D_MODEL, 2 * D_CROSS))],
        out_specs=pl.BlockSpec((N_MEM, 2 * D_CROSS), lambda i: (i, 0)),
        out_shape=jax.ShapeDtypeStruct((n, 2 * D_CROSS), BF16),
        compiler_params=_cparams("arbitrary"),
        name="mem_kv",
    )(mem2d, g, b, w_ckv)


def _mid_kernel(x_ref, att_ref, rw_ref, kv_ref, wo_ref, l1g_ref, l1b_ref, wq_ref, wco_ref, l2g_ref, l2b_ref,
                wr_ref, br_ref, before_ref, x2_ref, route_ref, routet_ref, meta_ref, *, sub, win):
    mix = (jnp.dot(att_ref[...], wo_ref[:D_ATT, :], preferred_element_type=F32)
           + jnp.dot(rw_ref[...], wo_ref[D_ATT:, :], preferred_element_type=F32))
    x1 = _ln(DEEPNORM_ALPHA * x_ref[...] + mix, l1g_ref[...], l1b_ref[...])

    q = jnp.dot(x1.astype(BF16), wq_ref[...], preferred_element_type=F32)
    q = (q * (CROSS_HEAD_DIM ** -0.5)).astype(BF16)
    kv = kv_ref[...]
    cr = None
    for h in range(CROSS_HEADS):
        hs = slice(h * CROSS_HEAD_DIM, (h + 1) * CROSS_HEAD_DIM)
        kh = kv[:, h * CROSS_HEAD_DIM:(h + 1) * CROSS_HEAD_DIM]
        vh = kv[:, D_CROSS + h * CROSS_HEAD_DIM:D_CROSS + (h + 1) * CROSS_HEAD_DIM]
        s = lax.dot_general(q[:, hs], kh, (((1,), (1,)), ((), ())), preferred_element_type=F32)
        m = jnp.max(s, -1, keepdims=True)
        p = jnp.exp(s - m)
        den = jnp.sum(p, -1, keepdims=True)
        o = jnp.dot(p.astype(BF16), vh, preferred_element_type=F32) / den
        part = jnp.dot(o.astype(BF16), wco_ref[hs, :], preferred_element_type=F32)
        cr = part if cr is None else cr + part
    x2 = _ln(DEEPNORM_ALPHA * x1 + cr, l2g_ref[...], l2b_ref[...])
    x2_ref[...] = x2

    x2_hi = x2.astype(BF16)
    x2_lo = (x2 - x2_hi.astype(F32)).astype(BF16)
    both = jnp.dot(x2_hi, wr_ref[...], preferred_element_type=F32)
    logits = (both[:, :ROUTE_LANES] + both[:, ROUTE_LANES:]
              + jnp.dot(x2_lo, wr_ref[:, :ROUTE_LANES], preferred_element_type=F32)) + br_ref[...]
    lane = lax.broadcasted_iota(jnp.int32, logits.shape, 1)
    is_g = (lane >= N_EXPERTS) & (lane < N_EXPERTS + N_EXPERT_GROUPS)
    lg = jnp.where(is_g, logits, NEG_INF)
    eg = jnp.where(is_g, jnp.exp(lg - jnp.max(lg, -1, keepdims=True)), 0.0)
    pg = eg / jnp.sum(eg, -1, keepdims=True)
    pg_top = jnp.max(pg, -1, keepdims=True)
    gi = jnp.min(jnp.where(is_g & (pg == pg_top), lane, 4 * ROUTE_LANES), -1, keepdims=True) - N_EXPERTS
    in_grp = (lane >= gi * EXPERTS_PER_GROUP) & (lane < (gi + 1) * EXPERTS_PER_GROUP)
    le = jnp.where(in_grp, logits, NEG_INF)
    ee = jnp.where(in_grp, jnp.exp(le - jnp.max(le, -1, keepdims=True)), 0.0)
    pe = ee / jnp.sum(ee, -1, keepdims=True)
    p1 = jnp.max(jnp.where(in_grp, pe, -1.0), -1, keepdims=True)
    i1 = jnp.min(jnp.where(in_grp & (pe == p1), lane, 4 * ROUTE_LANES), -1, keepdims=True)
    rest = in_grp & (lane != i1)
    p2 = jnp.max(jnp.where(rest, pe, -1.0), -1, keepdims=True)
    i2 = jnp.min(jnp.where(rest & (pe == p2), lane, 4 * ROUTE_LANES), -1, keepdims=True)
    tot = p1 + p2
    w1 = p1 / tot * pg_top
    w2 = p2 / tot * pg_top

    tm = logits.shape[0]
    n_sub = tm // sub
    e_lane = lane & (N_EXPERTS - 1)
    sel = ((e_lane == i1) | (e_lane == i2)) & (lane < 2 * N_EXPERTS)
    sel_b = sel.astype(BF16)
    rank = jnp.dot(before_ref[...], sel_b, preferred_element_type=F32)
    si = lax.broadcasted_iota(jnp.int32, (8, tm), 0)
    sj = lax.broadcasted_iota(jnp.int32, (8, tm), 1)
    cnt = jnp.dot(((sj // sub) == si).astype(BF16), sel_b, preferred_element_type=F32)
    padded = jnp.floor((cnt + (SEG_ALIGN - 1)) * (1.0 / SEG_ALIGN)) * SEG_ALIGN
    ui = lax.broadcasted_iota(jnp.int32, (ROUTE_LANES, ROUTE_LANES), 0)
    uj = lax.broadcasted_iota(jnp.int32, (ROUTE_LANES, ROUTE_LANES), 1)
    excl = ((ui < uj) & (uj < N_EXPERTS)).astype(F32)
    lane8 = lax.broadcasted_iota(jnp.int32, (8, ROUTE_LANES), 1)
    off = jnp.dot(jnp.where(lane8 < N_EXPERTS, padded, 0.0), excl, precision=HI, preferred_element_type=F32)
    nblk = jnp.floor((cnt + (win - 0.5)) * (1.0 / win))
    meta_ref[...] = jnp.where(lane8 < N_EXPERTS, off, nblk).astype(jnp.int32)
    sub_id = lax.broadcasted_iota(jnp.int32, (tm, 1), 0) // sub
    off_tok = jnp.zeros_like(rank)
    for s in range(n_sub):
        off_tok = jnp.where(sub_id == s, off[s:s + 1, :], off_tok)
    dest = off_tok + rank
    d1 = jnp.sum(jnp.where(lane == i1, dest, 0.0), -1, keepdims=True)
    d2 = jnp.sum(jnp.where(lane == i2, dest, 0.0), -1, keepdims=True)
    route = (jnp.where(lane == 0, w1, 0.0) + jnp.where(lane == 1, w2, 0.0)
             + jnp.where(lane == 2, d1, 0.0) + jnp.where(lane == 3, d2, 0.0))
    route_ref[...] = route
    pick = (lane8 == lax.broadcasted_iota(jnp.int32, (8, ROUTE_LANES), 0)).astype(F32)
    routet_ref[...] = lax.dot_general(pick, route, (((1,), (1,)), ((), ())), precision=HI,
                                      preferred_element_type=F32)


def _mid(x2d, att2d, rw2d, kv, p, tm, t, sub, win):
    n = x2d.shape[0]
    per_b = t // tm
    nt = n // tm
    vec = _const_spec((1, D_MODEL))
    tok = jnp.arange(tm)
    before = ((tok[None, :] < tok[:, None]) & ((tok[None, :] // sub) == (tok[:, None] // sub))).astype(BF16)
    return pl.pallas_call(
        functools.partial(_mid_kernel, sub=sub, win=win),
        grid=(nt,),
        in_specs=[pl.BlockSpec((tm, D_MODEL), lambda i: (i, 0)),
                  pl.BlockSpec((tm, D_ATT), lambda i: (i, 0)),
                  pl.BlockSpec((tm, D_RWKV), lambda i: (i, 0)),
                  pl.BlockSpec((N_MEM, 2 * D_CROSS), lambda i: (i // per_b, 0)),
                  _const_spec((D_MODEL, D_MODEL)), vec, vec,
                  _const_spec((D_MODEL, D_CROSS)), _const_spec((D_CROSS, D_MODEL)), vec, vec,
                  _const_spec((D_MODEL, 2 * ROUTE_LANES)), _const_spec((1, ROUTE_LANES)), _const_spec((tm, tm))],
        out_specs=[pl.BlockSpec((tm, D_MODEL), lambda i: (i, 0)),
                   pl.BlockSpec((tm, ROUTE_LANES), lambda i: (i, 0)),
                   pl.BlockSpec((8, tm), lambda i: (i, 0)),
                   pl.BlockSpec((8, ROUTE_LANES), lambda i: (i, 0))],
        out_shape=[jax.ShapeDtypeStruct((n, D_MODEL), F32),
                   jax.ShapeDtypeStruct((n, ROUTE_LANES), F32),
                   jax.ShapeDtypeStruct((nt * 8, tm), F32),
                   jax.ShapeDtypeStruct((nt * 8, ROUTE_LANES), jnp.int32)],
        compiler_params=_cparams("arbitrary"),
        name="mid",
    )(x2d, att2d, rw2d, kv, p['w_out'], p['ln1_g'], p['ln1_b'], p['w_cq'], p['w_co'], p['ln2_g'], p['ln2_b'],
      p['w_route'], p['b_route'], before)


def _moe_kernel(meta_ref, x_ref, route_ref, routet_ref, wg_ref, wu_ref, wd_ref, l3g_ref, l3b_ref, o_ref,
                xs_ref, ys_ref, *, sub, win, rows):
    i = pl.program_id(0)
    step = pl.program_id(1)
    n_sub = xs_ref.shape[0]
    dummy = rows - win

    @pl.when(step == 0)
    def _():
        ys_ref[...] = jnp.zeros_like(ys_ref)
        r_id = lax.broadcasted_iota(jnp.int32, (rows, sub), 0).astype(F32)
        for s in range(n_sub):
            d1 = routet_ref[2:3, s * sub:(s + 1) * sub]
            d2 = routet_ref[3:4, s * sub:(s + 1) * sub]
            perm = ((r_id == d1) | (r_id == d2)).astype(BF16)
            xb = x_ref[s * sub:(s + 1) * sub, :].astype(BF16)
            xs_ref[s] = jnp.dot(perm, xb, preferred_element_type=F32).astype(BF16)

    base = i * (n_sub * 32)
    for ee in range(EXPERTS_PER_STEP):
        e = step * EXPERTS_PER_STEP + ee
        offs = [meta_ref[base + s * 32 + e] for s in range(n_sub)]
        nbs = [meta_ref[base + s * 32 + N_EXPERTS + e] for s in range(n_sub)]
        nb_max = functools.reduce(jnp.maximum, nbs)

        def block(j, carry, ee=ee, offs=offs, nbs=nbs):
            starts = [pl.multiple_of(jnp.where(j < nbs[s], offs[s] + j * win, dummy), SEG_ALIGN)
                      for s in range(n_sub)]
            xw = jnp.concatenate([xs_ref[s, pl.ds(starts[s], win), :] for s in range(n_sub)], axis=0)
            gate = jnp.dot(xw, wg_ref[ee], preferred_element_type=F32)
            up = jnp.dot(xw, wu_ref[ee], preferred_element_type=F32)
            hdn = (gate * jax.nn.sigmoid(gate)) * up
            ye = jnp.dot(hdn.astype(BF16), wd_ref[ee], preferred_element_type=F32).astype(BF16)
            for s in range(n_sub):
                ys_ref[s, pl.ds(starts[s], win), :] = ye[s * win:(s + 1) * win]
            return carry

        lax.fori_loop(0, nb_max, block, 0)

    @pl.when(step == N_EXPERTS // EXPERTS_PER_STEP - 1)
    def _():
        c_id = lax.broadcasted_iota(jnp.int32, (sub, rows), 1).astype(F32)
        for s in range(n_sub):
            rt = route_ref[s * sub:(s + 1) * sub, :]
            w1, w2, d1, d2 = rt[:, 0:1], rt[:, 1:2], rt[:, 2:3], rt[:, 3:4]
            comb = jnp.where(c_id == d1, w1, 0.0) + jnp.where(c_id == d2, w2, 0.0)
            hi = comb.astype(BF16)
            lo = (comb - hi.astype(F32)).astype(BF16)
            ysv = ys_ref[s]
            ff = (jnp.dot(hi, ysv, preferred_element_type=F32) + jnp.dot(lo, ysv, preferred_element_type=F32))
            xr = x_ref[s * sub:(s + 1) * sub, :]
            o_ref[s * sub:(s + 1) * sub, :] = _ln(DEEPNORM_ALPHA * xr + ff, l3g_ref[...], l3b_ref[...])


def _moe(x2, route, routet, meta, p, tm, sub, win):
    n = x2.shape[0]
    nt = n // tm
    n_sub = tm // sub
    rows = 2 * sub + N_EXPERTS * SEG_ALIGN + 2 * win
    meta_flat = meta.reshape(nt, 8, ROUTE_LANES)[:, :n_sub, :32].reshape(-1)
    vec = pl.BlockSpec((1, D_MODEL), lambda i, e, m: (0, 0))
    grid_spec = pltpu.PrefetchScalarGridSpec(
        num_scalar_prefetch=1,
        grid=(nt, N_EXPERTS // EXPERTS_PER_STEP),
        in_specs=[pl.BlockSpec((tm, D_MODEL), lambda i, e, m: (i, 0)),
                  pl.BlockSpec((tm, ROUTE_LANES), lambda i, e, m: (i, 0)),
                  pl.BlockSpec((8, tm), lambda i, e, m: (i, 0)),
                  pl.BlockSpec((EXPERTS_PER_STEP, D_MODEL, D_EXPERT), lambda i, e, m: (e, 0, 0)),
                  pl.BlockSpec((EXPERTS_PER_STEP, D_MODEL, D_EXPERT), lambda i, e, m: (e, 0, 0)),
                  pl.BlockSpec((EXPERTS_PER_STEP, D_EXPERT, D_MODEL), lambda i, e, m: (e, 0, 0)),
                  vec, vec],
        out_specs=pl.BlockSpec((tm, D_MODEL), lambda i, e, m: (i, 0)),
        scratch_shapes=[pltpu.VMEM((n_sub, rows, D_MODEL), BF16),
                        pltpu.VMEM((n_sub, rows, D_MODEL), BF16)])
    return pl.pallas_call(
        functools.partial(_moe_kernel, sub=sub, win=win, rows=rows),
        grid_spec=grid_spec,
        out_shape=jax.ShapeDtypeStruct((n, D_MODEL), F32),
        compiler_params=_cparams("arbitrary", "arbitrary"),
        name="moe",
    )(meta_flat, x2, route, routet, p['w_gate'], p['w_up'], p['w_down'], p['ln3_g'], p['ln3_b'])


def _prep_params(w_in, tshift_mu, attn_sink, rwkv_w0, rwkv_w_up, rwkv_a0, rwkv_a_up, rwkv_g_up, rwkv_k_k,
                 rwkv_k_a, rwkv_r_k, rwkv_gn_g, rwkv_gn_b, w_out, ln1_g, ln1_b, mem_ln_g, mem_ln_b, w_cq, w_ckv,
                 w_co, ln2_g, ln2_b, w_route_group, b_route_group, w_route_expert, b_route_expert,
                 w_exp_gate, w_exp_up, w_exp_down, ln3_g, ln3_b):
    row = lambda a: a.reshape(1, -1).astype(F32)
    zeros_lora = jnp.zeros((2, W_LORA, D_RWKV), F32)
    hid = jnp.arange(LANES) // HEAD_DIM
    pad = ROUTE_LANES - N_EXPERTS - N_EXPERT_GROUPS
    return {
        'w_in_att': w_in[:, :D_QKV].astype(BF16),
        'w_in_rw': w_in[:, D_QKV:].astype(BF16),
        'sink': attn_sink.astype(F32),
        'mu': tshift_mu.astype(F32),
        'w0': rwkv_w0.reshape(2, 1, D_RWKV).astype(F32),
        'rw_w_up': jnp.concatenate([rwkv_w_up, zeros_lora], axis=1).astype(BF16),
        'a0': rwkv_a0.reshape(2, 1, D_RWKV).astype(F32),
        'rw_a_up': jnp.concatenate([zeros_lora, rwkv_a_up], axis=1).astype(BF16),
        'g_up': rwkv_g_up.astype(BF16),
        'k_k': row(rwkv_k_k), 'k_a': row(rwkv_k_a), 'r_k': row(rwkv_r_k),
        'gn_g': row(rwkv_gn_g), 'gn_b': row(rwkv_gn_b),
        'bd': (hid[:, None] == hid[None, :]).astype(BF16),
        'w_out': w_out.astype(BF16),
        'ln1_g': row(ln1_g), 'ln1_b': row(ln1_b),
        'mem_ln_g': row(mem_ln_g), 'mem_ln_b': row(mem_ln_b),
        'w_cq': w_cq.astype(BF16), 'w_ckv': w_ckv.astype(BF16), 'w_co': w_co.astype(BF16),
        'ln2_g': row(ln2_g), 'ln2_b': row(ln2_b),
        'w_route': _hi_lo(jnp.pad(jnp.concatenate([w_route_expert, w_route_group], axis=1), ((0, 0), (0, pad)))),
        'b_route': jnp.pad(jnp.concatenate([b_route_expert, b_route_group]), (0, pad)).reshape(1, -1).astype(F32),
        'w_gate': w_exp_gate.astype(BF16), 'w_up': w_exp_up.astype(BF16), 'w_down': w_exp_down.astype(BF16),
        'ln3_g': row(ln3_g), 'ln3_b': row(ln3_b),
    }


def _hi_lo(w):
    w = w.astype(F32)
    hi = w.astype(BF16)
    return jnp.concatenate([hi, (w - hi.astype(F32)).astype(BF16)], axis=-1)


def _tile(n, pref):
    t = pref
    while n % t:
        t //= 2
    return t


def _layer(x, mem, p):
    b, t, _ = x.shape
    n = b * t
    x2d = x.reshape(n, D_MODEL)
    tm = _tile(t, 512)
    qkv, zr = _inproj(x2d, p['w_in_att'], p['w_in_rw'], tm)
    att = _attention(qkv.reshape(b, t, D_QKV), p['sink'])
    zr3 = zr.reshape(b, t, D_RWKV_IN)
    lt = _tile(t, 512)
    y_fwd = _rwkv_dir(zr3, None, p, 0, lt)
    rw = _rwkv_dir(zr3, y_fwd, p, 1, lt)
    kv = _memkv(mem.reshape(b * N_MEM, D_MODEL), p['mem_ln_g'], p['mem_ln_b'], p['w_ckv'])
    tme = _tile(t, 1024)
    sub = _tile(tme, 256)
    win = max(SEG_ALIGN, (3 * sub // 16) // SEG_ALIGN * SEG_ALIGN)
    x2, route, routet, meta = _mid(x2d, att.reshape(n, D_ATT), rw.reshape(n, D_RWKV), kv, p, tme, t, sub, win)
    y = _moe(x2, route, routet, meta, p, tme, sub, win)
    return y.reshape(b, t, D_MODEL)


def kernel(x_prompt, x_sample, mem_prompt, mem_sample, w_in, tshift_mu, attn_sink, rwkv_w0, rwkv_w_up, rwkv_a0, rwkv_a_up, rwkv_g_up, rwkv_k_k, rwkv_k_a, rwkv_r_k, rwkv_gn_g, rwkv_gn_b, w_out, ln1_g, ln1_b, mem_ln_g, mem_ln_b, w_cq, w_ckv, w_co, ln2_g, ln2_b, w_route_group, b_route_group, w_route_expert, b_route_expert, w_exp_gate, w_exp_up, w_exp_down, ln3_g, ln3_b):
    weights = (w_in, tshift_mu, attn_sink, rwkv_w0, rwkv_w_up, rwkv_a0, rwkv_a_up, rwkv_g_up, rwkv_k_k, rwkv_k_a,
               rwkv_r_k, rwkv_gn_g, rwkv_gn_b, w_out, ln1_g, ln1_b, mem_ln_g, mem_ln_b, w_cq, w_ckv, w_co,
               ln2_g, ln2_b, w_route_group, b_route_group, w_route_expert, b_route_expert,
               w_exp_gate, w_exp_up, w_exp_down, ln3_g, ln3_b)
    p = _prep_params(*[w[0] for w in weights])
    return (_layer(x_prompt, mem_prompt, p), _layer(x_sample, mem_sample, p))
```

```python
import functools
import math

import jax
import jax.numpy as jnp
from jax import lax
from jax.experimental import pallas as pl
from jax.experimental.pallas import tpu as pltpu

F32 = jnp.float32
BF16 = jnp.bfloat16
HI = lax.Precision.HIGHEST

D_MODEL = 1024
HEAD_DIM = 64
ATT_Q_HEADS = 8
ATT_KV_HEADS = 2
ATT_GROUP = ATT_Q_HEADS // ATT_KV_HEADS
WINDOW = 128
BLOCK = 128
RWKV_HEADS = 8
D_ATT = ATT_Q_HEADS * HEAD_DIM
D_KV = ATT_KV_HEADS * HEAD_DIM
D_QKV = D_ATT + 2 * D_KV
D_RWKV = RWKV_HEADS * HEAD_DIM
W_LORA = 64
A_LORA = 64
G_LORA = 128
D_RWKV_IN = 3 * D_RWKV + W_LORA + A_LORA + G_LORA
N_MEM = 256
CROSS_HEADS = 4
CROSS_HEAD_DIM = 128
D_CROSS = CROSS_HEADS * CROSS_HEAD_DIM
N_EXPERT_GROUPS = 4
EXPERTS_PER_GROUP = 4
N_EXPERTS = N_EXPERT_GROUPS * EXPERTS_PER_GROUP
D_EXPERT = 512
LN_EPS = 1e-5
GN_EPS = 64e-5
DEEPNORM_ALPHA = 2.0 ** 0.25
NEG_INF = -1e30
LANES = 128
ROUTE_LANES = LANES
DECAY_SCALE = math.exp(-0.5)
CHUNK = 64
SEG_ALIGN = 16
EXPERTS_PER_STEP = 4
VMEM_LIMIT = 60 * 1024 * 1024


def _cparams(*sem):
    return pltpu.CompilerParams(dimension_semantics=sem, vmem_limit_bytes=VMEM_LIMIT)


def _ln(x, g, b):
    mu = jnp.mean(x, -1, keepdims=True)
    xc = x - mu
    var = jnp.mean(xc * xc, -1, keepdims=True)
    return xc * lax.rsqrt(var + LN_EPS) * g + b


def _const_spec(shape):
    nd = len(shape)
    return pl.BlockSpec(shape, lambda *_: (0,) * nd)


def _inproj_kernel(x_ref, wa_ref, wr_ref, qkv_ref, zr_ref):
    xb = x_ref[...].astype(BF16)
    qkv_ref[...] = jnp.dot(xb, wa_ref[...], preferred_element_type=F32).astype(BF16)
    zr_ref[...] = jnp.dot(xb, wr_ref[...], preferred_element_type=F32)


def _inproj(x2d, w_att, w_rw, tm):
    n = x2d.shape[0]
    return pl.pallas_call(
        _inproj_kernel,
        grid=(n // tm,),
        in_specs=[pl.BlockSpec((tm, D_MODEL), lambda i: (i, 0)),
                  _const_spec((D_MODEL, D_QKV)),
                  _const_spec((D_MODEL, D_RWKV_IN))],
        out_specs=[pl.BlockSpec((tm, D_QKV), lambda i: (i, 0)),
                   pl.BlockSpec((tm, D_RWKV_IN), lambda i: (i, 0))],
        out_shape=[jax.ShapeDtypeStruct((n, D_QKV), BF16),
                   jax.ShapeDtypeStruct((n, D_RWKV_IN), F32)],
        compiler_params=_cparams("arbitrary"),
        name="inproj",
    )(x2d, w_att, w_rw)


def _attn_kernel(sink_ref, cur_ref, prv_ref, nxt_ref, o_ref, *, ns, qb):
    n = pl.program_id(1)
    blocks = [prv_ref[0]] + [cur_ref[0, a * BLOCK:(a + 1) * BLOCK, :] for a in range(qb)] + [nxt_ref[0]]
    rows = ATT_GROUP * BLOCK
    ri = lax.broadcasted_iota(jnp.int32, (rows, 3 * BLOCK), 0)
    ki = lax.broadcasted_iota(jnp.int32, (rows, 3 * BLOCK), 1)
    adist = jnp.abs((ri & (BLOCK - 1)) - ki + BLOCK)
    band = adist <= WINDOW
    adist_f = adist.astype(F32)
    grp_id = lax.broadcasted_iota(jnp.int32, (rows, 1), 0) // BLOCK
    biases, sinks = [], []
    for h in range(ATT_KV_HEADS):
        slope = jnp.zeros((rows, 1), F32)
        sink = jnp.zeros((rows, 1), F32)
        for g in range(ATT_GROUP):
            hq = h * ATT_GROUP + g
            slope = jnp.where(grp_id == g, 2.0 ** (-8.0 / ATT_Q_HEADS * (hq + 1)), slope)
            sink = jnp.where(grp_id == g, sink_ref[hq], sink)
        biases.append(slope * adist_f)
        sinks.append(sink)
    for a in range(qb):
        valid = band
        if a == 0:
            valid = valid & ((ki >= BLOCK) | (n > 0))
        if a == qb - 1:
            valid = valid & ((ki < 2 * BLOCK) | (n < ns - 1))
        cur = blocks[a + 1]
        for h in range(ATT_KV_HEADS):
            k0 = D_ATT + h * HEAD_DIM
            v0 = D_ATT + D_KV + h * HEAD_DIM
            kcat = jnp.concatenate([blk[:, k0:k0 + HEAD_DIM] for blk in blocks[a:a + 3]], axis=0)
            vcat = jnp.concatenate([blk[:, v0:v0 + HEAD_DIM] for blk in blocks[a:a + 3]], axis=0)
            q0 = h * ATT_GROUP * HEAD_DIM
            q = jnp.concatenate([cur[:, q0 + g * HEAD_DIM:q0 + (g + 1) * HEAD_DIM] for g in range(ATT_GROUP)], axis=0)
            q = q * (HEAD_DIM ** -0.5)
            s = lax.dot_general(q, kcat, (((1,), (1,)), ((), ())), preferred_element_type=F32)
            s = jnp.where(valid, s - biases[h], NEG_INF)
            sk = sinks[h]
            m = jnp.maximum(jnp.max(s, -1, keepdims=True), sk)
            p = jnp.exp(s - m)
            den = jnp.sum(p, -1, keepdims=True) + jnp.exp(sk - m)
            o = (jnp.dot(p.astype(BF16), vcat, preferred_element_type=F32) / den).astype(BF16)
            for g in range(ATT_GROUP):
                o_ref[0, a * BLOCK:(a + 1) * BLOCK, q0 + g * HEAD_DIM:q0 + (g + 1) * HEAD_DIM] = (
                    o[g * BLOCK:(g + 1) * BLOCK])


def _attention(qkv, sink):
    b, t, _ = qkv.shape
    nb = t // BLOCK
    qb = _tile(nb, 8)
    ns = nb // qb
    return pl.pallas_call(
        functools.partial(_attn_kernel, ns=ns, qb=qb),
        grid=(b, ns),
        in_specs=[pl.BlockSpec(memory_space=pltpu.SMEM),
                  pl.BlockSpec((1, qb * BLOCK, D_QKV), lambda i, j: (i, j, 0)),
                  pl.BlockSpec((1, BLOCK, D_QKV), lambda i, j: (i, jnp.maximum(j * qb - 1, 0), 0)),
                  pl.BlockSpec((1, BLOCK, D_QKV), lambda i, j: (i, jnp.minimum((j + 1) * qb, nb - 1), 0))],
        out_specs=pl.BlockSpec((1, qb * BLOCK, D_ATT), lambda i, j: (i, j, 0)),
        out_shape=jax.ShapeDtypeStruct((b, t, D_ATT), BF16),
        compiler_params=_cparams("arbitrary", "arbitrary"),
        name="win_attn",
    )(sink, qkv, qkv, qkv)


def _bmm(a, b):
    return jnp.einsum('hlj,hjm->hlm', a.astype(BF16), b.astype(BF16), preferred_element_type=F32)


def _split3(x):
    hi = x.astype(BF16)
    r1 = x - hi.astype(F32)
    mid = r1.astype(BF16)
    lo = (r1 - mid.astype(F32)).astype(BF16)
    return hi, mid, lo


def _rwkv_kernel(*refs, reverse, lt, nt, final, grp):
    if final:
        (zc_ref, zp_ref, zn_ref, mu_ref, w0_ref, wup_ref, a0_ref, aup_ref, gup_ref, kk_ref, ka_ref, rk_ref,
         gng_ref, gnb_ref, bd_ref, yf_ref, out_ref,
         st_ref, y_scr, r_s, kd_s, v_s, a_s, b_s, lw_s, bonus_s, gate_s) = refs
    else:
        (zc_ref, zp_ref, zn_ref, mu_ref, w0_ref, wup_ref, a0_ref, aup_ref, kk_ref, ka_ref, bd_ref,
         out_ref, st_ref, y_scr, r_s, kd_s, v_s, a_s, b_s, lw_s) = refs
    i = pl.program_id(1)
    tt = (nt - 1 - i) if reverse else i

    @pl.when(i == 0)
    def _():
        st_ref[...] = jnp.zeros_like(st_ref)

    bd = bd_ref[...]

    def head_sum(x):
        hi = x.astype(BF16)
        lo = (x - hi.astype(F32)).astype(BF16)
        tiles = []
        for t in range(D_RWKV // LANES):
            sl = slice(t * LANES, (t + 1) * LANES)
            tiles.append(jnp.dot(hi[:, sl], bd, preferred_element_type=F32)
                         + jnp.dot(lo[:, sl], bd, preferred_element_type=F32))
        return jnp.concatenate(tiles, axis=-1)

    z = zc_ref[0]
    row = lax.broadcasted_iota(jnp.int32, (lt, 1), 0)
    zprev_edge = jnp.where(tt > 0, zp_ref[0, 7:8, :], 0.0)
    znext_edge = jnp.where(tt < nt - 1, zn_ref[0, 0:1, :], 0.0)
    prev = jnp.where(row == 0, zprev_edge, pltpu.roll(z, 1, 0))
    nxt = jnp.where(row == lt - 1, znext_edge, pltpu.roll(z, lt - 1, 0))
    zs = z + mu_ref[0:1, :] * (prev - z) + mu_ref[1:2, :] * (nxt - z)

    o1, o2, o3 = D_RWKV, 2 * D_RWKV, 3 * D_RWKV
    r = zs[:, :o1]
    k = zs[:, o1:o2]
    v = zs[:, o2:o3]
    wad = zs[:, o3:o3 + W_LORA + A_LORA]
    kk = k * kk_ref[...]
    kk = kk * lax.rsqrt(head_sum(kk * kk) + 1e-12)
    w_arg = w0_ref[...] + jnp.dot(jnp.tanh(wad).astype(BF16), wup_ref[...], preferred_element_type=F32)
    a_sig = jax.nn.sigmoid(a0_ref[...] + jnp.dot(wad.astype(BF16), aup_ref[...], preferred_element_type=F32))
    r_s[...] = r
    v_s[...] = v
    kd_s[...] = k * (1.0 + (a_sig - 1.0) * ka_ref[...])
    a_s[...] = -kk
    b_s[...] = kk * a_sig
    lw_s[...] = (-DECAY_SCALE) * jax.nn.sigmoid(w_arg)
    if final:
        bonus_s[...] = head_sum(r * k * rk_ref[...]) * v
        gd = zs[:, o3 + W_LORA + A_LORA:]
        gate_s[...] = jnp.dot(jax.nn.sigmoid(gd).astype(BF16), gup_ref[...], preferred_element_type=F32)

    span = grp * CHUNK
    pw_ = 2 * HEAD_DIM
    n_pair = RWKV_HEADS // 2
    ri = lax.broadcasted_iota(jnp.int32, (2 * CHUNK, 2 * pw_), 0)
    rj = lax.broadcasted_iota(jnp.int32, (2 * CHUNK, 2 * pw_), 1)
    rt, jt = ri & (CHUNK - 1), rj & (CHUNK - 1)
    if reverse:
        strict, incl = rt < jt, rt <= jt
    else:
        strict, incl = rt > jt, rt >= jt
    pmask = strict | (incl & (ri >= CHUNK))
    oi = lax.broadcasted_iota(jnp.int32, (16, grp * pw_), 0)
    oj = lax.broadcasted_iota(jnp.int32, (16, grp * pw_), 1)
    spread = (oi == (oj // pw_)).astype(BF16)
    ci = lax.broadcasted_iota(jnp.int32, (CHUNK, CHUNK), 0)
    cj = lax.broadcasted_iota(jnp.int32, (CHUNK, CHUNK), 1)
    tri = ((ci <= cj) if reverse else (ci >= cj)).astype(BF16)
    hi_ = lax.broadcasted_iota(jnp.int32, (CHUNK, pw_), 0)
    hj_ = lax.broadcasted_iota(jnp.int32, (CHUNK, pw_), 1)
    left = hj_ < HEAD_DIM
    eye_h = (hi_ == (hj_ & (HEAD_DIM - 1))).astype(F32)
    bi = lax.broadcasted_iota(jnp.int32, (pw_, pw_), 0)
    bj = lax.broadcasted_iota(jnp.int32, (pw_, pw_), 1)
    bdmask = (bi // HEAD_DIM) == (bj // HEAD_DIM)
    nc = lt // span

    def pairs(x):
        return jnp.stack([x[g * CHUNK:(g + 1) * CHUNK, p * pw_:(p + 1) * pw_]
                          for g in range(grp) for p in range(n_pair)], axis=0)

    def stack_mask(x):
        zero = jnp.zeros_like(x)
        return jnp.concatenate([jnp.where(left, x, zero), jnp.where(left, zero, x)], axis=1)

    def precompute(c):
        base = c * span
        rows = pl.ds(base, span)
        lw_c = lw_s[rows, :]
        parts = _split3(lw_c)
        cum = jnp.concatenate(
            [sum(jnp.dot(tri, q[g * CHUNK:(g + 1) * CHUNK], preferred_element_type=F32) for q in parts)
             for g in range(grp)], axis=0)
        g_rows = jnp.exp(cum.reshape(grp, CHUNK, D_RWKV)[:, (0 if reverse else CHUNK - 1), :])
        if spread.shape[0] > grp:
            g_rows = jnp.concatenate([g_rows, jnp.zeros((spread.shape[0] - grp, D_RWKV), F32)], axis=0)
        g_all = sum(lax.dot_general(q, spread, (((0,), (0,)), ((), ())), preferred_element_type=F32)
                    for q in _split3(g_rows)[:2]).reshape(n_pair, pw_, grp * pw_)
        e_pos = jnp.exp(cum)
        e_neg = jnp.exp(-cum)
        e_prev = jnp.exp(cum - lw_c)
        a_p = pairs((a_s[rows, :] * e_prev).astype(BF16))
        r_p = pairs((r_s[rows, :] * e_pos).astype(BF16))
        b_p = pairs((b_s[rows, :] * e_neg).astype(BF16))
        k_p = pairs((kd_s[rows, :] * e_neg).astype(BF16))
        v_p = pairs(v_s[rows, :].astype(BF16))
        ar = jnp.concatenate([a_p, r_p], axis=1)
        bk = jnp.concatenate([b_p, k_p], axis=1)
        bkm = jnp.concatenate([stack_mask(b_p), stack_mask(k_p)], axis=1)
        pw = jnp.einsum('blc,bjc->blj', ar, bkm, preferred_element_type=F32)
        pw = jnp.where(pmask, pw, 0.0)
        pw_b = pw.astype(BF16)
        yield None
        m_ab = pw[:, :CHUNK, :pw_]
        tinv = eye_h + m_ab
        mp = m_ab
        for _ in range(5):
            mp = _bmm(mp, stack_mask(mp.astype(BF16)))
            tinv = tinv + _bmm(tinv, stack_mask(mp.astype(BF16)))
            yield None
        tinv_b = tinv.astype(BF16)
        vm = stack_mask(v_p)
        mv = _bmm(pw_b[:, :CHUNK, pw_:], vm)

        yield base, ar, bk, pw_b, tinv_b, vm, v_p, mv, g_all

    def chain(st, pre):
        base, ar, bk, pw_b, tinv_b, vm, v_p, mv, g_all = pre
        yield None
        for q in (range(grp - 1, -1, -1) if reverse else range(grp)):
            hs = slice(q * n_pair, (q + 1) * n_pair)
            ar_st = _bmm(ar[hs], st)
            u = _bmm(tinv_b[hs], stack_mask((ar_st[:, :CHUNK] + mv[hs]).astype(BF16)))
            u_b = u.astype(BF16)
            y = ar_st[:, CHUNK:] + _bmm(pw_b[hs][:, CHUNK:, :], jnp.concatenate([stack_mask(u_b), vm[hs]], axis=1))
            upd = jnp.einsum('bjc,bjv->bcv', bk[hs], jnp.concatenate([u_b, v_p[hs]], axis=1),
                             preferred_element_type=F32)
            g_col = g_all[:, :, q * pw_:(q + 1) * pw_]
            st = g_col * (st + jnp.where(bdmask, upd, 0.0))
            for p in range(n_pair):
                y_scr[pl.ds(base + q * CHUNK, CHUNK), p * pw_:(p + 1) * pw_] = y[p]
            yield None
        yield st

    def drain(*gens):
        last = [None] * len(gens)
        live = list(range(len(gens)))
        while live:
            for k in list(live):
                try:
                    out = next(gens[k])
                    if out is not None:
                        last[k] = out
                except StopIteration:
                    live.remove(k)
        return last

    order = list(range(nc - 1, -1, -1) if reverse else range(nc))
    pre, = drain(precompute(order[0]))
    st = st_ref[...]
    for c in order[1:]:
        st, pre = drain(chain(st, pre), precompute(c))
    st, = drain(chain(st, pre))
    st_ref[...] = st

    if not final:
        out_ref[0] = y_scr[...]
    else:
        yy = yf_ref[0] + y_scr[...]
        mu_y = head_sum(yy) * (1.0 / HEAD_DIM)
        yc = yy - mu_y
        var_y = head_sum(yc * yc) * (1.0 / HEAD_DIM)
        yn = yc * lax.rsqrt(var_y + GN_EPS) * gng_ref[...] + gnb_ref[...]
        out_ref[0] = ((yn + bonus_s[...]) * gate_s[...]).astype(BF16)


def _rwkv_dir(zr, y_fwd, p, d, lt):
    b, t, _ = zr.shape
    nt = t // lt
    reverse = d == 1
    final = y_fwd is not None
    tmap = (lambda j: nt - 1 - j) if reverse else (lambda j: j)
    r8 = lt // 8
    vec = _const_spec((1, D_RWKV))
    in_specs = [pl.BlockSpec((1, lt, D_RWKV_IN), lambda i, j: (i, tmap(j), 0)),
                pl.BlockSpec((1, 8, D_RWKV_IN), lambda i, j: (i, jnp.maximum(tmap(j) * r8 - 1, 0), 0)),
                pl.BlockSpec((1, 8, D_RWKV_IN), lambda i, j: (i, jnp.minimum((tmap(j) + 1) * r8, t // 8 - 1), 0)),
                _const_spec((2, D_RWKV_IN)), vec, _const_spec((W_LORA + A_LORA, D_RWKV)), vec,
                _const_spec((W_LORA + A_LORA, D_RWKV))]
    args = [zr, zr, zr, p['mu'], p['w0'][d], p['rw_w_up'][d], p['a0'][d], p['rw_a_up'][d]]
    n_tok_scratch = 6
    if final:
        in_specs += [_const_spec((G_LORA, D_RWKV)), vec, vec, vec, vec, vec, _const_spec((LANES, LANES)),
                     pl.BlockSpec((1, lt, D_RWKV), lambda i, j: (i, tmap(j), 0))]
        args += [p['g_up'], p['k_k'], p['k_a'], p['r_k'], p['gn_g'], p['gn_b'], p['bd'], y_fwd]
        n_tok_scratch = 8
    else:
        in_specs += [vec, vec, _const_spec((LANES, LANES))]
        args += [p['k_k'], p['k_a'], p['bd']]
    return pl.pallas_call(
        functools.partial(_rwkv_kernel, reverse=reverse, lt=lt, nt=nt, final=final, grp=min(4, lt // CHUNK)),
        grid=(b, nt),
        in_specs=in_specs,
        out_specs=pl.BlockSpec((1, lt, D_RWKV), lambda i, j: (i, tmap(j), 0)),
        out_shape=jax.ShapeDtypeStruct((b, t, D_RWKV), BF16 if final else F32),
        scratch_shapes=[pltpu.VMEM((RWKV_HEADS // 2, 2 * HEAD_DIM, 2 * HEAD_DIM), F32)]
                       + [pltpu.VMEM((lt, D_RWKV), F32)] * (1 + n_tok_scratch),
        compiler_params=_cparams("arbitrary", "arbitrary"),
        name="rwkv_bwd_final" if final else "rwkv_fwd",
    )(*args)


def _memkv_kernel(m_ref, g_ref, b_ref, w_ref, kv_ref):
    m = _ln(m_ref[...], g_ref[...], b_ref[...])
    kv_ref[...] = jnp.dot(m.astype(BF16), w_ref[...], preferred_element_type=F32).astype(BF16)


def _memkv(mem2d, g, b, w_ckv):
    n = mem2d.shape[0]
    return pl.pallas_call(
        _memkv_kernel,
        grid=(n // N_MEM,),
        in_specs=[pl.BlockSpec((N_MEM, D_MODEL), lambda i: (i, 0)),
                  _const_spec((1, D_MODEL)), _const_spec((1, D_MODEL)),
                  _const_spec((D_MODEL, 2 * D_CROSS))],
        out_specs=pl.BlockSpec((N_MEM, 2 * D_CROSS), lambda i: (i, 0)),
        out_shape=jax.ShapeDtypeStruct((n, 2 * D_CROSS), BF16),
        compiler_params=_cparams("arbitrary"),
        name="mem_kv",
    )(mem2d, g, b, w_ckv)


def _mid_kernel(x_ref, att_ref, rw_ref, kv_ref, wo_ref, l1g_ref, l1b_ref, wq_ref, wco_ref, l2g_ref, l2b_ref,
                wr_ref, br_ref, before_ref, x2_ref, route_ref, routet_ref, meta_ref, *, sub, win):
    mix = (jnp.dot(att_ref[...], wo_ref[:D_ATT, :], preferred_element_type=F32)
           + jnp.dot(rw_ref[...], wo_ref[D_ATT:, :], preferred_element_type=F32))
    x1 = _ln(DEEPNORM_ALPHA * x_ref[...] + mix, l1g_ref[...], l1b_ref[...])

    q = jnp.dot(x1.astype(BF16), wq_ref[...], preferred_element_type=F32)
    q = (q * (CROSS_HEAD_DIM ** -0.5)).astype(BF16)
    kv = kv_ref[...]
    cr = None
    for h in range(CROSS_HEADS):
        hs = slice(h * CROSS_HEAD_DIM, (h + 1) * CROSS_HEAD_DIM)
        kh = kv[:, h * CROSS_HEAD_DIM:(h + 1) * CROSS_HEAD_DIM]
        vh = kv[:, D_CROSS + h * CROSS_HEAD_DIM:D_CROSS + (h + 1) * CROSS_HEAD_DIM]
        s = lax.dot_general(q[:, hs], kh, (((1,), (1,)), ((), ())), preferred_element_type=F32)
        m = jnp.max(s, -1, keepdims=True)
        p = jnp.exp(s - m)
        den = jnp.sum(p, -1, keepdims=True)
        o = jnp.dot(p.astype(BF16), vh, preferred_element_type=F32) / den
        part = jnp.dot(o.astype(BF16), wco_ref[hs, :], preferred_element_type=F32)
        cr = part if cr is None else cr + part
    x2 = _ln(DEEPNORM_ALPHA * x1 + cr, l2g_ref[...], l2b_ref[...])
    x2_ref[...] = x2

    x2_hi = x2.astype(BF16)
    x2_lo = (x2 - x2_hi.astype(F32)).astype(BF16)
    both = jnp.dot(x2_hi, wr_ref[...], preferred_element_type=F32)
    logits = (both[:, :ROUTE_LANES] + both[:, ROUTE_LANES:]
              + jnp.dot(x2_lo, wr_ref[:, :ROUTE_LANES], preferred_element_type=F32)) + br_ref[...]
    lane = lax.broadcasted_iota(jnp.int32, logits.shape, 1)
    is_g = (lane >= N_EXPERTS) & (lane < N_EXPERTS + N_EXPERT_GROUPS)
    lg = jnp.where(is_g, logits, NEG_INF)
    eg = jnp.where(is_g, jnp.exp(lg - jnp.max(lg, -1, keepdims=True)), 0.0)
    pg = eg / jnp.sum(eg, -1, keepdims=True)
    pg_top = jnp.max(pg, -1, keepdims=True)
    gi = jnp.min(jnp.where(is_g & (pg == pg_top), lane, 4 * ROUTE_LANES), -1, keepdims=True) - N_EXPERTS
    in_grp = (lane >= gi * EXPERTS_PER_GROUP) & (lane < (gi + 1) * EXPERTS_PER_GROUP)
    le = jnp.where(in_grp, logits, NEG_INF)
    ee = jnp.where(in_grp, jnp.exp(le - jnp.max(le, -1, keepdims=True)), 0.0)
    pe = ee / jnp.sum(ee, -1, keepdims=True)
    p1 = jnp.max(jnp.where(in_grp, pe, -1.0), -1, keepdims=True)
    i1 = jnp.min(jnp.where(in_grp & (pe == p1), lane, 4 * ROUTE_LANES), -1, keepdims=True)
    rest = in_grp & (lane != i1)
    p2 = jnp.max(jnp.where(rest, pe, -1.0), -1, keepdims=True)
    i2 = jnp.min(jnp.where(rest & (pe == p2), lane, 4 * ROUTE_LANES), -1, keepdims=True)
    tot = p1 + p2
    w1 = p1 / tot * pg_top
    w2 = p2 / tot * pg_top

    tm = logits.shape[0]
    n_sub = tm // sub
    e_lane = lane & (N_EXPERTS - 1)
    sel = ((e_lane == i1) | (e_lane == i2)) & (lane < 2 * N_EXPERTS)
    sel_b = sel.astype(BF16)
    rank = jnp.dot(before_ref[...], sel_b, preferred_element_type=F32)
    si = lax.broadcasted_iota(jnp.int32, (8, tm), 0)
    sj = lax.broadcasted_iota(jnp.int32, (8, tm), 1)
    cnt = jnp.dot(((sj // sub) == si).astype(BF16), sel_b, preferred_element_type=F32)
    padded = jnp.floor((cnt + (SEG_ALIGN - 1)) * (1.0 / SEG_ALIGN)) * SEG_ALIGN
    ui = lax.broadcasted_iota(jnp.int32, (ROUTE_LANES, ROUTE_LANES), 0)
    uj = lax.broadcasted_iota(jnp.int32, (ROUTE_LANES, ROUTE_LANES), 1)
    excl = ((ui < uj) & (uj < N_EXPERTS)).astype(F32)
    lane8 = lax.broadcasted_iota(jnp.int32, (8, ROUTE_LANES), 1)
    off = jnp.dot(jnp.where(lane8 < N_EXPERTS, padded, 0.0), excl, precision=HI, preferred_element_type=F32)
    nblk = jnp.floor((cnt + (win - 0.5)) * (1.0 / win))
    meta_ref[...] = jnp.where(lane8 < N_EXPERTS, off, nblk).astype(jnp.int32)
    sub_id = lax.broadcasted_iota(jnp.int32, (tm, 1), 0) // sub
    off_tok = jnp.zeros_like(rank)
    for s in range(n_sub):
        off_tok = jnp.where(sub_id == s, off[s:s + 1, :], off_tok)
    dest = off_tok + rank
    d1 = jnp.sum(jnp.where(lane == i1, dest, 0.0), -1, keepdims=True)
    d2 = jnp.sum(jnp.where(lane == i2, dest, 0.0), -1, keepdims=True)
    route = (jnp.where(lane == 0, w1, 0.0) + jnp.where(lane == 1, w2, 0.0)
             + jnp.where(lane == 2, d1, 0.0) + jnp.where(lane == 3, d2, 0.0))
    route_ref[...] = route
    pick = (lane8 == lax.broadcasted_iota(jnp.int32, (8, ROUTE_LANES), 0)).astype(F32)
    routet_ref[...] = lax.dot_general(pick, route, (((1,), (1,)), ((), ())), precision=HI,
                                      preferred_element_type=F32)


def _mid(x2d, att2d, rw2d, kv, p, tm, t, sub, win):
    n = x2d.shape[0]
    per_b = t // tm
    nt = n // tm
    vec = _const_spec((1, D_MODEL))
    tok = jnp.arange(tm)
    before = ((tok[None, :] < tok[:, None]) & ((tok[None, :] // sub) == (tok[:, None] // sub))).astype(BF16)
    return pl.pallas_call(
        functools.partial(_mid_kernel, sub=sub, win=win),
        grid=(nt,),
        in_specs=[pl.BlockSpec((tm, D_MODEL), lambda i: (i, 0)),
                  pl.BlockSpec((tm, D_ATT), lambda i: (i, 0)),
                  pl.BlockSpec((tm, D_RWKV), lambda i: (i, 0)),
                  pl.BlockSpec((N_MEM, 2 * D_CROSS), lambda i: (i // per_b, 0)),
                  _const_spec((D_MODEL, D_MODEL)), vec, vec,
                  _const_spec((D_MODEL, D_CROSS)), _const_spec((D_CROSS, D_MODEL)), vec, vec,
                  _const_spec((D_MODEL, 2 * ROUTE_LANES)), _const_spec((1, ROUTE_LANES)), _const_spec((tm, tm))],
        out_specs=[pl.BlockSpec((tm, D_MODEL), lambda i: (i, 0)),
                   pl.BlockSpec((tm, ROUTE_LANES), lambda i: (i, 0)),
                   pl.BlockSpec((8, tm), lambda i: (i, 0)),
                   pl.BlockSpec((8, ROUTE_LANES), lambda i: (i, 0))],
        out_shape=[jax.ShapeDtypeStruct((n, D_MODEL), F32),
                   jax.ShapeDtypeStruct((n, ROUTE_LANES), F32),
                   jax.ShapeDtypeStruct((nt * 8, tm), F32),
                   jax.ShapeDtypeStruct((nt * 8, ROUTE_LANES), jnp.int32)],
        compiler_params=_cparams("arbitrary"),
        name="mid",
    )(x2d, att2d, rw2d, kv, p['w_out'], p['ln1_g'], p['ln1_b'], p['w_cq'], p['w_co'], p['ln2_g'], p['ln2_b'],
      p['w_route'], p['b_route'], before)


def _moe_kernel(meta_ref, x_ref, route_ref, routet_ref, wg_ref, wu_ref, wd_ref, l3g_ref, l3b_ref, o_ref,
                xs_ref, ys_ref, *, sub, win, rows):
    i = pl.program_id(0)
    step = pl.program_id(1)
    n_sub = xs_ref.shape[0]
    dummy = rows - win

    @pl.when(step == 0)
    def _():
        ys_ref[...] = jnp.zeros_like(ys_ref)
        r_id = lax.broadcasted_iota(jnp.int32, (rows, sub), 0).astype(F32)
        for s in range(n_sub):
            d1 = routet_ref[2:3, s * sub:(s + 1) * sub]
            d2 = routet_ref[3:4, s * sub:(s + 1) * sub]
            perm = ((r_id == d1) | (r_id == d2)).astype(BF16)
            xb = x_ref[s * sub:(s + 1) * sub, :].astype(BF16)
            xs_ref[s] = jnp.dot(perm, xb, preferred_element_type=F32).astype(BF16)

    base = i * (n_sub * 32)
    for ee in range(EXPERTS_PER_STEP):
        e = step * EXPERTS_PER_STEP + ee
        offs = [meta_ref[base + s * 32 + e] for s in range(n_sub)]
        nbs = [meta_ref[base + s * 32 + N_EXPERTS + e] for s in range(n_sub)]
        nb_max = functools.reduce(jnp.maximum, nbs)

        def block(j, carry, ee=ee, offs=offs, nbs=nbs):
            starts = [pl.multiple_of(jnp.where(j < nbs[s], offs[s] + j * win, dummy), SEG_ALIGN)
                      for s in range(n_sub)]
            xw = jnp.concatenate([xs_ref[s, pl.ds(starts[s], win), :] for s in range(n_sub)], axis=0)
            gate = jnp.dot(xw, wg_ref[ee], preferred_element_type=F32)
            up = jnp.dot(xw, wu_ref[ee], preferred_element_type=F32)
            hdn = (gate * jax.nn.sigmoid(gate)) * up
            ye = jnp.dot(hdn.astype(BF16), wd_ref[ee], preferred_element_type=F32).astype(BF16)
            for s in range(n_sub):
                ys_ref[s, pl.ds(starts[s], win), :] = ye[s * win:(s + 1) * win]
            return carry

        lax.fori_loop(0, nb_max, block, 0)

    @pl.when(step == N_EXPERTS // EXPERTS_PER_STEP - 1)
    def _():
        c_id = lax.broadcasted_iota(jnp.int32, (sub, rows), 1).astype(F32)
        for s in range(n_sub):
            rt = route_ref[s * sub:(s + 1) * sub, :]
            w1, w2, d1, d2 = rt[:, 0:1], rt[:, 1:2], rt[:, 2:3], rt[:, 3:4]
            comb = jnp.where(c_id == d1, w1, 0.0) + jnp.where(c_id == d2, w2, 0.0)
            hi = comb.astype(BF16)
            lo = (comb - hi.astype(F32)).astype(BF16)
            ysv = ys_ref[s]
            ff = (jnp.dot(hi, ysv, preferred_element_type=F32) + jnp.dot(lo, ysv, preferred_element_type=F32))
            xr = x_ref[s * sub:(s + 1) * sub, :]
            o_ref[s * sub:(s + 1) * sub, :] = _ln(DEEPNORM_ALPHA * xr + ff, l3g_ref[...], l3b_ref[...])


def _moe(x2, route, routet, meta, p, tm, sub, win):
    n = x2.shape[0]
    nt = n // tm
    n_sub = tm // sub
    rows = 2 * sub + N_EXPERTS * SEG_ALIGN + 2 * win
    meta_flat = meta.reshape(nt, 8, ROUTE_LANES)[:, :n_sub, :32].reshape(-1)
    vec = pl.BlockSpec((1, D_MODEL), lambda i, e, m: (0, 0))
    grid_spec = pltpu.PrefetchScalarGridSpec(
        num_scalar_prefetch=1,
        grid=(nt, N_EXPERTS // EXPERTS_PER_STEP),
        in_specs=[pl.BlockSpec((tm, D_MODEL), lambda i, e, m: (i, 0)),
                  pl.BlockSpec((tm, ROUTE_LANES), lambda i, e, m: (i, 0)),
                  pl.BlockSpec((8, tm), lambda i, e, m: (i, 0)),
                  pl.BlockSpec((EXPERTS_PER_STEP, D_MODEL, D_EXPERT), lambda i, e, m: (e, 0, 0)),
                  pl.BlockSpec((EXPERTS_PER_STEP, D_MODEL, D_EXPERT), lambda i, e, m: (e, 0, 0)),
                  pl.BlockSpec((EXPERTS_PER_STEP, D_EXPERT, D_MODEL), lambda i, e, m: (e, 0, 0)),
                  vec, vec],
        out_specs=pl.BlockSpec((tm, D_MODEL), lambda i, e, m: (i, 0)),
        scratch_shapes=[pltpu.VMEM((n_sub, rows, D_MODEL), BF16),
                        pltpu.VMEM((n_sub, rows, D_MODEL), BF16)])
    return pl.pallas_call(
        functools.partial(_moe_kernel, sub=sub, win=win, rows=rows),
        grid_spec=grid_spec,
        out_shape=jax.ShapeDtypeStruct((n, D_MODEL), F32),
        compiler_params=_cparams("arbitrary", "arbitrary"),
        name="moe",
    )(meta_flat, x2, route, routet, p['w_gate'], p['w_up'], p['w_down'], p['ln3_g'], p['ln3_b'])


def _prep_params(w_in, tshift_mu, attn_sink, rwkv_w0, rwkv_w_up, rwkv_a0, rwkv_a_up, rwkv_g_up, rwkv_k_k,
                 rwkv_k_a, rwkv_r_k, rwkv_gn_g, rwkv_gn_b, w_out, ln1_g, ln1_b, mem_ln_g, mem_ln_b, w_cq, w_ckv,
                 w_co, ln2_g, ln2_b, w_route_group, b_route_group, w_route_expert, b_route_expert,
                 w_exp_gate, w_exp_up, w_exp_down, ln3_g, ln3_b):
    row = lambda a: a.reshape(1, -1).astype(F32)
    zeros_lora = jnp.zeros((2, W_LORA, D_RWKV), F32)
    hid = jnp.arange(LANES) // HEAD_DIM
    pad = ROUTE_LANES - N_EXPERTS - N_EXPERT_GROUPS
    return {
        'w_in_att': w_in[:, :D_QKV].astype(BF16),
        'w_in_rw': w_in[:, D_QKV:].astype(BF16),
        'sink': attn_sink.astype(F32),
        'mu': tshift_mu.astype(F32),
        'w0': rwkv_w0.reshape(2, 1, D_RWKV).astype(F32),
        'rw_w_up': jnp.concatenate([rwkv_w_up, zeros_lora], axis=1).astype(BF16),
        'a0': rwkv_a0.reshape(2, 1, D_RWKV).astype(F32),
        'rw_a_up': jnp.concatenate([zeros_lora, rwkv_a_up], axis=1).astype(BF16),
        'g_up': rwkv_g_up.astype(BF16),
        'k_k': row(rwkv_k_k), 'k_a': row(rwkv_k_a), 'r_k': row(rwkv_r_k),
        'gn_g': row(rwkv_gn_g), 'gn_b': row(rwkv_gn_b),
        'bd': (hid[:, None] == hid[None, :]).astype(BF16),
        'w_out': w_out.astype(BF16),
        'ln1_g': row(ln1_g), 'ln1_b': row(ln1_b),
        'mem_ln_g': row(mem_ln_g), 'mem_ln_b': row(mem_ln_b),
        'w_cq': w_cq.astype(BF16), 'w_ckv': w_ckv.astype(BF16), 'w_co': w_co.astype(BF16),
        'ln2_g': row(ln2_g), 'ln2_b': row(ln2_b),
        'w_route': _hi_lo(jnp.pad(jnp.concatenate([w_route_expert, w_route_group], axis=1), ((0, 0), (0, pad)))),
        'b_route': jnp.pad(jnp.concatenate([b_route_expert, b_route_group]), (0, pad)).reshape(1, -1).astype(F32),
        'w_gate': w_exp_gate.astype(BF16), 'w_up': w_exp_up.astype(BF16), 'w_down': w_exp_down.astype(BF16),
        'ln3_g': row(ln3_g), 'ln3_b': row(ln3_b),
    }


def _hi_lo(w):
    w = w.astype(F32)
    hi = w.astype(BF16)
    return jnp.concatenate([hi, (w - hi.astype(F32)).astype(BF16)], axis=-1)


def _tile(n, pref):
    t = pref
    while n % t:
        t //= 2
    return t


def _layer(x, mem, p):
    b, t, _ = x.shape
    n = b * t
    x2d = x.reshape(n, D_MODEL)
    tm = _tile(t, 512)
    qkv, zr = _inproj(x2d, p['w_in_att'], p['w_in_rw'], tm)
    att = _attention(qkv.reshape(b, t, D_QKV), p['sink'])
    zr3 = zr.reshape(b, t, D_RWKV_IN)
    lt = _tile(t, 1024)
    y_fwd = _rwkv_dir(zr3, None, p, 0, lt)
    rw = _rwkv_dir(zr3, y_fwd, p, 1, lt)
    kv = _memkv(mem.reshape(b * N_MEM, D_MODEL), p['mem_ln_g'], p['mem_ln_b'], p['w_ckv'])
    tme = _tile(t, 1024)
    sub = _tile(tme, 256)
    win = max(SEG_ALIGN, (3 * sub // 16) // SEG_ALIGN * SEG_ALIGN)
    x2, route, routet, meta = _mid(x2d, att.reshape(n, D_ATT), rw.reshape(n, D_RWKV), kv, p, tme, t, sub, win)
    y = _moe(x2, route, routet, meta, p, tme, sub, win)
    return y.reshape(b, t, D_MODEL)


def kernel(x_prompt, x_sample, mem_prompt, mem_sample, w_in, tshift_mu, attn_sink, rwkv_w0, rwkv_w_up, rwkv_a0, rwkv_a_up, rwkv_g_up, rwkv_k_k, rwkv_k_a, rwkv_r_k, rwkv_gn_g, rwkv_gn_b, w_out, ln1_g, ln1_b, mem_ln_g, mem_ln_b, w_cq, w_ckv, w_co, ln2_g, ln2_b, w_route_group, b_route_group, w_route_expert, b_route_expert, w_exp_gate, w_exp_up, w_exp_down, ln3_g, ln3_b):
    weights = (w_in, tshift_mu, attn_sink, rwkv_w0, rwkv_w_up, rwkv_a0, rwkv_a_up, rwkv_g_up, rwkv_k_k, rwkv_k_a,
               rwkv_r_k, rwkv_gn_g, rwkv_gn_b, w_out, ln1_g, ln1_b, mem_ln_g, mem_ln_b, w_cq, w_ckv, w_co,
               ln2_g, ln2_b, w_route_group, b_route_group, w_route_expert, b_route_expert,
               w_exp_gate, w_exp_up, w_exp_down, ln3_g, ln3_b)
    p = _prep_params(*[w[0] for w in weights])
    return (_layer(x_prompt, mem_prompt, p), _layer(x_sample, mem_sample, p))
```
